```python
import functools
import jax, jax.numpy as jnp
from jax import lax
import numpy as np


D_MODEL = 2048
BATCH = 2
SEQ = 4096
DEPTH = 2
DEC_BATCH = 32
DEC_SEQ = 1
PAST_LEN = 8192
PAGE_SIZE = 128

N_HEADS = 8
N_KV_HEADS = 4
HEAD_DIM = 128
MOBA_BLOCK = 256
MOBA_TOPK = 3
Q_CHUNK = 64
SGU_WIDTH = 1024
SGU_GROUPS = 8
SGU_GROUP_DIM = SGU_WIDTH // SGU_GROUPS
SGU_CHUNK = 128
N_MEM = 256
X_HEADS = 4
X_HEAD_DIM = 128
D_FF = 4 * D_MODEL
DN_ALPHA = (2 * DEPTH) ** 0.25
DN_BETA = (8 * DEPTH) ** -0.25
LN_EPS = 1e-5

Q_W = N_HEADS * HEAD_DIM
KV_W = N_KV_HEADS * HEAD_DIM
X_W = X_HEADS * X_HEAD_DIM
IN_W = Q_W + 2 * KV_W + 2 * SGU_WIDTH + 2 * D_MODEL
IN_SPLITS = (Q_W, Q_W + KV_W, Q_W + 2 * KV_W, Q_W + 2 * KV_W + SGU_WIDTH,
             Q_W + 2 * KV_W + 2 * SGU_WIDTH, Q_W + 2 * KV_W + 2 * SGU_WIDTH + D_MODEL)

kernel_name = 'hybrid_moba_gmlp_decoder_step'


def layer_norm(x, g, b):
    xf = x.astype(jnp.float32)
    mu = jnp.mean(xf, axis=-1, keepdims=True)
    var = jnp.mean(jnp.square(xf - mu), axis=-1, keepdims=True)
    y = (xf - mu) * lax.rsqrt(var + LN_EPS)
    return (y * g.astype(jnp.float32) + b.astype(jnp.float32)).astype(x.dtype)


def key_blocks(k):
    B, L, KVH, Dh = k.shape
    nb = -(-L // MOBA_BLOCK)
    kp = jnp.pad(k, ((0, 0), (0, nb * MOBA_BLOCK - L), (0, 0), (0, 0)))
    return kp.reshape(B, nb, MOBA_BLOCK, KVH, Dh).transpose(0, 3, 1, 2, 4)


def block_means(kb):
    return jnp.mean(kb.astype(jnp.float32), axis=3).astype(kb.dtype)


def moba_attend(q, kb, vb, kmean, q_pos):
    B, Q, H, Dh = q.shape
    nb = kb.shape[2]
    k_sel = min(MOBA_TOPK, nb)
    kv_of_head = jnp.arange(H) // (H // N_KV_HEADS)
    qt = jnp.swapaxes(q, 1, 2)
    own = q_pos // MOBA_BLOCK
    gate = jnp.einsum('bhqd,bhnd->bhqn', qt, kmean[:, kv_of_head]).astype(jnp.float32)
    fully_past = jnp.arange(nb)[None, :] < own[:, None]
    gate = jnp.where(fully_past, gate, -1e30)
    _, top = lax.top_k(gate, k_sel)
    own_b = jnp.broadcast_to(own[None, None, :, None].astype(top.dtype), (B, H, Q, 1))
    idx = jnp.concatenate([top, own_b], axis=-1)
    slot_ok = jnp.concatenate([top < own_b, jnp.ones((B, H, Q, 1), bool)], axis=-1)
    bi = jnp.arange(B)[:, None, None, None]
    hi = kv_of_head[None, :, None, None]
    kg = kb[bi, hi, idx]
    vg = vb[bi, hi, idx]
    logits = jnp.einsum('bhqd,bhqskd->bhqsk', qt, kg).astype(jnp.float32) * (HEAD_DIM ** -0.5)
    key_pos = idx[..., None] * MOBA_BLOCK + jnp.arange(MOBA_BLOCK, dtype=idx.dtype)
    valid = slot_ok[..., None] & (key_pos <= q_pos[None, None, :, None, None])
    p = jax.nn.softmax(jnp.where(valid, logits, -1e30), axis=(-2, -1)).astype(vg.dtype)
    return jnp.einsum('bhqsk,bhqskd->bqhd', p, vg)


def moba_prompt(q, k, v):
    B, S, H, Dh = q.shape
    kb = key_blocks(k)
    vb = key_blocks(v)
    kmean = block_means(kb)
    nq = S // Q_CHUNK
    qc = jnp.swapaxes(q.reshape(B, nq, Q_CHUNK, H, Dh), 0, 1)
    pos = jnp.arange(S, dtype=jnp.int32).reshape(nq, Q_CHUNK)
    out = lax.map(lambda qp: moba_attend(qp[0], kb, vb, kmean, qp[1]), (qc, pos))
    return jnp.swapaxes(out, 0, 1).reshape(B, S, H, Dh)


def moba_cached(q, k_new, v_new, k_pool, v_pool, page_table):
    DB, T = q.shape[:2]
    k_past = k_pool[page_table].reshape(DB, -1, N_KV_HEADS, HEAD_DIM)
    v_past = v_pool[page_table].reshape(DB, -1, N_KV_HEADS, HEAD_DIM)
    past_len = k_past.shape[1]
    kb = key_blocks(jnp.concatenate([k_past, k_new.astype(k_past.dtype)], axis=1))
    vb = key_blocks(jnp.concatenate([v_past, v_new.astype(v_past.dtype)], axis=1))
    q_pos = past_len + jnp.arange(T, dtype=jnp.int32)
    return moba_attend(q, kb, vb, block_means(kb), q_pos)


def sgu_mix(vn, w_s, b_s):
    B, T, _ = vn.shape
    tc = min(T, SGU_CHUNK)
    vr = vn.reshape(B, T // tc, tc, SGU_GROUPS, SGU_GROUP_DIM)
    ws = jnp.tril(w_s[:, :tc, :tc])
    mixed = jnp.einsum('gts,bnsgc->bntgc', ws, vr) + jnp.swapaxes(b_s[:, :tc], 0, 1)[:, :, None]
    return mixed.reshape(B, T, SGU_WIDTH)


def cross_attend(x, mk, mv, w_xq, w_xo):
    B, T, _ = x.shape
    q = (x @ w_xq).reshape(B, T, X_HEADS, X_HEAD_DIM)
    s = jnp.einsum('bthd,bmhd->bhtm', q, mk).astype(jnp.float32) * (X_HEAD_DIM ** -0.5)
    p = jax.nn.softmax(s, axis=-1).astype(mv.dtype)
    o = jnp.einsum('bhtm,bmhd->bthd', p, mv).reshape(B, T, X_W)
    return o @ w_xo


def sq_relu_mlp(x, w_up, w_down):
    h = jax.nn.relu(x @ w_up)
    return (h * h) @ w_down


def decoder_layer(x, moba_fn, mk, mv, w_in, b_gate, sgu_ln_g, sgu_ln_b, w_s, b_s, w_pa, w_pb, w_o,
                  ln1_g, ln1_b, w_xq, w_xo, ln2_g, ln2_b, w_up, w_down, ln3_g, ln3_b):
    B, T, _ = x.shape
    q, k, v, zu, zv, ga, gb = jnp.split(x @ w_in, IN_SPLITS, axis=-1)
    q = q.reshape(B, T, N_HEADS, HEAD_DIM)
    k = k.reshape(B, T, N_KV_HEADS, HEAD_DIM)
    v = v.reshape(B, T, N_KV_HEADS, HEAD_DIM)
    a = moba_fn(q, k, v).reshape(B, T, Q_W)
    vn = layer_norm(jax.nn.gelu(zv), sgu_ln_g, sgu_ln_b)
    s = jax.nn.gelu(zu) * sgu_mix(vn, w_s, b_s)
    g_a = jax.nn.sigmoid(ga + b_gate[0])
    g_b = jax.nn.sigmoid(gb + b_gate[1])
    mix = (g_a * (a @ w_pa) + g_b * (s @ w_pb)) @ w_o
    x = layer_norm(DN_ALPHA * x + mix, ln1_g, ln1_b)
    x = layer_norm(DN_ALPHA * x + cross_attend(x, mk, mv, w_xq, w_xo), ln2_g, ln2_b)
    x = layer_norm(DN_ALPHA * x + sq_relu_mlp(x, w_up, w_down), ln3_g, ln3_b)
    return x, k, v, vn


def setup_inputs(seed: int = 0) -> dict:
    key = jax.random.key(seed)
    ks = iter(jax.random.split(key, 40))
    n_pages = PAST_LEN // PAGE_SIZE
    n_used = DEC_BATCH * n_pages
    n_pool = n_used + (n_used + 3) // 4

    def nrm(shape, scale=1.0):
        return jax.random.normal(next(ks), shape, jnp.float32) * scale

    def gain(shape):
        return 1.0 + nrm(shape, 0.02)

    x_prompt = nrm((BATCH, SEQ, D_MODEL))
    x_sample = nrm((DEC_BATCH, DEC_SEQ, D_MODEL))
    mem_prompt = nrm((BATCH, N_MEM, D_MODEL))
    cache_k = nrm((DEPTH, n_pool, PAGE_SIZE, N_KV_HEADS, HEAD_DIM))
    cache_v = nrm((DEPTH, n_pool, PAGE_SIZE, N_KV_HEADS, HEAD_DIM))
    cache_mem_k = nrm((DEPTH, DEC_BATCH, N_MEM, X_HEADS, X_HEAD_DIM))
    cache_mem_v = nrm((DEPTH, DEC_BATCH, N_MEM, X_HEADS, X_HEAD_DIM))
    page_table = jax.random.permutation(next(ks), n_pool)[:n_used].reshape(DEC_BATCH, n_pages).astype(jnp.int32)
    return dict(
        x_prompt=x_prompt, x_sample=x_sample, mem_prompt=mem_prompt,
        cache_k=cache_k, cache_v=cache_v, cache_mem_k=cache_mem_k, cache_mem_v=cache_mem_v,
        page_table=page_table,
        w_in=nrm((DEPTH, D_MODEL, IN_W), D_MODEL ** -0.5),
        b_gate=nrm((DEPTH, 2, D_MODEL), 0.02),
        sgu_ln_g=gain((DEPTH, SGU_WIDTH)),
        sgu_ln_b=nrm((DEPTH, SGU_WIDTH), 0.02),
        w_s=nrm((DEPTH, SGU_GROUPS, SGU_CHUNK, SGU_CHUNK), SGU_CHUNK ** -0.5),
        b_s=gain((DEPTH, SGU_GROUPS, SGU_CHUNK)),
        w_pa=nrm((DEPTH, Q_W, D_MODEL), Q_W ** -0.5),
        w_pb=nrm((DEPTH, SGU_WIDTH, D_MODEL), SGU_WIDTH ** -0.5),
        w_o=nrm((DEPTH, D_MODEL, D_MODEL), DN_BETA * D_MODEL ** -0.5),
        ln1_g=gain((DEPTH, D_MODEL)),
        ln1_b=nrm((DEPTH, D_MODEL), 0.02),
        w_xq=nrm((DEPTH, D_MODEL, X_W), D_MODEL ** -0.5),
        w_xk=nrm((DEPTH, D_MODEL, X_W), D_MODEL ** -0.5),
        w_xv=nrm((DEPTH, D_MODEL, X_W), D_MODEL ** -0.5),
        w_xo=nrm((DEPTH, X_W, D_MODEL), DN_BETA * X_W ** -0.5),
        ln2_g=gain((DEPTH, D_MODEL)),
        ln2_b=nrm((DEPTH, D_MODEL), 0.02),
        w_up=nrm((DEPTH, D_MODEL, D_FF), D_MODEL ** -0.5),
        w_down=nrm((DEPTH, D_FF, D_MODEL), DN_BETA * D_FF ** -0.5),
        ln3_g=gain((DEPTH, D_MODEL)),
        ln3_b=nrm((DEPTH, D_MODEL), 0.02),
    )


def reference(x_prompt, x_sample, mem_prompt, cache_k, cache_v, cache_mem_k, cache_mem_v, page_table,
              w_in, b_gate, sgu_ln_g, sgu_ln_b, w_s, b_s, w_pa, w_pb, w_o, ln1_g, ln1_b,
              w_xq, w_xk, w_xv, w_xo, ln2_g, ln2_b, w_up, w_down, ln3_g, ln3_b):
    B, S, _ = x_prompt.shape
    xp, xs = x_prompt, x_sample
    kp_l, vp_l, mkp_l, mvp_l, ks_l, vs_l, vns_l = [], [], [], [], [], [], []
    for l in range(DEPTH):
        shared = (w_in[l], b_gate[l], sgu_ln_g[l], sgu_ln_b[l], w_s[l], b_s[l], w_pa[l], w_pb[l], w_o[l],
                  ln1_g[l], ln1_b[l], w_xq[l], w_xo[l], ln2_g[l], ln2_b[l], w_up[l], w_down[l],
                  ln3_g[l], ln3_b[l])
        mkp = (mem_prompt @ w_xk[l]).reshape(B, N_MEM, X_HEADS, X_HEAD_DIM)
        mvp = (mem_prompt @ w_xv[l]).reshape(B, N_MEM, X_HEADS, X_HEAD_DIM)
        xp, kp, vp, _ = decoder_layer(xp, moba_prompt, mkp, mvp, *shared)
        kp_l.append(kp.reshape(B, S // PAGE_SIZE, PAGE_SIZE, N_KV_HEADS, HEAD_DIM))
        vp_l.append(vp.reshape(B, S // PAGE_SIZE, PAGE_SIZE, N_KV_HEADS, HEAD_DIM))
        mkp_l.append(mkp)
        mvp_l.append(mvp)
        moba_s = functools.partial(moba_cached, k_pool=cache_k[l], v_pool=cache_v[l], page_table=page_table)
        xs, ks, vs, vns = decoder_layer(xs, moba_s, cache_mem_k[l], cache_mem_v[l], *shared)
        ks_l.append(ks)
        vs_l.append(vs)
        vns_l.append(vns)
    return (xp, xs, jnp.stack(kp_l), jnp.stack(vp_l), jnp.stack(mkp_l), jnp.stack(mvp_l),
            jnp.stack(ks_l), jnp.stack(vs_l), jnp.stack(vns_l))
```

```python
import functools

import jax
import jax.numpy as jnp
import numpy as np
from jax import lax
from jax.experimental import pallas as pl
from jax.experimental.pallas import tpu as pltpu

D_MODEL = 2048
DEPTH = 2
PAGE_SIZE = 128
N_HEADS = 8
N_KV_HEADS = 4
HEAD_DIM = 128
MOBA_BLOCK = 256
MOBA_TOPK = 3
SGU_WIDTH = 1024
SGU_GROUPS = 8
SGU_GROUP_DIM = SGU_WIDTH // SGU_GROUPS
SGU_CHUNK = 128
N_MEM = 256
X_HEADS = 4
X_HEAD_DIM = 128
D_FF = 4 * D_MODEL
DN_ALPHA = (2 * DEPTH) ** 0.25
LN_EPS = 1e-5
Q_W = N_HEADS * HEAD_DIM
KV_W = N_KV_HEADS * HEAD_DIM
X_W = X_HEADS * X_HEAD_DIM
QKV_W = Q_W + 2 * KV_W
REST_W = 2 * SGU_WIDTH + 2 * D_MODEL
PAGES_PER_BLOCK = MOBA_BLOCK // PAGE_SIZE
MASKED = -1e30

LANES = 128
VMEM_LIMIT = 56 * 1024 * 1024

BF16 = jnp.bfloat16
F32 = jnp.float32
_NT = (((1,), (1,)), ((), ()))


def _params(*sem):
    return pltpu.CompilerParams(dimension_semantics=sem, vmem_limit_bytes=VMEM_LIMIT)


def _dot(a, b):
    return jnp.dot(a, b, preferred_element_type=F32)


def _dot_nt(a, b):
    return lax.dot_general(a, b, _NT, preferred_element_type=F32)


def _gelu(x):
    c = np.float32(np.sqrt(2 / np.pi))
    return x * (0.5 * (1.0 + jnp.tanh(c * (x + 0.044715 * (x * x * x)))))


def _layer_norm(z, g, b):
    mu = jnp.mean(z, axis=-1, keepdims=True)
    d = z - mu
    var = jnp.mean(d * d, axis=-1, keepdims=True)
    return d * lax.rsqrt(var + LN_EPS) * g + b


def _top_blocks(gate, valid_f, lane_f):
    sel = jnp.zeros(gate.shape, F32)
    for _ in range(MOBA_TOPK):
        m = jnp.max(gate, axis=-1, keepdims=True)
        first = jnp.min(jnp.where(gate == m, lane_f, float(LANES)), axis=-1, keepdims=True)
        pick = lane_f == first
        sel = jnp.where(pick, valid_f, sel)
        gate = jnp.where(pick, -jnp.inf, gate)
    return sel


def _mm_kernel(x_ref, w_ref, *out_refs):
    acc = _dot(x_ref[...], w_ref[...])
    for o in out_refs:
        o[...] = acc.astype(o.dtype)


def _matmul(x, w, col_off, ncols, out_dtypes, tm, tn):
    m, k = x.shape
    tm = min(tm, m)
    tn = min(tn, ncols)
    assert m % tm == 0 and ncols % tn == 0 and col_off % tn == 0
    joff = col_off // tn
    outs = pl.pallas_call(
        _mm_kernel,
        out_shape=[jax.ShapeDtypeStruct((m, ncols), dt) for dt in out_dtypes],
        grid=(ncols // tn, m // tm),
        in_specs=[pl.BlockSpec((tm, k), lambda j, i: (i, 0)),
                  pl.BlockSpec((k, tn), lambda j, i: (0, j + joff))],
        out_specs=[pl.BlockSpec((tm, tn), lambda j, i: (i, j)) for _ in out_dtypes],
        compiler_params=_params("parallel", "parallel"),
        name="matmul",
    )(x, w)
    return outs


def _kmean_kernel(k_ref, o_ref):
    o_ref[0] = jnp.sum(k_ref[...], axis=0, keepdims=True) * (1.0 / MOBA_BLOCK)


def _prompt_kmeans(qkv_f):
    m = qkv_f.shape[0]
    nb = m // MOBA_BLOCK
    return pl.pallas_call(
        _kmean_kernel,
        out_shape=jax.ShapeDtypeStruct((nb, 1, KV_W), F32),
        grid=(nb,),
        in_specs=[pl.BlockSpec((MOBA_BLOCK, KV_W), lambda i: (i, Q_W // KV_W))],
        out_specs=pl.BlockSpec((1, 1, KV_W), lambda i: (i, 0, 0)),
        compiler_params=_params("parallel"),
        name="prompt_kmeans",
    )(qkv_f)


def _moba_prompt_kernel(q_ref, k_ref, v_ref, km_ref, o_ref):
    i = pl.program_id(2)
    q = q_ref[...]
    blk = MOBA_BLOCK
    scale = HEAD_DIM ** -0.5

    gate = _dot_nt(q, km_ref[0].astype(BF16))
    lane = lax.broadcasted_iota(jnp.int32, gate.shape, 1)
    valid = lane < i
    gate = jnp.where(valid, gate, MASKED)
    sel = _top_blocks(gate, valid.astype(F32), lane.astype(F32)).astype(BF16)

    def block_scores(j):
        kj = k_ref[pl.ds(pl.multiple_of(j * blk, blk), blk), :]
        return _dot_nt(q, kj) * scale

    def block_values(j):
        return v_ref[pl.ds(pl.multiple_of(j * blk, blk), blk), :]

    row = lax.broadcasted_iota(jnp.int32, (blk, blk), 0)
    col = lax.broadcasted_iota(jnp.int32, (blk, blk), 1)
    s = jnp.where(col <= row, block_scores(i), MASKED)
    m0 = jnp.max(s, axis=-1, keepdims=True)
    p = jnp.exp(s - m0)
    l0 = jnp.sum(p, axis=-1, keepdims=True)
    acc0 = _dot(p.astype(BF16), block_values(i))

    blk_id = lax.broadcasted_iota(jnp.int32, (LANES, blk), 0)

    def body(j, carry):
        m, l, acc = carry
        chosen = _dot(sel, jnp.where(blk_id == j, 1.0, 0.0).astype(BF16))
        s = jnp.where(chosen > 0.5, block_scores(j), MASKED)
        m_new = jnp.maximum(m, jnp.max(s, axis=-1, keepdims=True))
        a = jnp.exp(m - m_new)
        p = jnp.exp(s - m_new)
        l = a * l + jnp.sum(p, axis=-1, keepdims=True)
        acc = a * acc + _dot(p.astype(BF16), block_values(j))
        return m_new, l, acc

    _, l, acc = lax.fori_loop(0, i, body, (m0, l0, acc0))
    o_ref[...] = (acc / l).astype(o_ref.dtype)


def _moba_prompt(qkv_b, kmean_pad, batch, seq):
    nq = seq // MOBA_BLOCK
    rep = N_HEADS // N_KV_HEADS
    k_col0 = Q_W // HEAD_DIM
    v_col0 = (Q_W + KV_W) // HEAD_DIM
    return pl.pallas_call(
        _moba_prompt_kernel,
        out_shape=jax.ShapeDtypeStruct((batch * seq, Q_W), BF16),
        grid=(batch, N_HEADS, nq),
        in_specs=[
            pl.BlockSpec((MOBA_BLOCK, HEAD_DIM), lambda b, h, i: (b * nq + i, h)),
            pl.BlockSpec((seq, HEAD_DIM), lambda b, h, i: (b, k_col0 + h // rep)),
            pl.BlockSpec((seq, HEAD_DIM), lambda b, h, i: (b, v_col0 + h // rep)),
            pl.BlockSpec((1, LANES, HEAD_DIM), lambda b, h, i: (b, 0, h // rep)),
        ],
        out_specs=pl.BlockSpec((MOBA_BLOCK, HEAD_DIM), lambda b, h, i: (b * nq + i, h)),
        compiler_params=_params("parallel", "parallel", "parallel"),
        name="moba_prompt",
    )(qkv_b, qkv_b, qkv_b, kmean_pad)


_PAGES_PER_STEP = 8


def _cached_kmean_kernel(pt_ref, *refs):
    del pt_ref
    pages, o_ref = refs[:-1], refs[-1]
    for t in range(_PAGES_PER_STEP // PAGES_PER_BLOCK):
        tot = jnp.sum(pages[2 * t][0], axis=0, keepdims=True) + jnp.sum(pages[2 * t + 1][0], axis=0, keepdims=True)
        o_ref[0, 0, t:t + 1, :] = tot * (1.0 / MOBA_BLOCK)


def _cached_kmeans(cache_k_pages, pt_flat, layer, n_pool, n_samples, n_pages):
    steps = n_pages // _PAGES_PER_STEP
    per_step = _PAGES_PER_STEP // PAGES_PER_BLOCK
    base = layer * n_pool

    def page_spec(r):
        return pl.BlockSpec((1, PAGE_SIZE, KV_W),
                            lambda b, g, pt: (base + pt[b * n_pages + g * _PAGES_PER_STEP + r], 0, 0))

    out = pl.pallas_call(
        _cached_kmean_kernel,
        out_shape=jax.ShapeDtypeStruct((n_samples, steps, per_step, KV_W), F32),
        grid_spec=pltpu.PrefetchScalarGridSpec(
            num_scalar_prefetch=1,
            grid=(n_samples, steps),
            in_specs=[page_spec(r) for r in range(_PAGES_PER_STEP)],
            out_specs=pl.BlockSpec((1, 1, per_step, KV_W), lambda b, g, pt: (b, g, 0, 0)),
        ),
        compiler_params=_params("parallel", "parallel"),
        name="cached_kmeans",
    )(pt_flat, *([cache_k_pages] * _PAGES_PER_STEP))
    return out.reshape(n_samples, steps * per_step, KV_W)


def _cached_select_kernel(q_ref, km_ref, o_ref, *, n_blocks):
    rep = N_HEADS // N_KV_HEADS
    hrow = lax.broadcasted_iota(jnp.int32, (N_HEADS, LANES), 0)
    lane = lax.broadcasted_iota(jnp.int32, (N_HEADS, LANES), 1)
    q8 = jnp.zeros((N_HEADS, HEAD_DIM), F32)
    hsub = lax.broadcasted_iota(jnp.int32, (N_HEADS, HEAD_DIM), 0)
    for h in range(N_HEADS):
        q8 = jnp.where(hsub == h, q_ref[0, :, h * HEAD_DIM:(h + 1) * HEAD_DIM], q8)
    q8 = q8.astype(BF16)
    gate = jnp.zeros((N_HEADS, LANES), F32)
    for kvh in range(N_KV_HEADS):
        g = _dot_nt(q8, km_ref[0, :, kvh * HEAD_DIM:(kvh + 1) * HEAD_DIM].astype(BF16))
        gate = jnp.where(hrow >= kvh * rep, g, gate)
    valid = lane < n_blocks
    gate = jnp.where(valid, gate, MASKED)
    lane_f = lane.astype(F32)
    out = jnp.zeros((N_HEADS, LANES), F32)
    for r in range(MOBA_TOPK):
        m = jnp.max(gate, axis=-1, keepdims=True)
        first = jnp.min(jnp.where(gate == m, lane_f, float(LANES)), axis=-1, keepdims=True)
        out = jnp.where(lane == r, first, out)
        gate = jnp.where(lane_f == first, -jnp.inf, gate)
    o_ref[0] = out.astype(jnp.int32)


def _cached_select(qkv_f3, kmean_pad, n_blocks):
    n = qkv_f3.shape[0]
    return pl.pallas_call(
        functools.partial(_cached_select_kernel, n_blocks=n_blocks),
        out_shape=jax.ShapeDtypeStruct((n, N_HEADS, LANES), jnp.int32),
        grid=(n,),
        in_specs=[pl.BlockSpec((1, 1, Q_W), lambda b: (b, 0, 0)),
                  pl.BlockSpec((1, LANES, KV_W), lambda b: (b, 0, 0))],
        out_specs=pl.BlockSpec((1, N_HEADS, LANES), lambda b: (b, 0, 0)),
        compiler_params=_params("parallel"),
        name="cached_select",
    )(qkv_f3, kmean_pad)


def _cached_attend_kernel(pt_ref, sel_ref, q_ref, kn_ref, vn_ref, *refs):
    del pt_ref, sel_ref
    n_sel = MOBA_TOPK * PAGES_PER_BLOCK
    k_pages, v_pages, o_ref = refs[:n_sel], refs[n_sel:2 * n_sel], refs[-1]
    scale = HEAD_DIM ** -0.5
    q = q_ref[0].astype(BF16)
    q8 = jnp.broadcast_to(q, (8, HEAD_DIM))
    s_pages = [_dot_nt(q8, kp[0].astype(BF16))[0:1] * scale for kp in k_pages]
    kn = kn_ref[0].astype(BF16)
    s_new = jnp.sum(q.astype(F32) * kn.astype(F32), axis=-1, keepdims=True) * scale
    m = s_new
    for s in s_pages:
        m = jnp.maximum(m, jnp.max(s, axis=-1, keepdims=True))
    e_pages = [jnp.exp(s - m) for s in s_pages]
    e_new = jnp.exp(s_new - m)
    denom = e_new
    for e in e_pages:
        denom = denom + jnp.sum(e, axis=-1, keepdims=True)
    out = (e_new / denom).astype(BF16).astype(F32) * vn_ref[0].astype(BF16).astype(F32)
    for e, vp in zip(e_pages, v_pages):
        p8 = jnp.broadcast_to((e / denom).astype(BF16), (8, PAGE_SIZE))
        out = out + _dot(p8, vp[0].astype(BF16))[0:1]
    o_ref[0] = out


def _cached_attend(qkv_f3, cache_k_heads, cache_v_heads, pt_flat, sel_flat, layer, n_pool, n_pages):
    n = qkv_f3.shape[0]
    rep = N_HEADS // N_KV_HEADS
    base = layer * n_pool
    k_col0 = Q_W // HEAD_DIM
    v_col0 = (Q_W + KV_W) // HEAD_DIM

    def page_spec(t, r):
        def index(b, h, pt, sel):
            block = sel[(b * N_HEADS + h) * MOBA_TOPK + t]
            return (base + pt[b * n_pages + block * PAGES_PER_BLOCK + r], 0, h // rep)
        return pl.BlockSpec((1, PAGE_SIZE, HEAD_DIM), index)

    page_specs = [page_spec(t, r) for t in range(MOBA_TOPK) for r in range(PAGES_PER_BLOCK)]
    return pl.pallas_call(
        _cached_attend_kernel,
        out_shape=jax.ShapeDtypeStruct((n, 1, Q_W), F32),
        grid_spec=pltpu.PrefetchScalarGridSpec(
            num_scalar_prefetch=2,
            grid=(n, N_HEADS),
            in_specs=[pl.BlockSpec((1, 1, HEAD_DIM), lambda b, h, pt, sel: (b, 0, h)),
                      pl.BlockSpec((1, 1, HEAD_DIM), lambda b, h, pt, sel: (b, 0, k_col0 + h // rep)),
                      pl.BlockSpec((1, 1, HEAD_DIM), lambda b, h, pt, sel: (b, 0, v_col0 + h // rep))]
            + page_specs + page_specs,
            out_specs=pl.BlockSpec((1, 1, HEAD_DIM), lambda b, h, pt, sel: (b, 0, h)),
        ),
        compiler_params=_params("parallel", "parallel"),
        name="cached_attend",
    )(pt_flat, sel_flat, qkv_f3, qkv_f3, qkv_f3,
      *([cache_k_heads] * len(page_specs)), *([cache_v_heads] * len(page_specs)))


def _sgu_chunk_kernel(zu_ref, zv_ref, g_ref, b_ref, ws_ref, bs_ref, s_ref):
    vn = _layer_norm(_gelu(zv_ref[...]), g_ref[...], b_ref[...])
    gu = _gelu(zu_ref[...])
    t = SGU_CHUNK
    row = lax.broadcasted_iota(jnp.int32, (t, t), 0)
    col = lax.broadcasted_iota(jnp.int32, (t, t), 1)
    for g in range(SGU_GROUPS):
        cs = slice(g * SGU_GROUP_DIM, (g + 1) * SGU_GROUP_DIM)
        ws = jnp.where(col <= row, ws_ref[g], 0.0).astype(BF16)
        bias = bs_ref[:, g:g + 1]
        for c in range(zu_ref.shape[0] // t):
            rs = slice(c * t, (c + 1) * t)
            mixed = _dot(ws, vn[rs, cs].astype(BF16)) + bias
            s_ref[rs, cs] = (gu[rs, cs] * mixed).astype(s_ref.dtype)


def _sgu_chunks(rest, ln_g, ln_b, w_s, b_s_t, tm):
    m = rest.shape[0]
    return pl.pallas_call(
        _sgu_chunk_kernel,
        out_shape=jax.ShapeDtypeStruct((m, SGU_WIDTH), BF16),
        grid=(m // tm,),
        in_specs=[pl.BlockSpec((tm, SGU_WIDTH), lambda i: (i, 0)),
                  pl.BlockSpec((tm, SGU_WIDTH), lambda i: (i, 1)),
                  pl.BlockSpec((1, SGU_WIDTH), lambda i: (0, 0)),
                  pl.BlockSpec((1, SGU_WIDTH), lambda i: (0, 0)),
                  pl.BlockSpec((SGU_GROUPS, SGU_CHUNK, SGU_CHUNK), lambda i: (0, 0, 0)),
                  pl.BlockSpec((SGU_CHUNK, SGU_GROUPS), lambda i: (0, 0))],
        out_specs=pl.BlockSpec((tm, SGU_WIDTH), lambda i: (i, 0)),
        compiler_params=_params("parallel"),
        name="sgu_chunks",
    )(rest, rest, ln_g, ln_b, w_s, b_s_t)


def _sgu_first_row_kernel(zu_ref, zv_ref, g_ref, b_ref, w0_ref, b0_ref, s_ref, vn_ref):
    vn = _layer_norm(_gelu(zv_ref[...]), g_ref[...], b_ref[...])
    vn_ref[...] = vn
    s_ref[...] = (_gelu(zu_ref[...]) * (vn * w0_ref[...] + b0_ref[...])).astype(s_ref.dtype)


def _sgu_first_rows(rest, ln_g, ln_b, w0, b0):
    m = rest.shape[0]
    vec = pl.BlockSpec((1, SGU_WIDTH), lambda i: (0, 0))
    return pl.pallas_call(
        _sgu_first_row_kernel,
        out_shape=[jax.ShapeDtypeStruct((m, SGU_WIDTH), BF16), jax.ShapeDtypeStruct((m, SGU_WIDTH), F32)],
        grid=(1,),
        in_specs=[pl.BlockSpec((m, SGU_WIDTH), lambda i: (0, 0)),
                  pl.BlockSpec((m, SGU_WIDTH), lambda i: (0, 1)), vec, vec, vec, vec],
        out_specs=[pl.BlockSpec((m, SGU_WIDTH), lambda i: (0, 0)), pl.BlockSpec((m, SGU_WIDTH), lambda i: (0, 0))],
        compiler_params=_params("arbitrary"),
        name="sgu_first_rows",
    )(rest, rest, ln_g, ln_b, w0, b0)


def _merge_kernel(a_ref, s_ref, ga_ref, gb_ref, bg_ref, wa_ref, wb_ref, o_ref):
    g_a = jax.nn.sigmoid(ga_ref[...] + bg_ref[0:1, :])
    g_b = jax.nn.sigmoid(gb_ref[...] + bg_ref[1:2, :])
    mix = g_a * _dot(a_ref[...], wa_ref[...]) + g_b * _dot(s_ref[...], wb_ref[...])
    o_ref[...] = mix.astype(o_ref.dtype)


def _merge(a, s, rest, b_gate, w_pa, w_pb, tm, tn):
    m = a.shape[0]
    tm = min(tm, m)
    ga0 = 2 * SGU_WIDTH // tn
    gb0 = (2 * SGU_WIDTH + D_MODEL) // tn
    return pl.pallas_call(
        _merge_kernel,
        out_shape=jax.ShapeDtypeStruct((m, D_MODEL), BF16),
        grid=(D_MODEL // tn, m // tm),
        in_specs=[pl.BlockSpec((tm, Q_W), lambda j, i: (i, 0)),
                  pl.BlockSpec((tm, SGU_WIDTH), lambda j, i: (i, 0)),
                  pl.BlockSpec((tm, tn), lambda j, i: (i, ga0 + j)),
                  pl.BlockSpec((tm, tn), lambda j, i: (i, gb0 + j)),
                  pl.BlockSpec((2, tn), lambda j, i: (0, j)),
                  pl.BlockSpec((Q_W, tn), lambda j, i: (0, j)),
                  pl.BlockSpec((SGU_WIDTH, tn), lambda j, i: (0, j))],
        out_specs=pl.BlockSpec((tm, tn), lambda j, i: (i, j)),
        compiler_params=_params("parallel", "parallel"),
        name="merge",
    )(a, s, rest, rest, b_gate, w_pa, w_pb)


def _proj_ln_kernel(a_ref, w_ref, x_ref, g_ref, b_ref, of_ref, ob_ref):
    z = DN_ALPHA * x_ref[...] + _dot(a_ref[...], w_ref[...])
    y = _layer_norm(z, g_ref[...], b_ref[...])
    of_ref[...] = y
    ob_ref[...] = y.astype(ob_ref.dtype)


def _proj_ln(a, w, x, g, b, tm):
    m, k = a.shape
    tm = min(tm, m)
    vec = pl.BlockSpec((1, D_MODEL), lambda i: (0, 0))
    return pl.pallas_call(
        _proj_ln_kernel,
        out_shape=[jax.ShapeDtypeStruct((m, D_MODEL), F32), jax.ShapeDtypeStruct((m, D_MODEL), BF16)],
        grid=(m // tm,),
        in_specs=[pl.BlockSpec((tm, k), lambda i: (i, 0)),
                  pl.BlockSpec((k, D_MODEL), lambda i: (0, 0)),
                  pl.BlockSpec((tm, D_MODEL), lambda i: (i, 0)), vec, vec],
        out_specs=[pl.BlockSpec((tm, D_MODEL), lambda i: (i, 0)), pl.BlockSpec((tm, D_MODEL), lambda i: (i, 0))],
        compiler_params=_params("parallel"),
        name="proj_ln",
    )(a, w, x, g, b)


def _xattn_kernel(q_ref, mk_ref, mv_ref, o_ref):
    scale = X_HEAD_DIM ** -0.5
    tq = q_ref.shape[1]
    rows = max(tq, 8)
    for h in range(X_HEADS):
        cs = slice(h * X_HEAD_DIM, (h + 1) * X_HEAD_DIM)
        q = q_ref[0, :, cs]
        if rows != tq:
            q = jnp.broadcast_to(q[0:1], (rows, X_HEAD_DIM))
        s = _dot_nt(q, mk_ref[0, :, cs].astype(BF16)) * scale
        e = jnp.exp(s - jnp.max(s, axis=-1, keepdims=True))
        p = (e / jnp.sum(e, axis=-1, keepdims=True)).astype(BF16)
        o = _dot(p, mv_ref[0, :, cs].astype(BF16))
        o_ref[0, :, cs] = o[0:tq].astype(o_ref.dtype)


def _xattn(q3, mk3, mv3, mem_off, tq):
    n, t, _ = q3.shape
    tq = min(tq, t)
    return pl.pallas_call(
        _xattn_kernel,
        out_shape=jax.ShapeDtypeStruct((n, t, X_W), BF16),
        grid=(n, t // tq),
        in_specs=[pl.BlockSpec((1, tq, X_W), lambda b, i: (b, i, 0)),
                  pl.BlockSpec((1, N_MEM, X_W), lambda b, i: (mem_off + b, 0, 0)),
                  pl.BlockSpec((1, N_MEM, X_W), lambda b, i: (mem_off + b, 0, 0))],
        out_specs=pl.BlockSpec((1, tq, X_W), lambda b, i: (b, i, 0)),
        compiler_params=_params("parallel", "parallel"),
        name="xattn",
    )(q3, mk3, mv3)


def _mlp_kernel(xb_ref, xf_ref, wu_ref, wd_ref, g_ref, b_ref, of_ref, ob_ref, acc_ref):
    f = pl.program_id(1)

    @pl.when(f == 0)
    def _():
        acc_ref[...] = jnp.zeros_like(acc_ref)

    h = jnp.maximum(_dot(xb_ref[...], wu_ref[...]), 0.0)
    acc_ref[...] += _dot((h * h).astype(BF16), wd_ref[...])

    @pl.when(f == pl.num_programs(1) - 1)
    def _():
        y = _layer_norm(DN_ALPHA * xf_ref[...] + acc_ref[...], g_ref[...], b_ref[...])
        of_ref[...] = y
        ob_ref[...] = y.astype(ob_ref.dtype)


def _mlp(xb, xf, w_up, w_down, g, b, tm, tf):
    m = xb.shape[0]
    tm = min(tm, m)
    vec = pl.BlockSpec((1, D_MODEL), lambda i, f: (0, 0))
    return pl.pallas_call(
        _mlp_kernel,
        out_shape=[jax.ShapeDtypeStruct((m, D_MODEL), F32), jax.ShapeDtypeStruct((m, D_MODEL), BF16)],
        grid=(m // tm, D_FF // tf),
        in_specs=[pl.BlockSpec((tm, D_MODEL), lambda i, f: (i, 0)),
                  pl.BlockSpec((tm, D_MODEL), lambda i, f: (i, 0)),
                  pl.BlockSpec((D_MODEL, tf), lambda i, f: (0, f)),
                  pl.BlockSpec((tf, D_MODEL), lambda i, f: (f, 0)), vec, vec],
        out_specs=[pl.BlockSpec((tm, D_MODEL), lambda i, f: (i, 0)),
                   pl.BlockSpec((tm, D_MODEL), lambda i, f: (i, 0))],
        scratch_shapes=[pltpu.VMEM((tm, D_MODEL), F32)],
        compiler_params=_params("parallel", "arbitrary"),
        name="mlp",
    )(xb, xf, w_up, w_down, g, b)


def _layer_tail(xf, a, s, rest, mem_k, mem_v, mem_off, n_mem, w, tiles):
    m = xf.shape[0]
    mix = _merge(a, s, rest, w["b_gate"], w["w_pa"], w["w_pb"], tiles["merge_tm"], 1024)
    x1f, x1b = _proj_ln(mix, w["w_o"], xf, w["ln1_g"], w["ln1_b"], tiles["ln_tm"])
    (qx,) = _matmul(x1b, w["w_xq"], 0, X_W, (BF16,), tiles["mm_tm"], X_W)
    o = _xattn(qx.reshape(n_mem, m // n_mem, X_W), mem_k, mem_v, mem_off, 512).reshape(m, X_W)
    x2f, x2b = _proj_ln(o, w["w_xo"], x1f, w["ln2_g"], w["ln2_b"], tiles["ln_tm"])
    return _mlp(x2b, x2f, w["w_up"], w["w_down"], w["ln3_g"], w["ln3_b"], tiles["mlp_tm"], 512)


_PROMPT_TILES = dict(mm_tm=1024, merge_tm=512, ln_tm=256, mlp_tm=512)
_SAMPLE_TILES = dict(mm_tm=32, merge_tm=32, ln_tm=32, mlp_tm=32)


def kernel(x_prompt, x_sample, mem_prompt, cache_k, cache_v, cache_mem_k, cache_mem_v, page_table,
           w_in, b_gate, sgu_ln_g, sgu_ln_b, w_s, b_s, w_pa, w_pb, w_o, ln1_g, ln1_b,
           w_xq, w_xk, w_xv, w_xo, ln2_g, ln2_b, w_up, w_down, ln3_g, ln3_b):
    batch, seq, _ = x_prompt.shape
    n_samples, dec_seq, _ = x_sample.shape
    assert dec_seq == 1 and seq % MOBA_BLOCK == 0 and seq // MOBA_BLOCK <= LANES
    n_pool = cache_k.shape[1]
    n_pages = page_table.shape[1]
    n_past_blocks = n_pages // PAGES_PER_BLOCK
    assert n_pages % _PAGES_PER_STEP == 0 and n_past_blocks <= LANES
    mp, ms = batch * seq, n_samples

    pt_flat = page_table.reshape(-1).astype(jnp.int32)
    cache_k_pages = cache_k.reshape(DEPTH * n_pool, PAGE_SIZE, KV_W)
    cache_v_pages = cache_v.reshape(DEPTH * n_pool, PAGE_SIZE, KV_W)
    cache_mk = cache_mem_k.reshape(DEPTH * n_samples, N_MEM, X_W)
    cache_mv = cache_mem_v.reshape(DEPTH * n_samples, N_MEM, X_W)
    mem_b = mem_prompt.reshape(batch * N_MEM, D_MODEL).astype(BF16)

    xpf = x_prompt.reshape(mp, D_MODEL)
    xsf = x_sample.reshape(ms, D_MODEL)
    xpb, xsb = xpf.astype(BF16), xsf.astype(BF16)

    kp_l, vp_l, mkp_l, mvp_l, ks_l, vs_l, vns_l = [], [], [], [], [], [], []
    for l in range(DEPTH):
        w = dict(
            b_gate=b_gate[l], w_pa=w_pa[l].astype(BF16), w_pb=w_pb[l].astype(BF16), w_o=w_o[l].astype(BF16),
            ln1_g=ln1_g[l][None], ln1_b=ln1_b[l][None], w_xq=w_xq[l].astype(BF16), w_xo=w_xo[l].astype(BF16),
            ln2_g=ln2_g[l][None], ln2_b=ln2_b[l][None], w_up=w_up[l].astype(BF16), w_down=w_down[l].astype(BF16),
            ln3_g=ln3_g[l][None], ln3_b=ln3_b[l][None])
        w_in_b = w_in[l].astype(BF16)
        sgu_g, sgu_b = sgu_ln_g[l][None], sgu_ln_b[l][None]

        mk_f, mk_b = _matmul(mem_b, w_xk[l].astype(BF16), 0, X_W, (F32, BF16), 512, X_W)
        mv_f, mv_b = _matmul(mem_b, w_xv[l].astype(BF16), 0, X_W, (F32, BF16), 512, X_W)
        qkv_f, qkv_b = _matmul(xpb, w_in_b, 0, QKV_W, (F32, BF16), 1024, 1024)
        (rest,) = _matmul(xpb, w_in_b, QKV_W, REST_W, (F32,), 1024, 1024)
        kmean = _prompt_kmeans(qkv_f).reshape(batch, seq // MOBA_BLOCK, KV_W)
        kmean = jnp.pad(kmean, ((0, 0), (0, LANES - seq // MOBA_BLOCK), (0, 0)))
        a = _moba_prompt(qkv_b, kmean, batch, seq)
        s = _sgu_chunks(rest, sgu_g, sgu_b, w_s[l], b_s[l].T, 256)
        xpf, xpb = _layer_tail(xpf, a, s, rest, mk_b.reshape(batch, N_MEM, X_W), mv_b.reshape(batch, N_MEM, X_W),
                               0, batch, w, _PROMPT_TILES)
        kp_l.append(qkv_f[:, Q_W:Q_W + KV_W].reshape(batch, seq // PAGE_SIZE, PAGE_SIZE, N_KV_HEADS, HEAD_DIM))
        vp_l.append(qkv_f[:, Q_W + KV_W:].reshape(batch, seq // PAGE_SIZE, PAGE_SIZE, N_KV_HEADS, HEAD_DIM))
        mkp_l.append(mk_f.reshape(batch, N_MEM, X_HEADS, X_HEAD_DIM))
        mvp_l.append(mv_f.reshape(batch, N_MEM, X_HEADS, X_HEAD_DIM))

        (qkv_s,) = _matmul(xsb, w_in_b, 0, QKV_W, (F32,), 32, 1024)
        (rest_s,) = _matmul(xsb, w_in_b, QKV_W, REST_W, (F32,), 32, 1024)
        qkv_s3 = qkv_s.reshape(ms, 1, QKV_W)
        kmean_s = _cached_kmeans(cache_k_pages, pt_flat, l, n_pool, ms, n_pages)
        kmean_s = jnp.pad(kmean_s, ((0, 0), (0, LANES - n_past_blocks), (0, 0)))
        sel = _cached_select(qkv_s3, kmean_s, n_past_blocks)[:, :, :MOBA_TOPK].reshape(-1)
        a_s = _cached_attend(qkv_s3, cache_k_pages, cache_v_pages, pt_flat, sel, l, n_pool, n_pages)
        a_s = a_s.reshape(ms, Q_W).astype(BF16)
        w0 = jnp.repeat(w_s[l][:, 0, 0], SGU_GROUP_DIM)[None]
        b0 = jnp.repeat(b_s[l][:, 0], SGU_GROUP_DIM)[None]
        s_s, vn_s = _sgu_first_rows(rest_s, sgu_g, sgu_b, w0, b0)
        xsf, xsb = _layer_tail(xsf, a_s, s_s, rest_s, cache_mk, cache_mv, l * n_samples, ms, w, _SAMPLE_TILES)
        ks_l.append(qkv_s[:, Q_W:Q_W + KV_W].reshape(ms, 1, N_KV_HEADS, HEAD_DIM))
        vs_l.append(qkv_s[:, Q_W + KV_W:].reshape(ms, 1, N_KV_HEADS, HEAD_DIM))
        vns_l.append(vn_s.reshape(ms, 1, SGU_WIDTH))

    return (xpf.reshape(batch, seq, D_MODEL), xsf.reshape(ms, 1, D_MODEL),
            jnp.stack(kp_l), jnp.stack(vp_l), jnp.stack(mkp_l), jnp.stack(mvp_l),
            jnp.stack(ks_l), jnp.stack(vs_l), jnp.stack(vns_l))
```

```python
import functools

import jax
import jax.numpy as jnp
import numpy as np
from jax import lax
from jax.experimental import pallas as pl
from jax.experimental.pallas import tpu as pltpu

D_MODEL = 2048
DEPTH = 2
PAGE_SIZE = 128
N_HEADS = 8
N_KV_HEADS = 4
HEAD_DIM = 128
MOBA_BLOCK = 256
MOBA_TOPK = 3
SGU_WIDTH = 1024
SGU_GROUPS = 8
SGU_GROUP_DIM = SGU_WIDTH // SGU_GROUPS
SGU_CHUNK = 128
N_MEM = 256
X_HEADS = 4
X_HEAD_DIM = 128
D_FF = 4 * D_MODEL
DN_ALPHA = (2 * DEPTH) ** 0.25
LN_EPS = 1e-5
Q_W = N_HEADS * HEAD_DIM
KV_W = N_KV_HEADS * HEAD_DIM
X_W = X_HEADS * X_HEAD_DIM
QKV_W = Q_W + 2 * KV_W
REST_W = 2 * SGU_WIDTH + 2 * D_MODEL
PAGES_PER_BLOCK = MOBA_BLOCK // PAGE_SIZE
PAGE_ROWS = PAGE_SIZE * N_KV_HEADS
MASKED = -1e30

LANES = 128
SUBLANES = 8
VMEM_LIMIT = 56 * 1024 * 1024

BF16 = jnp.bfloat16
F32 = jnp.float32
_NT = (((1,), (1,)), ((), ()))


def _params(*sem):
    return pltpu.CompilerParams(dimension_semantics=sem, vmem_limit_bytes=VMEM_LIMIT)


def _dot(a, b):
    return jnp.dot(a, b, preferred_element_type=F32)


def _dot_nt(a, b):
    return lax.dot_general(a, b, _NT, preferred_element_type=F32)


def _gelu(x):
    c = np.float32(np.sqrt(2 / np.pi))
    return x * (0.5 * (1.0 + jnp.tanh(c * (x + 0.044715 * (x * x * x)))))


def _layer_norm(z, g, b):
    mu = jnp.mean(z, axis=-1, keepdims=True)
    d = z - mu
    var = jnp.mean(d * d, axis=-1, keepdims=True)
    return d * lax.rsqrt(var + LN_EPS) * g + b


def _top_blocks(gate, valid_f, idx_f, axis):
    sel = jnp.zeros(gate.shape, F32)
    for _ in range(MOBA_TOPK):
        m = jnp.max(gate, axis=axis, keepdims=True)
        first = jnp.min(jnp.where(gate == m, idx_f, float(gate.shape[axis])), axis=axis, keepdims=True)
        pick = idx_f == first
        sel = jnp.where(pick, valid_f, sel)
        gate = jnp.where(pick, -jnp.inf, gate)
    return sel


def _rows_to_sublanes(ref, n_rows, width):
    sub = lax.broadcasted_iota(jnp.int32, (SUBLANES, width), 0)
    out = jnp.zeros((SUBLANES, width), F32)
    for r in range(n_rows):
        out = jnp.where(sub == r, ref[0, :, r * width:(r + 1) * width].astype(F32), out)
    return out


def _mm_kernel(x_ref, w_ref, *out_refs):
    acc = _dot(x_ref[...], w_ref[...])
    for o in out_refs:
        o[...] = acc.astype(o.dtype)


def _matmul(x, w, col_off, ncols, out_dtypes, tm, tn):
    m, k = x.shape
    tm = min(tm, m)
    tn = min(tn, ncols)
    assert m % tm == 0 and ncols % tn == 0 and col_off % tn == 0
    joff = col_off // tn
    outs = pl.pallas_call(
        _mm_kernel,
        out_shape=[jax.ShapeDtypeStruct((m, ncols), dt) for dt in out_dtypes],
        grid=(ncols // tn, m // tm),
        in_specs=[pl.BlockSpec((tm, k), lambda j, i: (i, 0)),
                  pl.BlockSpec((k, tn), lambda j, i: (0, j + joff))],
        out_specs=[pl.BlockSpec((tm, tn), lambda j, i: (i, j)) for _ in out_dtypes],
        compiler_params=_params("parallel", "parallel"),
        name="matmul",
    )(x, w)
    return outs


def _kmean_kernel(k_ref, o_ref):
    o_ref[0] = jnp.sum(k_ref[...], axis=0, keepdims=True) * (1.0 / MOBA_BLOCK)


def _prompt_kmeans(qkv_f):
    m = qkv_f.shape[0]
    nb = m // MOBA_BLOCK
    return pl.pallas_call(
        _kmean_kernel,
        out_shape=jax.ShapeDtypeStruct((nb, 1, KV_W), F32),
        grid=(nb,),
        in_specs=[pl.BlockSpec((MOBA_BLOCK, KV_W), lambda i: (i, Q_W // KV_W))],
        out_specs=pl.BlockSpec((1, 1, KV_W), lambda i: (i, 0, 0)),
        compiler_params=_params("parallel"),
        name="prompt_kmeans",
    )(qkv_f)


_MOBA_KV_PER_STEP = 4


def _moba_prompt_kernel(q_ref, k_ref, vt_ref, km_ref, o_ref, sel_ref, m_ref, l_ref, acc_ref):
    i = pl.program_id(2)
    rep = N_HEADS // N_KV_HEADS
    heads = _MOBA_KV_PER_STEP * rep
    blk = MOBA_BLOCK
    scale_log2e = np.float32(HEAD_DIM ** -0.5 * np.log2(np.e))

    def cols(c):
        return slice(c * HEAD_DIM, (c + 1) * HEAD_DIM)

    qs = [q_ref[:, cols(c)] for c in range(heads)]
    blk_id = lax.broadcasted_iota(jnp.int32, (km_ref.shape[0], blk), 0)
    valid = blk_id < i
    for c in range(heads):
        km = km_ref[:, cols(c // rep)].astype(BF16)
        gate = jnp.where(valid, _dot_nt(km, qs[c]), MASKED)
        sel_ref[c] = _top_blocks(gate, valid.astype(F32), blk_id.astype(F32), 0)

    def attend(j, masks, first):
        kjs = [k_ref[pl.ds(pl.multiple_of(j * blk, blk), blk), cols(g)] for g in range(_MOBA_KV_PER_STEP)]
        scores = [_dot_nt(kjs[c // rep], qs[c]) for c in range(heads)]
        ps, m_news, l_blks = [], [], []
        for c in range(heads):
            s = jnp.where(masks[c], scores[c] * scale_log2e, MASKED)
            m_blk = jnp.max(s, axis=0, keepdims=True)
            m_new = m_blk if first else jnp.maximum(m_ref[c], m_blk)
            p = jnp.exp2(s - m_new)
            l_blks.append(jnp.sum(p, axis=0, keepdims=True))
            ps.append(p.astype(BF16))
            m_news.append(m_new)
        pvs = [_dot(vt_ref[c // rep, j], ps[c]) for c in range(heads)]
        for c in range(heads):
            if first:
                l_ref[c] = l_blks[c]
                acc_ref[c] = pvs[c]
            else:
                a = jnp.exp2(m_ref[c] - m_news[c])
                l_ref[c] = a * l_ref[c] + l_blks[c]
                acc_ref[c] = a * acc_ref[c] + pvs[c]
            m_ref[c] = m_news[c]

    key = lax.broadcasted_iota(jnp.int32, (blk, blk), 0)
    qry = lax.broadcasted_iota(jnp.int32, (blk, blk), 1)
    attend(i, [key <= qry] * heads, True)

    def body(j, carry):
        attend(j, [sel_ref[c, pl.ds(j, 1), :] > 0.5 for c in range(heads)], False)
        return carry

    lax.fori_loop(0, i, body, 0)
    for c in range(heads):
        o_ref[:, cols(c)] = (acc_ref[c] / l_ref[c]).T.astype(o_ref.dtype)


def _moba_prompt(qkv_b, vt, kmean, batch, seq):
    nq = seq // MOBA_BLOCK
    g = _MOBA_KV_PER_STEP
    heads = g * (N_HEADS // N_KV_HEADS)
    assert N_KV_HEADS % g == 0
    k_blk0 = Q_W // (g * HEAD_DIM)
    return pl.pallas_call(
        _moba_prompt_kernel,
        out_shape=jax.ShapeDtypeStruct((batch * seq, Q_W), BF16),
        grid=(batch, N_KV_HEADS // g, nq),
        in_specs=[
            pl.BlockSpec((MOBA_BLOCK, heads * HEAD_DIM), lambda b, gg, i: (b * nq + i, gg)),
            pl.BlockSpec((seq, g * HEAD_DIM), lambda b, gg, i: (b, k_blk0 + gg)),
            pl.BlockSpec((None, g, nq, HEAD_DIM, MOBA_BLOCK), lambda b, gg, i: (b, gg, 0, 0, 0)),
            pl.BlockSpec((None, nq, g * HEAD_DIM), lambda b, gg, i: (b, 0, gg)),
        ],
        out_specs=pl.BlockSpec((MOBA_BLOCK, heads * HEAD_DIM), lambda b, gg, i: (b * nq + i, gg)),
        scratch_shapes=[pltpu.VMEM((heads, nq, MOBA_BLOCK), F32),
                        pltpu.VMEM((heads, 1, MOBA_BLOCK), F32),
                        pltpu.VMEM((heads, 1, MOBA_BLOCK), F32),
                        pltpu.VMEM((heads, HEAD_DIM, MOBA_BLOCK), F32)],
        compiler_params=_params("parallel", "parallel", "arbitrary"),
        name="moba_prompt",
    )(qkv_b, qkv_b, vt, kmean)


_PAGES_PER_STEP = 8


def _cached_kmean_kernel(pt_ref, *refs):
    del pt_ref
    pages, o_ref = refs[:-1], refs[-1]
    for t in range(_PAGES_PER_STEP // PAGES_PER_BLOCK):
        tot = jnp.zeros((SUBLANES, HEAD_DIM), F32)
        for r in range(PAGES_PER_BLOCK):
            page = pages[PAGES_PER_BLOCK * t + r][0]
            tot = tot + jnp.sum(page.reshape(PAGE_ROWS // SUBLANES, SUBLANES, HEAD_DIM), axis=0)
        o_ref[0, 0, t] = (tot[0:N_KV_HEADS] + tot[N_KV_HEADS:2 * N_KV_HEADS]) * (1.0 / MOBA_BLOCK)


def _cached_kmeans(cache_k_rows, pt_flat, layer, n_pool, n_samples, n_pages):
    assert SUBLANES == 2 * N_KV_HEADS
    steps = n_pages // _PAGES_PER_STEP
    per_step = _PAGES_PER_STEP // PAGES_PER_BLOCK
    base = layer * n_pool

    def page_spec(r):
        return pl.BlockSpec((1, PAGE_ROWS, HEAD_DIM),
                            lambda b, g, pt: (base + pt[b * n_pages + g * _PAGES_PER_STEP + r], 0, 0))

    out = pl.pallas_call(
        _cached_kmean_kernel,
        out_shape=jax.ShapeDtypeStruct((n_samples, steps, per_step, N_KV_HEADS, HEAD_DIM), F32),
        grid_spec=pltpu.PrefetchScalarGridSpec(
            num_scalar_prefetch=1,
            grid=(n_samples, steps),
            in_specs=[page_spec(r) for r in range(_PAGES_PER_STEP)],
            out_specs=pl.BlockSpec((1, 1, per_step, N_KV_HEADS, HEAD_DIM), lambda b, g, pt: (b, g, 0, 0, 0)),
        ),
        compiler_params=_params("parallel", "parallel"),
        name="cached_kmeans",
    )(pt_flat, *([cache_k_rows] * _PAGES_PER_STEP))
    return out.reshape(n_samples, steps * per_step, KV_W)


def _cached_select_kernel(q_ref, km_ref, o_ref, *, n_blocks):
    rep = N_HEADS // N_KV_HEADS
    hrow = lax.broadcasted_iota(jnp.int32, (N_HEADS, LANES), 0)
    lane = lax.broadcasted_iota(jnp.int32, (N_HEADS, LANES), 1)
    q8 = _rows_to_sublanes(q_ref, N_HEADS, HEAD_DIM).astype(BF16)
    gate = jnp.zeros((N_HEADS, LANES), F32)
    for kvh in range(N_KV_HEADS):
        g = _dot_nt(q8, km_ref[0, :, kvh * HEAD_DIM:(kvh + 1) * HEAD_DIM].astype(BF16))
        gate = jnp.where(hrow >= kvh * rep, g, gate)
    valid = lane < n_blocks
    gate = jnp.where(valid, gate, MASKED)
    lane_f = lane.astype(F32)
    out = jnp.zeros((N_HEADS, LANES), F32)
    for r in range(MOBA_TOPK):
        m = jnp.max(gate, axis=-1, keepdims=True)
        first = jnp.min(jnp.where(gate == m, lane_f, float(LANES)), axis=-1, keepdims=True)
        out = jnp.where(lane == r, first, out)
        gate = jnp.where(lane_f == first, -jnp.inf, gate)
    o_ref[0] = out.astype(jnp.int32)


def _cached_select(qkv_f3, kmean_pad, n_blocks):
    n = qkv_f3.shape[0]
    return pl.pallas_call(
        functools.partial(_cached_select_kernel, n_blocks=n_blocks),
        out_shape=jax.ShapeDtypeStruct((n, N_HEADS, LANES), jnp.int32),
        grid=(n,),
        in_specs=[pl.BlockSpec((1, 1, Q_W), lambda b: (b, 0, 0)),
                  pl.BlockSpec((1, LANES, KV_W), lambda b: (b, 0, 0))],
        out_specs=pl.BlockSpec((1, N_HEADS, LANES), lambda b: (b, 0, 0)),
        compiler_params=_params("parallel"),
        name="cached_select",
    )(qkv_f3, kmean_pad)


def _cached_attend_kernel(pt_ref, sel_ref, q_ref, kn_ref, vn_ref, *refs):
    del pt_ref, sel_ref
    n_sel = MOBA_TOPK * PAGES_PER_BLOCK
    k_pages, v_pages, o_ref = refs[:n_sel], refs[n_sel:2 * n_sel], refs[-1]
    rep = N_HEADS // N_KV_HEADS
    kvh = pl.program_id(1) // rep
    scale = HEAD_DIM ** -0.5
    q = q_ref[0].astype(BF16)
    q8 = jnp.broadcast_to(q, (SUBLANES, HEAD_DIM))
    row_head = jnp.bitwise_and(lax.broadcasted_iota(jnp.int32, (1, PAGE_ROWS), 1), N_KV_HEADS - 1)
    mine = row_head == kvh
    s_pages = [jnp.where(mine, _dot_nt(q8, kp[0].astype(BF16))[0:1] * scale, MASKED) for kp in k_pages]
    kn = kn_ref[0].astype(BF16)
    s_new = jnp.sum(q.astype(F32) * kn.astype(F32), axis=-1, keepdims=True) * scale
    m = s_new
    for s in s_pages:
        m = jnp.maximum(m, jnp.max(s, axis=-1, keepdims=True))
    e_pages = [jnp.exp(s - m) for s in s_pages]
    e_new = jnp.exp(s_new - m)
    denom = e_new
    for e in e_pages:
        denom = denom + jnp.sum(e, axis=-1, keepdims=True)
    out = (e_new / denom).astype(BF16).astype(F32) * vn_ref[0].astype(BF16).astype(F32)
    for e, vp in zip(e_pages, v_pages):
        p8 = jnp.broadcast_to((e / denom).astype(BF16), (SUBLANES, PAGE_ROWS))
        out = out + _dot(p8, vp[0].astype(BF16))[0:1]
    o_ref[0] = out


def _cached_attend(qkv_f3, cache_k_rows, cache_v_rows, pt_flat, sel_flat, layer, n_pool, n_pages):
    n = qkv_f3.shape[0]
    rep = N_HEADS // N_KV_HEADS
    base = layer * n_pool
    k_col0 = Q_W // HEAD_DIM
    v_col0 = (Q_W + KV_W) // HEAD_DIM

    def page_spec(t, r):
        def index(b, h, pt, sel):
            block = sel[(b * N_HEADS + h) * MOBA_TOPK + t]
            return (base + pt[b * n_pages + block * PAGES_PER_BLOCK + r], 0, 0)
        return pl.BlockSpec((1, PAGE_ROWS, HEAD_DIM), index)

    page_specs = [page_spec(t, r) for t in range(MOBA_TOPK) for r in range(PAGES_PER_BLOCK)]
    return pl.pallas_call(
        _cached_attend_kernel,
        out_shape=jax.ShapeDtypeStruct((n, 1, Q_W), F32),
        grid_spec=pltpu.PrefetchScalarGridSpec(
            num_scalar_prefetch=2,
            grid=(n, N_HEADS),
            in_specs=[pl.BlockSpec((1, 1, HEAD_DIM), lambda b, h, pt, sel: (b, 0, h)),
                      pl.BlockSpec((1, 1, HEAD_DIM), lambda b, h, pt, sel: (b, 0, k_col0 + h // rep)),
                      pl.BlockSpec((1, 1, HEAD_DIM), lambda b, h, pt, sel: (b, 0, v_col0 + h // rep))]
            + page_specs + page_specs,
            out_specs=pl.BlockSpec((1, 1, HEAD_DIM), lambda b, h, pt, sel: (b, 0, h)),
        ),
        compiler_params=_params("parallel", "parallel"),
        name="cached_attend",
    )(pt_flat, sel_flat, qkv_f3, qkv_f3, qkv_f3,
      *([cache_k_rows] * len(page_specs)), *([cache_v_rows] * len(page_specs)))


def _sgu_chunk_kernel(zu_ref, zv_ref, g_ref, b_ref, ws_ref, bs_ref, s_ref):
    vn = _layer_norm(_gelu(zv_ref[...]), g_ref[...], b_ref[...])
    gu = _gelu(zu_ref[...])
    t = SGU_CHUNK
    row = lax.broadcasted_iota(jnp.int32, (t, t), 0)
    col = lax.broadcasted_iota(jnp.int32, (t, t), 1)
    for g in range(SGU_GROUPS):
        cs = slice(g * SGU_GROUP_DIM, (g + 1) * SGU_GROUP_DIM)
        ws = jnp.where(col <= row, ws_ref[g], 0.0).astype(BF16)
        bias = bs_ref[:, g:g + 1]
        for c in range(zu_ref.shape[0] // t):
            rs = slice(c * t, (c + 1) * t)
            mixed = _dot(ws, vn[rs, cs].astype(BF16)) + bias
            s_ref[rs, cs] = (gu[rs, cs] * mixed).astype(s_ref.dtype)


def _sgu_chunks(rest, ln_g, ln_b, w_s, b_s_t, tm):
    m = rest.shape[0]
    return pl.pallas_call(
        _sgu_chunk_kernel,
        out_shape=jax.ShapeDtypeStruct((m, SGU_WIDTH), BF16),
        grid=(m // tm,),
        in_specs=[pl.BlockSpec((tm, SGU_WIDTH), lambda i: (i, 0)),
                  pl.BlockSpec((tm, SGU_WIDTH), lambda i: (i, 1)),
                  pl.BlockSpec((1, SGU_WIDTH), lambda i: (0, 0)),
                  pl.BlockSpec((1, SGU_WIDTH), lambda i: (0, 0)),
                  pl.BlockSpec((SGU_GROUPS, SGU_CHUNK, SGU_CHUNK), lambda i: (0, 0, 0)),
                  pl.BlockSpec((SGU_CHUNK, SGU_GROUPS), lambda i: (0, 0))],
        out_specs=pl.BlockSpec((tm, SGU_WIDTH), lambda i: (i, 0)),
        compiler_params=_params("parallel"),
        name="sgu_chunks",
    )(rest, rest, ln_g, ln_b, w_s, b_s_t)


def _sgu_first_row_kernel(zu_ref, zv_ref, g_ref, b_ref, w0_ref, b0_ref, s_ref, vn_ref):
    vn = _layer_norm(_gelu(zv_ref[...]), g_ref[...], b_ref[...])
    vn_ref[...] = vn
    s_ref[...] = (_gelu(zu_ref[...]) * (vn * w0_ref[...] + b0_ref[...])).astype(s_ref.dtype)


def _sgu_first_rows(rest, ln_g, ln_b, w0, b0):
    m = rest.shape[0]
    vec = pl.BlockSpec((1, SGU_WIDTH), lambda i: (0, 0))
    return pl.pallas_call(
        _sgu_first_row_kernel,
        out_shape=[jax.ShapeDtypeStruct((m, SGU_WIDTH), BF16), jax.ShapeDtypeStruct((m, SGU_WIDTH), F32)],
        grid=(1,),
        in_specs=[pl.BlockSpec((m, SGU_WIDTH), lambda i: (0, 0)),
                  pl.BlockSpec((m, SGU_WIDTH), lambda i: (0, 1)), vec, vec, vec, vec],
        out_specs=[pl.BlockSpec((m, SGU_WIDTH), lambda i: (0, 0)), pl.BlockSpec((m, SGU_WIDTH), lambda i: (0, 0))],
        compiler_params=_params("arbitrary"),
        name="sgu_first_rows",
    )(rest, rest, ln_g, ln_b, w0, b0)


def _merge_kernel(a_ref, s_ref, ga_ref, gb_ref, bg_ref, wa_ref, wb_ref, o_ref):
    g_a = jax.nn.sigmoid(ga_ref[...] + bg_ref[0:1, :])
    g_b = jax.nn.sigmoid(gb_ref[...] + bg_ref[1:2, :])
    mix = g_a * _dot(a_ref[...], wa_ref[...]) + g_b * _dot(s_ref[...], wb_ref[...])
    o_ref[...] = mix.astype(o_ref.dtype)


def _merge(a, s, rest, b_gate, w_pa, w_pb, tm, tn):
    m = a.shape[0]
    tm = min(tm, m)
    ga0 = 2 * SGU_WIDTH // tn
    gb0 = (2 * SGU_WIDTH + D_MODEL) // tn
    return pl.pallas_call(
        _merge_kernel,
        out_shape=jax.ShapeDtypeStruct((m, D_MODEL), BF16),
        grid=(D_MODEL // tn, m // tm),
        in_specs=[pl.BlockSpec((tm, Q_W), lambda j, i: (i, 0)),
                  pl.BlockSpec((tm, SGU_WIDTH), lambda j, i: (i, 0)),
                  pl.BlockSpec((tm, tn), lambda j, i: (i, ga0 + j)),
                  pl.BlockSpec((tm, tn), lambda j, i: (i, gb0 + j)),
                  pl.BlockSpec((2, tn), lambda j, i: (0, j)),
                  pl.BlockSpec((Q_W, tn), lambda j, i: (0, j)),
                  pl.BlockSpec((SGU_WIDTH, tn), lambda j, i: (0, j))],
        out_specs=pl.BlockSpec((tm, tn), lambda j, i: (i, j)),
        compiler_params=_params("parallel", "parallel"),
        name="merge",
    )(a, s, rest, rest, b_gate, w_pa, w_pb)


def _proj_ln_kernel(a_ref, w_ref, x_ref, g_ref, b_ref, of_ref, ob_ref):
    z = DN_ALPHA * x_ref[...] + _dot(a_ref[...], w_ref[...])
    y = _layer_norm(z, g_ref[...], b_ref[...])
    of_ref[...] = y
    ob_ref[...] = y.astype(ob_ref.dtype)


def _proj_ln(a, w, x, g, b, tm):
    m, k = a.shape
    tm = min(tm, m)
    vec = pl.BlockSpec((1, D_MODEL), lambda i: (0, 0))
    return pl.pallas_call(
        _proj_ln_kernel,
        out_shape=[jax.ShapeDtypeStruct((m, D_MODEL), F32), jax.ShapeDtypeStruct((m, D_MODEL), BF16)],
        grid=(m // tm,),
        in_specs=[pl.BlockSpec((tm, k), lambda i: (i, 0)),
                  pl.BlockSpec((k, D_MODEL), lambda i: (0, 0)),
                  pl.BlockSpec((tm, D_MODEL), lambda i: (i, 0)), vec, vec],
        out_specs=[pl.BlockSpec((tm, D_MODEL), lambda i: (i, 0)), pl.BlockSpec((tm, D_MODEL), lambda i: (i, 0))],
        compiler_params=_params("parallel"),
        name="proj_ln",
    )(a, w, x, g, b)


def _xattn_kernel(q_ref, mk_ref, mv_ref, o_ref):
    scale = X_HEAD_DIM ** -0.5
    for h in range(X_HEADS):
        cs = slice(h * X_HEAD_DIM, (h + 1) * X_HEAD_DIM)
        s = _dot_nt(q_ref[0, :, cs], mk_ref[0, :, cs]) * scale
        e = jnp.exp(s - jnp.max(s, axis=-1, keepdims=True))
        p = (e / jnp.sum(e, axis=-1, keepdims=True)).astype(BF16)
        o_ref[0, :, cs] = _dot(p, mv_ref[0, :, cs]).astype(o_ref.dtype)


def _xattn(q3, mk3, mv3, tq):
    n, t, _ = q3.shape
    tq = min(tq, t)
    return pl.pallas_call(
        _xattn_kernel,
        out_shape=jax.ShapeDtypeStruct((n, t, X_W), BF16),
        grid=(n, t // tq),
        in_specs=[pl.BlockSpec((1, tq, X_W), lambda b, i: (b, i, 0)),
                  pl.BlockSpec((1, N_MEM, X_W), lambda b, i: (b, 0, 0)),
                  pl.BlockSpec((1, N_MEM, X_W), lambda b, i: (b, 0, 0))],
        out_specs=pl.BlockSpec((1, tq, X_W), lambda b, i: (b, i, 0)),
        compiler_params=_params("parallel", "parallel"),
        name="xattn",
    )(q3, mk3, mv3)


def _xattn_single_kernel(q_ref, mk_ref, mv_ref, o_ref):
    scale = X_HEAD_DIM ** -0.5
    rows = N_MEM * X_HEADS
    q8 = _rows_to_sublanes(q_ref, X_HEADS, X_HEAD_DIM).astype(BF16)
    head = lax.broadcasted_iota(jnp.int32, (SUBLANES, rows), 0)
    row_head = jnp.bitwise_and(lax.broadcasted_iota(jnp.int32, (SUBLANES, rows), 1), X_HEADS - 1)
    s = jnp.where(row_head == head, _dot_nt(q8, mk_ref[0].astype(BF16)) * scale, MASKED)
    e = jnp.exp(s - jnp.max(s, axis=-1, keepdims=True))
    p = (e / jnp.sum(e, axis=-1, keepdims=True)).astype(BF16)
    o = _dot(p, mv_ref[0].astype(BF16))
    for h in range(X_HEADS):
        o_ref[0, :, h * X_HEAD_DIM:(h + 1) * X_HEAD_DIM] = o[h:h + 1].astype(o_ref.dtype)


def _xattn_single(q3, mk_rows, mv_rows, mem_off):
    n = q3.shape[0]
    rows = N_MEM * X_HEADS
    return pl.pallas_call(
        _xattn_single_kernel,
        out_shape=jax.ShapeDtypeStruct((n, 1, X_W), F32),
        grid=(n,),
        in_specs=[pl.BlockSpec((1, 1, X_W), lambda b: (b, 0, 0)),
                  pl.BlockSpec((1, rows, X_HEAD_DIM), lambda b: (mem_off + b, 0, 0)),
                  pl.BlockSpec((1, rows, X_HEAD_DIM), lambda b: (mem_off + b, 0, 0))],
        out_specs=pl.BlockSpec((1, 1, X_W), lambda b: (b, 0, 0)),
        compiler_params=_params("parallel"),
        name="xattn_single",
    )(q3, mk_rows, mv_rows)


def _mlp_kernel(xb_ref, xf_ref, wu_ref, wd_ref, g_ref, b_ref, of_ref, ob_ref, acc_ref):
    f = pl.program_id(1)

    @pl.when(f == 0)
    def _():
        acc_ref[...] = jnp.zeros_like(acc_ref)

    h = jnp.maximum(_dot(xb_ref[...], wu_ref[...]), 0.0)
    acc_ref[...] += _dot((h * h).astype(BF16), wd_ref[...])

    @pl.when(f == pl.num_programs(1) - 1)
    def _():
        y = _layer_norm(DN_ALPHA * xf_ref[...] + acc_ref[...], g_ref[...], b_ref[...])
        of_ref[...] = y
        ob_ref[...] = y.astype(ob_ref.dtype)


def _mlp(xb, xf, w_up, w_down, g, b, tm, tf):
    m = xb.shape[0]
    tm = min(tm, m)
    vec = pl.BlockSpec((1, D_MODEL), lambda i, f: (0, 0))
    return pl.pallas_call(
        _mlp_kernel,
        out_shape=[jax.ShapeDtypeStruct((m, D_MODEL), F32), jax.ShapeDtypeStruct((m, D_MODEL), BF16)],
        grid=(m // tm, D_FF // tf),
        in_specs=[pl.BlockSpec((tm, D_MODEL), lambda i, f: (i, 0)),
                  pl.BlockSpec((tm, D_MODEL), lambda i, f: (i, 0)),
                  pl.BlockSpec((D_MODEL, tf), lambda i, f: (0, f)),
                  pl.BlockSpec((tf, D_MODEL), lambda i, f: (f, 0)), vec, vec],
        out_specs=[pl.BlockSpec((tm, D_MODEL), lambda i, f: (i, 0)),
                   pl.BlockSpec((tm, D_MODEL), lambda i, f: (i, 0))],
        scratch_shapes=[pltpu.VMEM((tm, D_MODEL), F32)],
        compiler_params=_params("parallel", "arbitrary"),
        name="mlp",
    )(xb, xf, w_up, w_down, g, b)


def _layer_tail(xf, a, s, rest, xattn_fn, w, tiles):
    mix = _merge(a, s, rest, w["b_gate"], w["w_pa"], w["w_pb"], tiles["merge_tm"], 1024)
    x1f, x1b = _proj_ln(mix, w["w_o"], xf, w["ln1_g"], w["ln1_b"], tiles["ln_tm"])
    o = xattn_fn(x1b)
    x2f, x2b = _proj_ln(o, w["w_xo"], x1f, w["ln2_g"], w["ln2_b"], tiles["ln_tm"])
    return _mlp(x2b, x2f, w["w_up"], w["w_down"], w["ln3_g"], w["ln3_b"], tiles["mlp_tm"], 512)


_PROMPT_TILES = dict(merge_tm=512, ln_tm=256, mlp_tm=512)
_SAMPLE_TILES = dict(merge_tm=32, ln_tm=32, mlp_tm=32)


def kernel(x_prompt, x_sample, mem_prompt, cache_k, cache_v, cache_mem_k, cache_mem_v, page_table,
           w_in, b_gate, sgu_ln_g, sgu_ln_b, w_s, b_s, w_pa, w_pb, w_o, ln1_g, ln1_b,
           w_xq, w_xk, w_xv, w_xo, ln2_g, ln2_b, w_up, w_down, ln3_g, ln3_b):
    batch, seq, _ = x_prompt.shape
    n_samples, dec_seq, _ = x_sample.shape
    assert dec_seq == 1 and seq % MOBA_BLOCK == 0
    assert N_KV_HEADS & (N_KV_HEADS - 1) == 0 and X_HEADS & (X_HEADS - 1) == 0 and X_HEADS <= SUBLANES
    n_pool = cache_k.shape[1]
    n_pages = page_table.shape[1]
    n_blocks = seq // MOBA_BLOCK
    n_past_blocks = n_pages // PAGES_PER_BLOCK
    assert n_pages % _PAGES_PER_STEP == 0 and n_past_blocks <= LANES
    mp, ms = batch * seq, n_samples

    pt_flat = page_table.reshape(-1).astype(jnp.int32)
    cache_k_rows = cache_k.reshape(DEPTH * n_pool, PAGE_ROWS, HEAD_DIM)
    cache_v_rows = cache_v.reshape(DEPTH * n_pool, PAGE_ROWS, HEAD_DIM)
    cache_mk_rows = cache_mem_k.reshape(DEPTH * n_samples, N_MEM * X_HEADS, X_HEAD_DIM)
    cache_mv_rows = cache_mem_v.reshape(DEPTH * n_samples, N_MEM * X_HEADS, X_HEAD_DIM)
    mem_b = mem_prompt.reshape(batch * N_MEM, D_MODEL).astype(BF16)

    xpf = x_prompt.reshape(mp, D_MODEL)
    xsf = x_sample.reshape(ms, D_MODEL)
    xpb, xsb = xpf.astype(BF16), xsf.astype(BF16)

    kp_l, vp_l, mkp_l, mvp_l, ks_l, vs_l, vns_l = [], [], [], [], [], [], []
    for l in range(DEPTH):
        w = dict(
            b_gate=b_gate[l], w_pa=w_pa[l].astype(BF16), w_pb=w_pb[l].astype(BF16), w_o=w_o[l].astype(BF16),
            ln1_g=ln1_g[l][None], ln1_b=ln1_b[l][None], w_xo=w_xo[l].astype(BF16),
            ln2_g=ln2_g[l][None], ln2_b=ln2_b[l][None], w_up=w_up[l].astype(BF16), w_down=w_down[l].astype(BF16),
            ln3_g=ln3_g[l][None], ln3_b=ln3_b[l][None])
        w_in_b = w_in[l].astype(BF16)
        w_xq_b = w_xq[l].astype(BF16)
        sgu_g, sgu_b = sgu_ln_g[l][None], sgu_ln_b[l][None]

        mk_f, mk_b = _matmul(mem_b, w_xk[l].astype(BF16), 0, X_W, (F32, BF16), 512, X_W)
        mv_f, mv_b = _matmul(mem_b, w_xv[l].astype(BF16), 0, X_W, (F32, BF16), 512, X_W)
        qkv_f, qkv_b = _matmul(xpb, w_in_b, 0, QKV_W, (F32, BF16), 1024, 1024)
        (rest,) = _matmul(xpb, w_in_b, QKV_W, REST_W, (F32,), 1024, 1024)
        kmean = _prompt_kmeans(qkv_f).reshape(batch, n_blocks, KV_W)
        vt = qkv_b[:, Q_W + KV_W:].reshape(batch, n_blocks, MOBA_BLOCK, N_KV_HEADS, HEAD_DIM)
        vt = jnp.transpose(vt, (0, 3, 1, 4, 2))
        a = _moba_prompt(qkv_b, vt, kmean, batch, seq)
        s = _sgu_chunks(rest, sgu_g, sgu_b, w_s[l], b_s[l].T, 256)

        def prompt_xattn(x1b, mk_b=mk_b, mv_b=mv_b, w_xq_b=w_xq_b):
            (qx,) = _matmul(x1b, w_xq_b, 0, X_W, (BF16,), 1024, X_W)
            o = _xattn(qx.reshape(batch, seq, X_W), mk_b.reshape(batch, N_MEM, X_W),
                       mv_b.reshape(batch, N_MEM, X_W), 512)
            return o.reshape(mp, X_W)

        xpf, xpb = _layer_tail(xpf, a, s, rest, prompt_xattn, w, _PROMPT_TILES)
        kp_l.append(qkv_f[:, Q_W:Q_W + KV_W].reshape(batch, seq // PAGE_SIZE, PAGE_SIZE, N_KV_HEADS, HEAD_DIM))
        vp_l.append(qkv_f[:, Q_W + KV_W:].reshape(batch, seq // PAGE_SIZE, PAGE_SIZE, N_KV_HEADS, HEAD_DIM))
        mkp_l.append(mk_f.reshape(batch, N_MEM, X_HEADS, X_HEAD_DIM))
        mvp_l.append(mv_f.reshape(batch, N_MEM, X_HEADS, X_HEAD_DIM))

        (qkv_s,) = _matmul(xsb, w_in_b, 0, QKV_W, (F32,), 32, 1024)
        (rest_s,) = _matmul(xsb, w_in_b, QKV_W, REST_W, (F32,), 32, 1024)
        qkv_s3 = qkv_s.reshape(ms, 1, QKV_W)
        kmean_s = _cached_kmeans(cache_k_rows, pt_flat, l, n_pool, ms, n_pages)
        kmean_s = jnp.pad(kmean_s, ((0, 0), (0, LANES - n_past_blocks), (0, 0)))
        sel = _cached_select(qkv_s3, kmean_s, n_past_blocks)[:, :, :MOBA_TOPK].reshape(-1)
        a_s = _cached_attend(qkv_s3, cache_k_rows, cache_v_rows, pt_flat, sel, l, n_pool, n_pages)
        a_s = a_s.reshape(ms, Q_W).astype(BF16)
        w0 = jnp.repeat(w_s[l][:, 0, 0], SGU_GROUP_DIM)[None]
        b0 = jnp.repeat(b_s[l][:, 0], SGU_GROUP_DIM)[None]
        s_s, vn_s = _sgu_first_rows(rest_s, sgu_g, sgu_b, w0, b0)

        def sample_xattn(x1b, l=l, w_xq_b=w_xq_b):
            (qx,) = _matmul(x1b, w_xq_b, 0, X_W, (F32,), 32, X_W)
            o = _xattn_single(qx.reshape(ms, 1, X_W), cache_mk_rows, cache_mv_rows, l * n_samples)
            return o.reshape(ms, X_W).astype(BF16)

        xsf, xsb = _layer_tail(xsf, a_s, s_s, rest_s, sample_xattn, w, _SAMPLE_TILES)
        ks_l.append(qkv_s[:, Q_W:Q_W + KV_W].reshape(ms, 1, N_KV_HEADS, HEAD_DIM))
        vs_l.append(qkv_s[:, Q_W + KV_W:].reshape(ms, 1, N_KV_HEADS, HEAD_DIM))
        vns_l.append(vn_s.reshape(ms, 1, SGU_WIDTH))

    return (xpf.reshape(batch, seq, D_MODEL), xsf.reshape(ms, 1, D_MODEL),
            jnp.stack(kp_l), jnp.stack(vp_l), jnp.stack(mkp_l), jnp.stack(mvp_l),
            jnp.stack(ks_l), jnp.stack(vs_l), jnp.stack(vns_l))
```

```python
import functools

import jax
import jax.numpy as jnp
import numpy as np
from jax import lax
from jax.experimental import pallas as pl
from jax.experimental.pallas import tpu as pltpu

D_MODEL = 2048
DEPTH = 2
PAGE_SIZE = 128
N_HEADS = 8
N_KV_HEADS = 4
HEAD_DIM = 128
MOBA_BLOCK = 256
MOBA_TOPK = 3
SGU_WIDTH = 1024
SGU_GROUPS = 8
SGU_GROUP_DIM = SGU_WIDTH // SGU_GROUPS
SGU_CHUNK = 128
N_MEM = 256
X_HEADS = 4
X_HEAD_DIM = 128
D_FF = 4 * D_MODEL
DN_ALPHA = (2 * DEPTH) ** 0.25
LN_EPS = 1e-5
Q_W = N_HEADS * HEAD_DIM
KV_W = N_KV_HEADS * HEAD_DIM
X_W = X_HEADS * X_HEAD_DIM
QKV_W = Q_W + 2 * KV_W
REST_W = 2 * SGU_WIDTH + 2 * D_MODEL
PAGES_PER_BLOCK = MOBA_BLOCK // PAGE_SIZE
PAGE_ROWS = PAGE_SIZE * N_KV_HEADS
MASKED = -1e30

LANES = 128
SUBLANES = 8
VMEM_LIMIT = 56 * 1024 * 1024

BF16 = jnp.bfloat16
F32 = jnp.float32
_NT = (((1,), (1,)), ((), ()))


def _params(*sem):
    return pltpu.CompilerParams(dimension_semantics=sem, vmem_limit_bytes=VMEM_LIMIT)


def _dot(a, b):
    return jnp.dot(a, b, preferred_element_type=F32)


def _dot_nt(a, b):
    return lax.dot_general(a, b, _NT, preferred_element_type=F32)


def _gelu(x):
    c = np.float32(np.sqrt(2 / np.pi))
    return x * (0.5 * (1.0 + jnp.tanh(c * (x + 0.044715 * (x * x * x)))))


def _layer_norm(z, g, b):
    mu = jnp.mean(z, axis=-1, keepdims=True)
    d = z - mu
    var = jnp.mean(d * d, axis=-1, keepdims=True)
    return d * lax.rsqrt(var + LN_EPS) * g + b


def _top_blocks(gate, valid_f, idx_f, axis):
    sel = jnp.zeros(gate.shape, F32)
    for _ in range(MOBA_TOPK):
        m = jnp.max(gate, axis=axis, keepdims=True)
        first = jnp.min(jnp.where(gate == m, idx_f, float(gate.shape[axis])), axis=axis, keepdims=True)
        pick = idx_f == first
        sel = jnp.where(pick, valid_f, sel)
        gate = jnp.where(pick, -jnp.inf, gate)
    return sel


def _rows_to_sublanes(ref, n_rows, width):
    sub = lax.broadcasted_iota(jnp.int32, (SUBLANES, width), 0)
    out = jnp.zeros((SUBLANES, width), F32)
    for r in range(n_rows):
        out = jnp.where(sub == r, ref[0, :, r * width:(r + 1) * width].astype(F32), out)
    return out


def _mm_kernel(x_ref, w_ref, *out_refs):
    acc = _dot(x_ref[...], w_ref[...])
    for o in out_refs:
        o[...] = acc.astype(o.dtype)


def _cast_kernel(x_ref, o_ref):
    o_ref[...] = x_ref[...].astype(o_ref.dtype)


_CAST_BLOCK_BYTES = 4 * 1024 * 1024


def _to_bf16(w):
    d, k, n = w.shape
    rows = d * k
    tr = min(rows, max(SUBLANES, _CAST_BLOCK_BYTES // (4 * n)))
    assert rows % tr == 0
    out = pl.pallas_call(
        _cast_kernel,
        out_shape=jax.ShapeDtypeStruct((rows, n), BF16),
        grid=(rows // tr,),
        in_specs=[pl.BlockSpec((tr, n), lambda i: (i, 0))],
        out_specs=pl.BlockSpec((tr, n), lambda i: (i, 0)),
        compiler_params=_params("parallel"),
        name="to_bf16",
    )(w.reshape(rows, n))
    return out.reshape(d, k, n)


def _layer_spec(layer, block, index):
    return pl.BlockSpec((None,) + block, lambda *g: (layer,) + index(*g))


def _matmul(x, w, layer, col_off, ncols, out_dtypes, tm, tn):
    m, k = x.shape
    tm = min(tm, m)
    tn = min(tn, ncols)
    assert m % tm == 0 and ncols % tn == 0 and col_off % tn == 0
    joff = col_off // tn
    outs = pl.pallas_call(
        _mm_kernel,
        out_shape=[jax.ShapeDtypeStruct((m, ncols), dt) for dt in out_dtypes],
        grid=(ncols // tn, m // tm),
        in_specs=[pl.BlockSpec((tm, k), lambda j, i: (i, 0)),
                  _layer_spec(layer, (k, tn), lambda j, i: (0, j + joff))],
        out_specs=[pl.BlockSpec((tm, tn), lambda j, i: (i, j)) for _ in out_dtypes],
        compiler_params=_params("parallel", "parallel"),
        name="matmul",
    )(x, w)
    return outs


def _kmean_kernel(k_ref, o_ref):
    o_ref[0] = jnp.sum(k_ref[...], axis=0, keepdims=True) * (1.0 / MOBA_BLOCK)


def _prompt_kmeans(qkv_f):
    m = qkv_f.shape[0]
    nb = m // MOBA_BLOCK
    return pl.pallas_call(
        _kmean_kernel,
        out_shape=jax.ShapeDtypeStruct((nb, 1, KV_W), F32),
        grid=(nb,),
        in_specs=[pl.BlockSpec((MOBA_BLOCK, KV_W), lambda i: (i, Q_W // KV_W))],
        out_specs=pl.BlockSpec((1, 1, KV_W), lambda i: (i, 0, 0)),
        compiler_params=_params("parallel"),
        name="prompt_kmeans",
    )(qkv_f)


_MOBA_KV_PER_STEP = 4


def _moba_prompt_kernel(q_ref, k_ref, vt_ref, km_ref, o_ref, sel_ref, m_ref, l_ref, acc_ref):
    i = pl.program_id(2)
    rep = N_HEADS // N_KV_HEADS
    heads = _MOBA_KV_PER_STEP * rep
    blk = MOBA_BLOCK
    scale_log2e = np.float32(HEAD_DIM ** -0.5 * np.log2(np.e))

    def cols(c):
        return slice(c * HEAD_DIM, (c + 1) * HEAD_DIM)

    qs = [q_ref[:, cols(c)] for c in range(heads)]
    blk_id = lax.broadcasted_iota(jnp.int32, (km_ref.shape[0], blk), 0)
    valid = blk_id < i
    for c in range(heads):
        km = km_ref[:, cols(c // rep)].astype(BF16)
        gate = jnp.where(valid, _dot_nt(km, qs[c]), MASKED)
        sel_ref[c] = _top_blocks(gate, valid.astype(F32), blk_id.astype(F32), 0)

    def attend(j, masks, first):
        kjs = [k_ref[pl.ds(pl.multiple_of(j * blk, blk), blk), cols(g)] for g in range(_MOBA_KV_PER_STEP)]
        scores = [_dot_nt(kjs[c // rep], qs[c]) for c in range(heads)]
        ps, m_news, l_blks = [], [], []
        for c in range(heads):
            s = jnp.where(masks[c], scores[c] * scale_log2e, MASKED)
            m_blk = jnp.max(s, axis=0, keepdims=True)
            m_new = m_blk if first else jnp.maximum(m_ref[c], m_blk)
            p = jnp.exp2(s - m_new)
            l_blks.append(jnp.sum(p, axis=0, keepdims=True))
            ps.append(p.astype(BF16))
            m_news.append(m_new)
        pvs = [_dot(vt_ref[c // rep, j], ps[c]) for c in range(heads)]
        for c in range(heads):
            if first:
                l_ref[c] = l_blks[c]
                acc_ref[c] = pvs[c]
            else:
                a = jnp.exp2(m_ref[c] - m_news[c])
                l_ref[c] = a * l_ref[c] + l_blks[c]
                acc_ref[c] = a * acc_ref[c] + pvs[c]
            m_ref[c] = m_news[c]

    key = lax.broadcasted_iota(jnp.int32, (blk, blk), 0)
    qry = lax.broadcasted_iota(jnp.int32, (blk, blk), 1)
    attend(i, [key <= qry] * heads, True)

    def body(j, carry):
        attend(j, [sel_ref[c, pl.ds(j, 1), :] > 0.5 for c in range(heads)], False)
        return carry

    lax.fori_loop(0, i, body, 0)
    for c in range(heads):
        o_ref[:, cols(c)] = (acc_ref[c] / l_ref[c]).T.astype(o_ref.dtype)


def _moba_prompt(qkv_b, vt, kmean, batch, seq):
    nq = seq // MOBA_BLOCK
    g = _MOBA_KV_PER_STEP
    heads = g * (N_HEADS // N_KV_HEADS)
    assert N_KV_HEADS % g == 0
    k_blk0 = Q_W // (g * HEAD_DIM)
    return pl.pallas_call(
        _moba_prompt_kernel,
        out_shape=jax.ShapeDtypeStruct((batch * seq, Q_W), BF16),
        grid=(batch, N_KV_HEADS // g, nq),
        in_specs=[
            pl.BlockSpec((MOBA_BLOCK, heads * HEAD_DIM), lambda b, gg, i: (b * nq + i, gg)),
            pl.BlockSpec((seq, g * HEAD_DIM), lambda b, gg, i: (b, k_blk0 + gg)),
            pl.BlockSpec((None, g, nq, HEAD_DIM, MOBA_BLOCK), lambda b, gg, i: (b, gg, 0, 0, 0)),
            pl.BlockSpec((None, nq, g * HEAD_DIM), lambda b, gg, i: (b, 0, gg)),
        ],
        out_specs=pl.BlockSpec((MOBA_BLOCK, heads * HEAD_DIM), lambda b, gg, i: (b * nq + i, gg)),
        scratch_shapes=[pltpu.VMEM((heads, nq, MOBA_BLOCK), F32),
                        pltpu.VMEM((heads, 1, MOBA_BLOCK), F32),
                        pltpu.VMEM((heads, 1, MOBA_BLOCK), F32),
                        pltpu.VMEM((heads, HEAD_DIM, MOBA_BLOCK), F32)],
        compiler_params=_params("parallel", "parallel", "arbitrary"),
        name="moba_prompt",
    )(qkv_b, qkv_b, vt, kmean)


def _cached_moba_kernel(pt_ref, qkv_ref, ck_hbm, cv_hbm, o_ref, kbuf, vbuf, km_ref, ksem, vsem,
                        *, layer, n_pool, n_pages):
    b = pl.program_id(0)
    n = pl.num_programs(0)
    slot = b % 2
    rep = N_HEADS // N_KV_HEADS
    n_blocks = n_pages // PAGES_PER_BLOCK
    scale = HEAD_DIM ** -0.5
    base = layer * n_pool

    def k_copy(sample, page_slot, sl):
        page = base + pt_ref[sample * n_pages + page_slot]
        return pltpu.make_async_copy(ck_hbm.at[page], kbuf.at[sl, page_slot], ksem.at[sl])

    def start_keys(sample, sl):
        for p in range(n_pages):
            k_copy(sample, p, sl).start()

    @pl.when(b == 0)
    def _():
        km_ref[...] = jnp.zeros_like(km_ref)
        start_keys(0, 0)

    @pl.when(b + 1 < n)
    def _():
        start_keys(b + 1, 1 - slot)

    for p in range(n_pages):
        k_copy(b, p, slot).wait()

    for blk in range(n_blocks):
        tot = jnp.zeros((SUBLANES, HEAD_DIM), F32)
        for r in range(PAGES_PER_BLOCK):
            page = kbuf[slot, PAGES_PER_BLOCK * blk + r]
            tot = tot + jnp.sum(page.reshape(PAGE_ROWS // SUBLANES, SUBLANES, HEAD_DIM), axis=0)
        km_ref[blk * SUBLANES:(blk + 1) * SUBLANES, :] = tot + pltpu.roll(tot, N_KV_HEADS, axis=0)

    q8 = _rows_to_sublanes(qkv_ref, N_HEADS, HEAD_DIM)
    q8b = q8.astype(BF16)
    hrow = lax.broadcasted_iota(jnp.int32, (N_HEADS, LANES), 0)
    lane = lax.broadcasted_iota(jnp.int32, (N_HEADS, LANES), 1)
    gate = jnp.zeros((N_HEADS, LANES), F32)
    for kvh in range(N_KV_HEADS):
        km = km_ref[pl.ds(kvh, LANES, stride=SUBLANES), :] * (1.0 / MOBA_BLOCK)
        gate = jnp.where(hrow >= kvh * rep, _dot_nt(q8b, km.astype(BF16)), gate)
    gate = jnp.where(lane < n_blocks, gate, MASKED)
    lane_f = lane.astype(F32)
    picks = []
    for _ in range(MOBA_TOPK):
        m = jnp.max(gate, axis=-1, keepdims=True)
        first = jnp.min(jnp.where(gate == m, lane_f, float(LANES)), axis=-1, keepdims=True)
        picks.append(first.astype(jnp.int32))
        gate = jnp.where(lane_f == first, -jnp.inf, gate)
    blocks = [[picks[t][h, 0] for t in range(MOBA_TOPK)] for h in range(N_HEADS)]

    def v_copy(h, t, r):
        page = pt_ref[b * n_pages + blocks[h][t] * PAGES_PER_BLOCK + r]
        return pltpu.make_async_copy(cv_hbm.at[layer, page, :, h // rep, :],
                                     vbuf.at[(h * MOBA_TOPK + t) * PAGES_PER_BLOCK + r], vsem)

    sel = [(h, t, r) for h in range(N_HEADS) for t in range(MOBA_TOPK) for r in range(PAGES_PER_BLOCK)]
    for h, t, r in sel:
        v_copy(h, t, r).start()

    k_new = qkv_ref[0, :, Q_W:Q_W + KV_W]
    v_new = qkv_ref[0, :, Q_W + KV_W:]
    scores = {}
    for h, t, r in sel:
        kvh = h // rep
        kp = kbuf[slot, blocks[h][t] * PAGES_PER_BLOCK + r, pl.ds(kvh, PAGE_SIZE, stride=N_KV_HEADS), :]
        qh = jnp.broadcast_to(q8b[h:h + 1], (SUBLANES, HEAD_DIM))
        scores[h, t, r] = _dot_nt(qh, kp.astype(BF16))[0:1] * scale
    probs, p_new = {}, []
    for h in range(N_HEADS):
        kvh = h // rep
        kn = k_new[:, kvh * HEAD_DIM:(kvh + 1) * HEAD_DIM].astype(BF16).astype(F32)
        s_new = jnp.sum(q8b[h:h + 1].astype(F32) * kn, axis=-1, keepdims=True) * scale
        mine = [scores[h, t, r] for t in range(MOBA_TOPK) for r in range(PAGES_PER_BLOCK)]
        m = s_new
        for s in mine:
            m = jnp.maximum(m, jnp.max(s, axis=-1, keepdims=True))
        es = [jnp.exp(s - m) for s in mine]
        e_new = jnp.exp(s_new - m)
        denom = e_new
        for e in es:
            denom = denom + jnp.sum(e, axis=-1, keepdims=True)
        p_new.append((e_new / denom).astype(BF16).astype(F32))
        for idx, (t, r) in enumerate((t, r) for t in range(MOBA_TOPK) for r in range(PAGES_PER_BLOCK)):
            probs[h, t, r] = (es[idx] / denom).astype(BF16)

    for h, t, r in sel:
        v_copy(h, t, r).wait()

    for h in range(N_HEADS):
        kvh = h // rep
        out = p_new[h] * v_new[:, kvh * HEAD_DIM:(kvh + 1) * HEAD_DIM].astype(BF16).astype(F32)
        for t in range(MOBA_TOPK):
            for r in range(PAGES_PER_BLOCK):
                p8 = jnp.broadcast_to(probs[h, t, r], (SUBLANES, PAGE_SIZE))
                vp = vbuf[(h * MOBA_TOPK + t) * PAGES_PER_BLOCK + r]
                out = out + _dot(p8, vp.astype(BF16))[0:1]
        o_ref[0, :, h * HEAD_DIM:(h + 1) * HEAD_DIM] = out


def _cached_moba(qkv_f3, cache_k_rows, cache_v, pt_flat, layer, n_pool, n_pages):
    n = qkv_f3.shape[0]
    assert SUBLANES == 2 * N_KV_HEADS and n_pages // PAGES_PER_BLOCK <= LANES
    return pl.pallas_call(
        functools.partial(_cached_moba_kernel, layer=layer, n_pool=n_pool, n_pages=n_pages),
        out_shape=jax.ShapeDtypeStruct((n, 1, Q_W), F32),
        grid_spec=pltpu.PrefetchScalarGridSpec(
            num_scalar_prefetch=1,
            grid=(n,),
            in_specs=[pl.BlockSpec((1, 1, QKV_W), lambda b, pt: (b, 0, 0)),
                      pl.BlockSpec(memory_space=pl.ANY),
                      pl.BlockSpec(memory_space=pl.ANY)],
            out_specs=pl.BlockSpec((1, 1, Q_W), lambda b, pt: (b, 0, 0)),
            scratch_shapes=[pltpu.VMEM((2, n_pages, PAGE_ROWS, HEAD_DIM), F32),
                            pltpu.VMEM((N_HEADS * MOBA_TOPK * PAGES_PER_BLOCK, PAGE_SIZE, HEAD_DIM), F32),
                            pltpu.VMEM((LANES * SUBLANES, HEAD_DIM), F32),
                            pltpu.SemaphoreType.DMA((2,)),
                            pltpu.SemaphoreType.DMA(())],
        ),
        compiler_params=_params("arbitrary"),
        name="cached_moba",
    )(pt_flat, qkv_f3, cache_k_rows, cache_v)


def _sgu_chunk_kernel(zu_ref, zv_ref, g_ref, b_ref, ws_ref, bs_ref, s_ref):
    vn = _layer_norm(_gelu(zv_ref[...]), g_ref[...], b_ref[...])
    gu = _gelu(zu_ref[...])
    t = SGU_CHUNK
    row = lax.broadcasted_iota(jnp.int32, (t, t), 0)
    col = lax.broadcasted_iota(jnp.int32, (t, t), 1)
    for g in range(SGU_GROUPS):
        cs = slice(g * SGU_GROUP_DIM, (g + 1) * SGU_GROUP_DIM)
        ws = jnp.where(col <= row, ws_ref[g], 0.0).astype(BF16)
        bias = bs_ref[:, g:g + 1]
        for c in range(zu_ref.shape[0] // t):
            rs = slice(c * t, (c + 1) * t)
            mixed = _dot(ws, vn[rs, cs].astype(BF16)) + bias
            s_ref[rs, cs] = (gu[rs, cs] * mixed).astype(s_ref.dtype)


def _sgu_chunks(rest, ln_g, ln_b, w_s, b_s_t, tm):
    m = rest.shape[0]
    return pl.pallas_call(
        _sgu_chunk_kernel,
        out_shape=jax.ShapeDtypeStruct((m, SGU_WIDTH), BF16),
        grid=(m // tm,),
        in_specs=[pl.BlockSpec((tm, SGU_WIDTH), lambda i: (i, 0)),
                  pl.BlockSpec((tm, SGU_WIDTH), lambda i: (i, 1)),
                  pl.BlockSpec((1, SGU_WIDTH), lambda i: (0, 0)),
                  pl.BlockSpec((1, SGU_WIDTH), lambda i: (0, 0)),
                  pl.BlockSpec((SGU_GROUPS, SGU_CHUNK, SGU_CHUNK), lambda i: (0, 0, 0)),
                  pl.BlockSpec((SGU_CHUNK, SGU_GROUPS), lambda i: (0, 0))],
        out_specs=pl.BlockSpec((tm, SGU_WIDTH), lambda i: (i, 0)),
        compiler_params=_params("parallel"),
        name="sgu_chunks",
    )(rest, rest, ln_g, ln_b, w_s, b_s_t)


def _sgu_first_row_kernel(zu_ref, zv_ref, g_ref, b_ref, w0_ref, b0_ref, s_ref, vn_ref):
    vn = _layer_norm(_gelu(zv_ref[...]), g_ref[...], b_ref[...])
    vn_ref[...] = vn
    s_ref[...] = (_gelu(zu_ref[...]) * (vn * w0_ref[...] + b0_ref[...])).astype(s_ref.dtype)


def _sgu_first_rows(rest, ln_g, ln_b, w0, b0):
    m = rest.shape[0]
    vec = pl.BlockSpec((1, SGU_WIDTH), lambda i: (0, 0))
    return pl.pallas_call(
        _sgu_first_row_kernel,
        out_shape=[jax.ShapeDtypeStruct((m, SGU_WIDTH), BF16), jax.ShapeDtypeStruct((m, SGU_WIDTH), F32)],
        grid=(1,),
        in_specs=[pl.BlockSpec((m, SGU_WIDTH), lambda i: (0, 0)),
                  pl.BlockSpec((m, SGU_WIDTH), lambda i: (0, 1)), vec, vec, vec, vec],
        out_specs=[pl.BlockSpec((m, SGU_WIDTH), lambda i: (0, 0)), pl.BlockSpec((m, SGU_WIDTH), lambda i: (0, 0))],
        compiler_params=_params("arbitrary"),
        name="sgu_first_rows",
    )(rest, rest, ln_g, ln_b, w0, b0)


def _merge_kernel(a_ref, s_ref, ga_ref, gb_ref, bg_ref, wa_ref, wb_ref, o_ref):
    g_a = jax.nn.sigmoid(ga_ref[...] + bg_ref[0:1, :])
    g_b = jax.nn.sigmoid(gb_ref[...] + bg_ref[1:2, :])
    mix = g_a * _dot(a_ref[...], wa_ref[...]) + g_b * _dot(s_ref[...], wb_ref[...])
    o_ref[...] = mix.astype(o_ref.dtype)


def _merge(a, s, rest, b_gate, w_pa, w_pb, layer, tm, tn):
    m = a.shape[0]
    tm = min(tm, m)
    ga0 = 2 * SGU_WIDTH // tn
    gb0 = (2 * SGU_WIDTH + D_MODEL) // tn
    return pl.pallas_call(
        _merge_kernel,
        out_shape=jax.ShapeDtypeStruct((m, D_MODEL), BF16),
        grid=(D_MODEL // tn, m // tm),
        in_specs=[pl.BlockSpec((tm, Q_W), lambda j, i: (i, 0)),
                  pl.BlockSpec((tm, SGU_WIDTH), lambda j, i: (i, 0)),
                  pl.BlockSpec((tm, tn), lambda j, i: (i, ga0 + j)),
                  pl.BlockSpec((tm, tn), lambda j, i: (i, gb0 + j)),
                  _layer_spec(layer, (2, tn), lambda j, i: (0, j)),
                  _layer_spec(layer, (Q_W, tn), lambda j, i: (0, j)),
                  _layer_spec(layer, (SGU_WIDTH, tn), lambda j, i: (0, j))],
        out_specs=pl.BlockSpec((tm, tn), lambda j, i: (i, j)),
        compiler_params=_params("parallel", "parallel"),
        name="merge",
    )(a, s, rest, rest, b_gate, w_pa, w_pb)


def _proj_ln_kernel(a_ref, w_ref, x_ref, g_ref, b_ref, of_ref, ob_ref):
    z = DN_ALPHA * x_ref[...] + _dot(a_ref[...], w_ref[...])
    y = _layer_norm(z, g_ref[...], b_ref[...])
    of_ref[...] = y
    ob_ref[...] = y.astype(ob_ref.dtype)


def _proj_ln(a, w, x, g, b, layer, tm):
    m, k = a.shape
    tm = min(tm, m)
    vec = _layer_spec(layer, (1, D_MODEL), lambda i: (0, 0))
    return pl.pallas_call(
        _proj_ln_kernel,
        out_shape=[jax.ShapeDtypeStruct((m, D_MODEL), F32), jax.ShapeDtypeStruct((m, D_MODEL), BF16)],
        grid=(m // tm,),
        in_specs=[pl.BlockSpec((tm, k), lambda i: (i, 0)),
                  _layer_spec(layer, (k, D_MODEL), lambda i: (0, 0)),
                  pl.BlockSpec((tm, D_MODEL), lambda i: (i, 0)), vec, vec],
        out_specs=[pl.BlockSpec((tm, D_MODEL), lambda i: (i, 0)), pl.BlockSpec((tm, D_MODEL), lambda i: (i, 0))],
        compiler_params=_params("parallel"),
        name="proj_ln",
    )(a, w, x, g, b)


def _xattn_kernel(q_ref, mk_ref, mv_ref, o_ref):
    scale = X_HEAD_DIM ** -0.5
    for h in range(X_HEADS):
        cs = slice(h * X_HEAD_DIM, (h + 1) * X_HEAD_DIM)
        s = _dot_nt(q_ref[0, :, cs], mk_ref[0, :, cs]) * scale
        e = jnp.exp(s - jnp.max(s, axis=-1, keepdims=True))
        p = (e / jnp.sum(e, axis=-1, keepdims=True)).astype(BF16)
        o_ref[0, :, cs] = _dot(p, mv_ref[0, :, cs]).astype(o_ref.dtype)


def _xattn(q3, mk3, mv3, tq):
    n, t, _ = q3.shape
    tq = min(tq, t)
    return pl.pallas_call(
        _xattn_kernel,
        out_shape=jax.ShapeDtypeStruct((n, t, X_W), BF16),
        grid=(n, t // tq),
        in_specs=[pl.BlockSpec((1, tq, X_W), lambda b, i: (b, i, 0)),
                  pl.BlockSpec((1, N_MEM, X_W), lambda b, i: (b, 0, 0)),
                  pl.BlockSpec((1, N_MEM, X_W), lambda b, i: (b, 0, 0))],
        out_specs=pl.BlockSpec((1, tq, X_W), lambda b, i: (b, i, 0)),
        compiler_params=_params("parallel", "parallel"),
        name="xattn",
    )(q3, mk3, mv3)


def _xattn_single_kernel(q_ref, mk_ref, mv_ref, o_ref):
    scale = X_HEAD_DIM ** -0.5
    rows = N_MEM * X_HEADS
    q8 = _rows_to_sublanes(q_ref, X_HEADS, X_HEAD_DIM).astype(BF16)
    head = lax.broadcasted_iota(jnp.int32, (SUBLANES, rows), 0)
    row_head = jnp.bitwise_and(lax.broadcasted_iota(jnp.int32, (SUBLANES, rows), 1), X_HEADS - 1)
    s = jnp.where(row_head == head, _dot_nt(q8, mk_ref[0].astype(BF16)) * scale, MASKED)
    e = jnp.exp(s - jnp.max(s, axis=-1, keepdims=True))
    p = (e / jnp.sum(e, axis=-1, keepdims=True)).astype(BF16)
    o = _dot(p, mv_ref[0].astype(BF16))
    for h in range(X_HEADS):
        o_ref[0, :, h * X_HEAD_DIM:(h + 1) * X_HEAD_DIM] = o[h:h + 1].astype(o_ref.dtype)


def _xattn_single(q3, mk_rows, mv_rows, mem_off):
    n = q3.shape[0]
    rows = N_MEM * X_HEADS
    return pl.pallas_call(
        _xattn_single_kernel,
        out_shape=jax.ShapeDtypeStruct((n, 1, X_W), F32),
        grid=(n,),
        in_specs=[pl.BlockSpec((1, 1, X_W), lambda b: (b, 0, 0)),
                  pl.BlockSpec((1, rows, X_HEAD_DIM), lambda b: (mem_off + b, 0, 0)),
                  pl.BlockSpec((1, rows, X_HEAD_DIM), lambda b: (mem_off + b, 0, 0))],
        out_specs=pl.BlockSpec((1, 1, X_W), lambda b: (b, 0, 0)),
        compiler_params=_params("parallel"),
        name="xattn_single",
    )(q3, mk_rows, mv_rows)


def _mlp_kernel(xb_ref, xf_ref, wu_ref, wd_ref, g_ref, b_ref, of_ref, ob_ref, acc_ref):
    f = pl.program_id(1)

    @pl.when(f == 0)
    def _():
        acc_ref[...] = jnp.zeros_like(acc_ref)

    h = jnp.maximum(_dot(xb_ref[...], wu_ref[...]), 0.0)
    acc_ref[...] += _dot((h * h).astype(BF16), wd_ref[...])

    @pl.when(f == pl.num_programs(1) - 1)
    def _():
        y = _layer_norm(DN_ALPHA * xf_ref[...] + acc_ref[...], g_ref[...], b_ref[...])
        of_ref[...] = y
        ob_ref[...] = y.astype(ob_ref.dtype)


def _mlp(xb, xf, w_up, w_down, g, b, layer, tm, tf):
    m = xb.shape[0]
    tm = min(tm, m)
    vec = _layer_spec(layer, (1, D_MODEL), lambda i, f: (0, 0))
    return pl.pallas_call(
        _mlp_kernel,
        out_shape=[jax.ShapeDtypeStruct((m, D_MODEL), F32), jax.ShapeDtypeStruct((m, D_MODEL), BF16)],
        grid=(m // tm, D_FF // tf),
        in_specs=[pl.BlockSpec((tm, D_MODEL), lambda i, f: (i, 0)),
                  pl.BlockSpec((tm, D_MODEL), lambda i, f: (i, 0)),
                  _layer_spec(layer, (D_MODEL, tf), lambda i, f: (0, f)),
                  _layer_spec(layer, (tf, D_MODEL), lambda i, f: (f, 0)), vec, vec],
        out_specs=[pl.BlockSpec((tm, D_MODEL), lambda i, f: (i, 0)),
                   pl.BlockSpec((tm, D_MODEL), lambda i, f: (i, 0))],
        scratch_shapes=[pltpu.VMEM((tm, D_MODEL), F32)],
        compiler_params=_params("parallel", "arbitrary"),
        name="mlp",
    )(xb, xf, w_up, w_down, g, b)


def _layer_tail(xf, a, s, rest, xattn_fn, w, layer, tiles):
    mix = _merge(a, s, rest, w["b_gate"], w["w_pa"], w["w_pb"], layer, tiles["merge_tm"], 1024)
    x1f, x1b = _proj_ln(mix, w["w_o"], xf, w["ln1_g"], w["ln1_b"], layer, tiles["ln_tm"])
    o = xattn_fn(x1b)
    x2f, x2b = _proj_ln(o, w["w_xo"], x1f, w["ln2_g"], w["ln2_b"], layer, tiles["ln_tm"])
    return _mlp(x2b, x2f, w["w_up"], w["w_down"], w["ln3_g"], w["ln3_b"], layer, tiles["mlp_tm"], 512)


_PROMPT_TILES = dict(merge_tm=512, ln_tm=256, mlp_tm=512)
_SAMPLE_TILES = dict(merge_tm=32, ln_tm=32, mlp_tm=32)


def kernel(x_prompt, x_sample, mem_prompt, cache_k, cache_v, cache_mem_k, cache_mem_v, page_table,
           w_in, b_gate, sgu_ln_g, sgu_ln_b, w_s, b_s, w_pa, w_pb, w_o, ln1_g, ln1_b,
           w_xq, w_xk, w_xv, w_xo, ln2_g, ln2_b, w_up, w_down, ln3_g, ln3_b):
    batch, seq, _ = x_prompt.shape
    n_samples, dec_seq, _ = x_sample.shape
    assert dec_seq == 1 and seq % MOBA_BLOCK == 0
    assert N_KV_HEADS & (N_KV_HEADS - 1) == 0 and X_HEADS & (X_HEADS - 1) == 0 and X_HEADS <= SUBLANES
    n_pool = cache_k.shape[1]
    n_pages = page_table.shape[1]
    n_blocks = seq // MOBA_BLOCK
    assert n_pages % PAGES_PER_BLOCK == 0
    mp, ms = batch * seq, n_samples

    pt_flat = page_table.reshape(-1).astype(jnp.int32)
    cache_k_rows = cache_k.reshape(DEPTH * n_pool, PAGE_ROWS, HEAD_DIM)
    cache_mk_rows = cache_mem_k.reshape(DEPTH * n_samples, N_MEM * X_HEADS, X_HEAD_DIM)
    cache_mv_rows = cache_mem_v.reshape(DEPTH * n_samples, N_MEM * X_HEADS, X_HEAD_DIM)
    mem_b = mem_prompt.reshape(batch * N_MEM, D_MODEL).astype(BF16)

    xpf = x_prompt.reshape(mp, D_MODEL)
    xsf = x_sample.reshape(ms, D_MODEL)
    xpb, xsb = xpf.astype(BF16), xsf.astype(BF16)

    w = dict(
        b_gate=b_gate, w_pa=_to_bf16(w_pa), w_pb=_to_bf16(w_pb), w_o=_to_bf16(w_o), w_xo=_to_bf16(w_xo),
        w_up=_to_bf16(w_up), w_down=_to_bf16(w_down),
        ln1_g=ln1_g[:, None], ln1_b=ln1_b[:, None], ln2_g=ln2_g[:, None], ln2_b=ln2_b[:, None],
        ln3_g=ln3_g[:, None], ln3_b=ln3_b[:, None])
    w_in_b, w_xq_b, w_xk_b, w_xv_b = _to_bf16(w_in), _to_bf16(w_xq), _to_bf16(w_xk), _to_bf16(w_xv)

    kp_l, vp_l, mkp_l, mvp_l, ks_l, vs_l, vns_l = [], [], [], [], [], [], []
    for l in range(DEPTH):
        sgu_g, sgu_b = sgu_ln_g[l][None], sgu_ln_b[l][None]

        mk_f, mk_b = _matmul(mem_b, w_xk_b, l, 0, X_W, (F32, BF16), 512, X_W)
        mv_f, mv_b = _matmul(mem_b, w_xv_b, l, 0, X_W, (F32, BF16), 512, X_W)
        qkv_f, qkv_b = _matmul(xpb, w_in_b, l, 0, QKV_W, (F32, BF16), 1024, 1024)
        (rest,) = _matmul(xpb, w_in_b, l, QKV_W, REST_W, (F32,), 1024, 1024)
        kmean = _prompt_kmeans(qkv_f).reshape(batch, n_blocks, KV_W)
        vt = qkv_b[:, Q_W + KV_W:].reshape(batch, n_blocks, MOBA_BLOCK, N_KV_HEADS, HEAD_DIM)
        vt = jnp.transpose(vt, (0, 3, 1, 4, 2))
        a = _moba_prompt(qkv_b, vt, kmean, batch, seq)
        s = _sgu_chunks(rest, sgu_g, sgu_b, w_s[l], b_s[l].T, 256)

        def prompt_xattn(x1b, l=l, mk_b=mk_b, mv_b=mv_b):
            (qx,) = _matmul(x1b, w_xq_b, l, 0, X_W, (BF16,), 1024, X_W)
            o = _xattn(qx.reshape(batch, seq, X_W), mk_b.reshape(batch, N_MEM, X_W),
                       mv_b.reshape(batch, N_MEM, X_W), 512)
            return o.reshape(mp, X_W)

        xpf, xpb = _layer_tail(xpf, a, s, rest, prompt_xattn, w, l, _PROMPT_TILES)
        kp_l.append(qkv_f[:, Q_W:Q_W + KV_W].reshape(batch, seq // PAGE_SIZE, PAGE_SIZE, N_KV_HEADS, HEAD_DIM))
        vp_l.append(qkv_f[:, Q_W + KV_W:].reshape(batch, seq // PAGE_SIZE, PAGE_SIZE, N_KV_HEADS, HEAD_DIM))
        mkp_l.append(mk_f.reshape(batch, N_MEM, X_HEADS, X_HEAD_DIM))
        mvp_l.append(mv_f.reshape(batch, N_MEM, X_HEADS, X_HEAD_DIM))

        (qkv_s,) = _matmul(xsb, w_in_b, l, 0, QKV_W, (F32,), 32, 1024)
        (rest_s,) = _matmul(xsb, w_in_b, l, QKV_W, REST_W, (F32,), 32, 1024)
        qkv_s3 = qkv_s.reshape(ms, 1, QKV_W)
        a_s = _cached_moba(qkv_s3, cache_k_rows, cache_v, pt_flat, l, n_pool, n_pages)
        a_s = a_s.reshape(ms, Q_W).astype(BF16)
        w0 = jnp.repeat(w_s[l][:, 0, 0], SGU_GROUP_DIM)[None]
        b0 = jnp.repeat(b_s[l][:, 0], SGU_GROUP_DIM)[None]
        s_s, vn_s = _sgu_first_rows(rest_s, sgu_g, sgu_b, w0, b0)

        def sample_xattn(x1b, l=l):
            (qx,) = _matmul(x1b, w_xq_b, l, 0, X_W, (F32,), 32, X_W)
            o = _xattn_single(qx.reshape(ms, 1, X_W), cache_mk_rows, cache_mv_rows, l * n_samples)
            return o.reshape(ms, X_W).astype(BF16)

        xsf, xsb = _layer_tail(xsf, a_s, s_s, rest_s, sample_xattn, w, l, _SAMPLE_TILES)
        ks_l.append(qkv_s[:, Q_W:Q_W + KV_W].reshape(ms, 1, N_KV_HEADS, HEAD_DIM))
        vs_l.append(qkv_s[:, Q_W + KV_W:].reshape(ms, 1, N_KV_HEADS, HEAD_DIM))
        vns_l.append(vn_s.reshape(ms, 1, SGU_WIDTH))

    return (xpf.reshape(batch, seq, D_MODEL), xsf.reshape(ms, 1, D_MODEL),
            jnp.stack(kp_l), jnp.stack(vp_l), jnp.stack(mkp_l), jnp.stack(mvp_l),
            jnp.stack(ks_l), jnp.stack(vs_l), jnp.stack(vns_l))
```

```python
import functools

import jax
import jax.numpy as jnp
import numpy as np
from jax import lax
from jax.experimental import pallas as pl
from jax.experimental.pallas import tpu as pltpu

D_MODEL = 2048
DEPTH = 2
PAGE_SIZE = 128
N_HEADS = 8
N_KV_HEADS = 4
HEAD_DIM = 128
MOBA_BLOCK = 256
MOBA_TOPK = 3
SGU_WIDTH = 1024
SGU_GROUPS = 8
SGU_GROUP_DIM = SGU_WIDTH // SGU_GROUPS
SGU_CHUNK = 128
N_MEM = 256
X_HEADS = 4
X_HEAD_DIM = 128
D_FF = 4 * D_MODEL
DN_ALPHA = (2 * DEPTH) ** 0.25
LN_EPS = 1e-5
Q_W = N_HEADS * HEAD_DIM
KV_W = N_KV_HEADS * HEAD_DIM
X_W = X_HEADS * X_HEAD_DIM
QKV_W = Q_W + 2 * KV_W
REST_W = 2 * SGU_WIDTH + 2 * D_MODEL
PAGES_PER_BLOCK = MOBA_BLOCK // PAGE_SIZE
PAGE_ROWS = PAGE_SIZE * N_KV_HEADS
MASKED = -1e30

LANES = 128
SUBLANES = 8
VMEM_LIMIT = 56 * 1024 * 1024

BF16 = jnp.bfloat16
F32 = jnp.float32
_NT = (((1,), (1,)), ((), ()))


def _params(*sem):
    return pltpu.CompilerParams(dimension_semantics=sem, vmem_limit_bytes=VMEM_LIMIT)


def _dot(a, b):
    return jnp.dot(a, b, preferred_element_type=F32)


def _dot_nt(a, b):
    return lax.dot_general(a, b, _NT, preferred_element_type=F32)


def _gelu(x):
    c = np.float32(np.sqrt(2 / np.pi))
    return x * (0.5 * (1.0 + jnp.tanh(c * (x + 0.044715 * (x * x * x)))))


def _layer_norm(z, g, b):
    mu = jnp.mean(z, axis=-1, keepdims=True)
    d = z - mu
    var = jnp.mean(d * d, axis=-1, keepdims=True)
    return d * lax.rsqrt(var + LN_EPS) * g + b


def _top_blocks(gate, valid_f, idx_f, axis):
    sel = jnp.zeros(gate.shape, F32)
    for _ in range(MOBA_TOPK):
        m = jnp.max(gate, axis=axis, keepdims=True)
        first = jnp.min(jnp.where(gate == m, idx_f, float(gate.shape[axis])), axis=axis, keepdims=True)
        pick = idx_f == first
        sel = jnp.where(pick, valid_f, sel)
        gate = jnp.where(pick, -jnp.inf, gate)
    return sel


def _rows_to_sublanes(ref, n_rows, width):
    sub = lax.broadcasted_iota(jnp.int32, (SUBLANES, width), 0)
    out = jnp.zeros((SUBLANES, width), F32)
    for r in range(n_rows):
        out = jnp.where(sub == r, ref[0, :, r * width:(r + 1) * width].astype(F32), out)
    return out


def _mm_kernel(x_ref, w_ref, *out_refs):
    acc = _dot(x_ref[...], w_ref[...])
    for o in out_refs:
        o[...] = acc.astype(o.dtype)


def _cast_kernel(x_ref, o_ref):
    o_ref[...] = x_ref[...].astype(o_ref.dtype)


_CAST_BLOCK_BYTES = 4 * 1024 * 1024


def _to_bf16(w):
    d, k, n = w.shape
    rows = d * k
    tr = min(rows, max(SUBLANES, _CAST_BLOCK_BYTES // (4 * n)))
    assert rows % tr == 0
    out = pl.pallas_call(
        _cast_kernel,
        out_shape=jax.ShapeDtypeStruct((rows, n), BF16),
        grid=(rows // tr,),
        in_specs=[pl.BlockSpec((tr, n), lambda i: (i, 0))],
        out_specs=pl.BlockSpec((tr, n), lambda i: (i, 0)),
        compiler_params=_params("parallel"),
        name="to_bf16",
    )(w.reshape(rows, n))
    return out.reshape(d, k, n)


def _layer_spec(layer, block, index):
    return pl.BlockSpec((None,) + block, lambda *g: (layer,) + index(*g))


def _matmul(x, w, layer, col_off, ncols, out_dtypes, tm, tn):
    m, k = x.shape
    tm = min(tm, m)
    tn = min(tn, ncols)
    assert m % tm == 0 and ncols % tn == 0 and col_off % tn == 0
    joff = col_off // tn
    outs = pl.pallas_call(
        _mm_kernel,
        out_shape=[jax.ShapeDtypeStruct((m, ncols), dt) for dt in out_dtypes],
        grid=(ncols // tn, m // tm),
        in_specs=[pl.BlockSpec((tm, k), lambda j, i: (i, 0)),
                  _layer_spec(layer, (k, tn), lambda j, i: (0, j + joff))],
        out_specs=[pl.BlockSpec((tm, tn), lambda j, i: (i, j)) for _ in out_dtypes],
        compiler_params=_params("parallel", "parallel"),
        name="matmul",
    )(x, w)
    return outs


def _kv_proj_kernel(x_ref, w_ref, rows_ref, ob_ref, km_ref):
    acc = _dot(x_ref[...], w_ref[...])
    tm = acc.shape[0]
    for h in range(N_KV_HEADS):
        rows_ref[pl.ds(h, tm, stride=N_KV_HEADS), :] = acc[:, h * HEAD_DIM:(h + 1) * HEAD_DIM]
    ob_ref[...] = acc.astype(ob_ref.dtype)
    for c in range(tm // MOBA_BLOCK):
        blk = acc[c * MOBA_BLOCK:(c + 1) * MOBA_BLOCK]
        km_ref[c] = jnp.sum(blk, axis=0, keepdims=True) * (1.0 / MOBA_BLOCK)


def _kv_proj(x, w, layer, col_off, tm):
    m, k = x.shape
    assert m % tm == 0 and tm % MOBA_BLOCK == 0 and col_off % KV_W == 0
    return pl.pallas_call(
        _kv_proj_kernel,
        out_shape=[jax.ShapeDtypeStruct((m * N_KV_HEADS, HEAD_DIM), F32),
                   jax.ShapeDtypeStruct((m, KV_W), BF16),
                   jax.ShapeDtypeStruct((m // MOBA_BLOCK, 1, KV_W), F32)],
        grid=(m // tm,),
        in_specs=[pl.BlockSpec((tm, k), lambda i: (i, 0)),
                  _layer_spec(layer, (k, KV_W), lambda i: (0, col_off // KV_W))],
        out_specs=[pl.BlockSpec((tm * N_KV_HEADS, HEAD_DIM), lambda i: (i, 0)),
                   pl.BlockSpec((tm, KV_W), lambda i: (i, 0)),
                   pl.BlockSpec((tm // MOBA_BLOCK, 1, KV_W), lambda i: (i, 0, 0))],
        compiler_params=_params("parallel"),
        name="kv_proj",
    )(x, w)


_MOBA_KV_PER_STEP = 4


def _moba_prompt_kernel(q_ref, k_ref, vt_ref, km_ref, o_ref, sel_ref, m_ref, l_ref, acc_ref):
    i = pl.program_id(2)
    rep = N_HEADS // N_KV_HEADS
    heads = _MOBA_KV_PER_STEP * rep
    blk = MOBA_BLOCK
    scale_log2e = np.float32(HEAD_DIM ** -0.5 * np.log2(np.e))

    def cols(c):
        return slice(c * HEAD_DIM, (c + 1) * HEAD_DIM)

    qs = [q_ref[:, cols(c)] for c in range(heads)]
    blk_id = lax.broadcasted_iota(jnp.int32, (km_ref.shape[0], blk), 0)
    valid = blk_id < i
    for c in range(heads):
        km = km_ref[:, cols(c // rep)].astype(BF16)
        gate = jnp.where(valid, _dot_nt(km, qs[c]), MASKED)
        sel_ref[c] = _top_blocks(gate, valid.astype(F32), blk_id.astype(F32), 0)

    def attend(j, masks, first):
        kjs = [k_ref[pl.ds(pl.multiple_of(j * blk, blk), blk), cols(g)] for g in range(_MOBA_KV_PER_STEP)]
        scores = [_dot_nt(kjs[c // rep], qs[c]) for c in range(heads)]
        ps, m_news, l_blks = [], [], []
        for c in range(heads):
            s = jnp.where(masks[c], scores[c] * scale_log2e, MASKED)
            m_blk = jnp.max(s, axis=0, keepdims=True)
            m_new = m_blk if first else jnp.maximum(m_ref[c], m_blk)
            p = jnp.exp2(s - m_new)
            l_blks.append(jnp.sum(p, axis=0, keepdims=True))
            ps.append(p.astype(BF16))
            m_news.append(m_new)
        pvs = [_dot(vt_ref[c // rep, j], ps[c]) for c in range(heads)]
        for c in range(heads):
            if first:
                l_ref[c] = l_blks[c]
                acc_ref[c] = pvs[c]
            else:
                a = jnp.exp2(m_ref[c] - m_news[c])
                l_ref[c] = a * l_ref[c] + l_blks[c]
                acc_ref[c] = a * acc_ref[c] + pvs[c]
            m_ref[c] = m_news[c]

    key = lax.broadcasted_iota(jnp.int32, (blk, blk), 0)
    qry = lax.broadcasted_iota(jnp.int32, (blk, blk), 1)
    attend(i, [key <= qry] * heads, True)

    def body(j, carry):
        attend(j, [sel_ref[c, pl.ds(j, 1), :] > 0.5 for c in range(heads)], False)
        return carry

    lax.fori_loop(0, i, body, 0)
    for c in range(heads):
        o_ref[:, cols(c)] = (acc_ref[c] / l_ref[c]).T.astype(o_ref.dtype)


def _moba_prompt(q_b, k_b, vt, kmean, batch, seq):
    nq = seq // MOBA_BLOCK
    g = _MOBA_KV_PER_STEP
    heads = g * (N_HEADS // N_KV_HEADS)
    assert N_KV_HEADS % g == 0
    return pl.pallas_call(
        _moba_prompt_kernel,
        out_shape=jax.ShapeDtypeStruct((batch * seq, Q_W), BF16),
        grid=(batch, N_KV_HEADS // g, nq),
        in_specs=[
            pl.BlockSpec((MOBA_BLOCK, heads * HEAD_DIM), lambda b, gg, i: (b * nq + i, gg)),
            pl.BlockSpec((seq, g * HEAD_DIM), lambda b, gg, i: (b, gg)),
            pl.BlockSpec((None, g, nq, HEAD_DIM, MOBA_BLOCK), lambda b, gg, i: (b, gg, 0, 0, 0)),
            pl.BlockSpec((None, nq, g * HEAD_DIM), lambda b, gg, i: (b, 0, gg)),
        ],
        out_specs=pl.BlockSpec((MOBA_BLOCK, heads * HEAD_DIM), lambda b, gg, i: (b * nq + i, gg)),
        scratch_shapes=[pltpu.VMEM((heads, nq, MOBA_BLOCK), F32),
                        pltpu.VMEM((heads, 1, MOBA_BLOCK), F32),
                        pltpu.VMEM((heads, 1, MOBA_BLOCK), F32),
                        pltpu.VMEM((heads, HEAD_DIM, MOBA_BLOCK), F32)],
        compiler_params=_params("parallel", "parallel", "arbitrary"),
        name="moba_prompt",
    )(q_b, k_b, vt, kmean)


def _cached_moba_kernel(pt_ref, qkv_ref, ck_hbm, cv_hbm, o_ref, kbuf, vbuf, km_ref, ksem, vsem,
                        *, layer, n_pool, n_pages):
    b = pl.program_id(0)
    n = pl.num_programs(0)
    slot = b % 2
    rep = N_HEADS // N_KV_HEADS
    n_blocks = n_pages // PAGES_PER_BLOCK
    scale = HEAD_DIM ** -0.5
    base = layer * n_pool

    def k_copy(sample, page_slot, sl):
        page = base + pt_ref[sample * n_pages + page_slot]
        return pltpu.make_async_copy(ck_hbm.at[page], kbuf.at[sl, page_slot], ksem.at[sl])

    def start_keys(sample, sl):
        for p in range(n_pages):
            k_copy(sample, p, sl).start()

    @pl.when(b == 0)
    def _():
        km_ref[...] = jnp.zeros_like(km_ref)
        start_keys(0, 0)

    @pl.when(b + 1 < n)
    def _():
        start_keys(b + 1, 1 - slot)

    for p in range(n_pages):
        k_copy(b, p, slot).wait()

    for blk in range(n_blocks):
        tot = jnp.zeros((SUBLANES, HEAD_DIM), F32)
        for r in range(PAGES_PER_BLOCK):
            page = kbuf[slot, PAGES_PER_BLOCK * blk + r]
            tot = tot + jnp.sum(page.reshape(PAGE_ROWS // SUBLANES, SUBLANES, HEAD_DIM), axis=0)
        km_ref[blk * SUBLANES:(blk + 1) * SUBLANES, :] = tot + pltpu.roll(tot, N_KV_HEADS, axis=0)

    q8 = _rows_to_sublanes(qkv_ref, N_HEADS, HEAD_DIM)
    q8b = q8.astype(BF16)
    hrow = lax.broadcasted_iota(jnp.int32, (N_HEADS, LANES), 0)
    lane = lax.broadcasted_iota(jnp.int32, (N_HEADS, LANES), 1)
    gate = jnp.zeros((N_HEADS, LANES), F32)
    for kvh in range(N_KV_HEADS):
        km = km_ref[pl.ds(kvh, LANES, stride=SUBLANES), :] * (1.0 / MOBA_BLOCK)
        gate = jnp.where(hrow >= kvh * rep, _dot_nt(q8b, km.astype(BF16)), gate)
    gate = jnp.where(lane < n_blocks, gate, MASKED)
    lane_f = lane.astype(F32)
    picks = []
    for _ in range(MOBA_TOPK):
        m = jnp.max(gate, axis=-1, keepdims=True)
        first = jnp.min(jnp.where(gate == m, lane_f, float(LANES)), axis=-1, keepdims=True)
        picks.append(first.astype(jnp.int32))
        gate = jnp.where(lane_f == first, -jnp.inf, gate)
    blocks = [[picks[t][h, 0] for t in range(MOBA_TOPK)] for h in range(N_HEADS)]

    def v_copy(h, t, r):
        page = pt_ref[b * n_pages + blocks[h][t] * PAGES_PER_BLOCK + r]
        return pltpu.make_async_copy(cv_hbm.at[layer, page, :, h // rep, :],
                                     vbuf.at[(h * MOBA_TOPK + t) * PAGES_PER_BLOCK + r], vsem)

    sel = [(h, t, r) for h in range(N_HEADS) for t in range(MOBA_TOPK) for r in range(PAGES_PER_BLOCK)]
    for h, t, r in sel:
        v_copy(h, t, r).start()

    k_new = qkv_ref[0, :, Q_W:Q_W + KV_W]
    v_new = qkv_ref[0, :, Q_W + KV_W:]
    scores = {}
    for h, t, r in sel:
        kvh = h // rep
        kp = kbuf[slot, blocks[h][t] * PAGES_PER_BLOCK + r, pl.ds(kvh, PAGE_SIZE, stride=N_KV_HEADS), :]
        qh = jnp.broadcast_to(q8b[h:h + 1], (SUBLANES, HEAD_DIM))
        scores[h, t, r] = _dot_nt(qh, kp.astype(BF16))[0:1] * scale
    probs, p_new = {}, []
    for h in range(N_HEADS):
        kvh = h // rep
        kn = k_new[:, kvh * HEAD_DIM:(kvh + 1) * HEAD_DIM].astype(BF16).astype(F32)
        s_new = jnp.sum(q8b[h:h + 1].astype(F32) * kn, axis=-1, keepdims=True) * scale
        mine = [scores[h, t, r] for t in range(MOBA_TOPK) for r in range(PAGES_PER_BLOCK)]
        m = s_new
        for s in mine:
            m = jnp.maximum(m, jnp.max(s, axis=-1, keepdims=True))
        es = [jnp.exp(s - m) for s in mine]
        e_new = jnp.exp(s_new - m)
        denom = e_new
        for e in es:
            denom = denom + jnp.sum(e, axis=-1, keepdims=True)
        p_new.append((e_new / denom).astype(BF16).astype(F32))
        for idx, (t, r) in enumerate((t, r) for t in range(MOBA_TOPK) for r in range(PAGES_PER_BLOCK)):
            probs[h, t, r] = (es[idx] / denom).astype(BF16)

    for h, t, r in sel:
        v_copy(h, t, r).wait()

    for h in range(N_HEADS):
        kvh = h // rep
        out = p_new[h] * v_new[:, kvh * HEAD_DIM:(kvh + 1) * HEAD_DIM].astype(BF16).astype(F32)
        for t in range(MOBA_TOPK):
            for r in range(PAGES_PER_BLOCK):
                p8 = jnp.broadcast_to(probs[h, t, r], (SUBLANES, PAGE_SIZE))
                vp = vbuf[(h * MOBA_TOPK + t) * PAGES_PER_BLOCK + r]
                out = out + _dot(p8, vp.astype(BF16))[0:1]
        o_ref[0, :, h * HEAD_DIM:(h + 1) * HEAD_DIM] = out


def _cached_moba(qkv_f3, cache_k_rows, cache_v, pt_flat, layer, n_pool, n_pages):
    n = qkv_f3.shape[0]
    assert SUBLANES == 2 * N_KV_HEADS and n_pages // PAGES_PER_BLOCK <= LANES
    return pl.pallas_call(
        functools.partial(_cached_moba_kernel, layer=layer, n_pool=n_pool, n_pages=n_pages),
        out_shape=jax.ShapeDtypeStruct((n, 1, Q_W), F32),
        grid_spec=pltpu.PrefetchScalarGridSpec(
            num_scalar_prefetch=1,
            grid=(n,),
            in_specs=[pl.BlockSpec((1, 1, QKV_W), lambda b, pt: (b, 0, 0)),
                      pl.BlockSpec(memory_space=pl.ANY),
                      pl.BlockSpec(memory_space=pl.ANY)],
            out_specs=pl.BlockSpec((1, 1, Q_W), lambda b, pt: (b, 0, 0)),
            scratch_shapes=[pltpu.VMEM((2, n_pages, PAGE_ROWS, HEAD_DIM), F32),
                            pltpu.VMEM((N_HEADS * MOBA_TOPK * PAGES_PER_BLOCK, PAGE_SIZE, HEAD_DIM), F32),
                            pltpu.VMEM((LANES * SUBLANES, HEAD_DIM), F32),
                            pltpu.SemaphoreType.DMA((2,)),
                            pltpu.SemaphoreType.DMA(())],
        ),
        compiler_params=_params("arbitrary"),
        name="cached_moba",
    )(pt_flat, qkv_f3, cache_k_rows, cache_v)


def _sgu_chunk_kernel(zu_ref, zv_ref, g_ref, b_ref, ws_ref, bs_ref, s_ref):
    vn = _layer_norm(_gelu(zv_ref[...]), g_ref[...], b_ref[...])
    gu = _gelu(zu_ref[...])
    t = SGU_CHUNK
    row = lax.broadcasted_iota(jnp.int32, (t, t), 0)
    col = lax.broadcasted_iota(jnp.int32, (t, t), 1)
    for g in range(SGU_GROUPS):
        cs = slice(g * SGU_GROUP_DIM, (g + 1) * SGU_GROUP_DIM)
        ws = jnp.where(col <= row, ws_ref[g], 0.0).astype(BF16)
        bias = bs_ref[:, g:g + 1]
        for c in range(zu_ref.shape[0] // t):
            rs = slice(c * t, (c + 1) * t)
            mixed = _dot(ws, vn[rs, cs].astype(BF16)) + bias
            s_ref[rs, cs] = (gu[rs, cs] * mixed).astype(s_ref.dtype)


def _prompt_mid_kernel(a_ref, zu_ref, zv_ref, ga_ref, gb_ref, x_ref, mk_ref, mv_ref,
                       sg_ref, sb_ref, ws_ref, bs_ref, bg_ref, wpa_ref, wpb_ref, wo_ref, g1_ref, b1_ref,
                       wxq_ref, wxo_ref, g2_ref, b2_ref, of_ref, ob_ref, s_scr, o_scr):
    _sgu_chunk_kernel(zu_ref, zv_ref, sg_ref, sb_ref, ws_ref, bs_ref, s_scr)
    g_a = jax.nn.sigmoid(ga_ref[...] + bg_ref[0:1, :])
    g_b = jax.nn.sigmoid(gb_ref[...] + bg_ref[1:2, :])
    mix = g_a * _dot(a_ref[...], wpa_ref[...]) + g_b * _dot(s_scr[...], wpb_ref[...])
    x1 = _layer_norm(DN_ALPHA * x_ref[...] + _dot(mix.astype(BF16), wo_ref[...]), g1_ref[...], b1_ref[...])
    qx = _dot(x1.astype(BF16), wxq_ref[...]).astype(BF16)
    scale = X_HEAD_DIM ** -0.5
    for h in range(X_HEADS):
        cs = slice(h * X_HEAD_DIM, (h + 1) * X_HEAD_DIM)
        s = _dot_nt(qx[:, cs], mk_ref[:, cs]) * scale
        e = jnp.exp(s - jnp.max(s, axis=-1, keepdims=True))
        p = (e / jnp.sum(e, axis=-1, keepdims=True)).astype(BF16)
        o_scr[:, cs] = _dot(p, mv_ref[:, cs]).astype(o_scr.dtype)
    y = _layer_norm(DN_ALPHA * x1 + _dot(o_scr[...], wxo_ref[...]), g2_ref[...], b2_ref[...])
    of_ref[...] = y
    ob_ref[...] = y.astype(ob_ref.dtype)


def _prompt_mid(a, rest, x, mk, mv, sgu_g, sgu_b, w_s, b_s_t, w, w_xq, layer, seq, tm):
    m = a.shape[0]
    assert m % tm == 0 and seq % tm == 0 and tm % SGU_CHUNK == 0
    tiles_per_batch = seq // tm

    def const(block):
        return pl.BlockSpec((None,) + block, lambda i: (layer,) + (0,) * len(block), pipeline_mode=pl.Buffered(1))

    mem = pl.BlockSpec((None, N_MEM, X_W), lambda i: (i // tiles_per_batch, 0, 0))
    vec_d = const((1, D_MODEL))
    return pl.pallas_call(
        _prompt_mid_kernel,
        out_shape=[jax.ShapeDtypeStruct((m, D_MODEL), F32), jax.ShapeDtypeStruct((m, D_MODEL), BF16)],
        grid=(m // tm,),
        in_specs=[pl.BlockSpec((tm, Q_W), lambda i: (i, 0)),
                  pl.BlockSpec((tm, SGU_WIDTH), lambda i: (i, 0)),
                  pl.BlockSpec((tm, SGU_WIDTH), lambda i: (i, 1)),
                  pl.BlockSpec((tm, D_MODEL), lambda i: (i, 2 * SGU_WIDTH // D_MODEL)),
                  pl.BlockSpec((tm, D_MODEL), lambda i: (i, 2 * SGU_WIDTH // D_MODEL + 1)),
                  pl.BlockSpec((tm, D_MODEL), lambda i: (i, 0)),
                  mem, mem,
                  const((1, SGU_WIDTH)), const((1, SGU_WIDTH)),
                  const((SGU_GROUPS, SGU_CHUNK, SGU_CHUNK)), const((SGU_CHUNK, SGU_GROUPS)),
                  const((2, D_MODEL)), const((Q_W, D_MODEL)), const((SGU_WIDTH, D_MODEL)),
                  const((D_MODEL, D_MODEL)), vec_d, vec_d,
                  const((D_MODEL, X_W)), const((X_W, D_MODEL)), vec_d, vec_d],
        out_specs=[pl.BlockSpec((tm, D_MODEL), lambda i: (i, 0)), pl.BlockSpec((tm, D_MODEL), lambda i: (i, 0))],
        scratch_shapes=[pltpu.VMEM((tm, SGU_WIDTH), BF16), pltpu.VMEM((tm, X_W), BF16)],
        compiler_params=_params("parallel"),
        name="prompt_mid",
    )(a, rest, rest, rest, rest, x, mk, mv, sgu_g, sgu_b, w_s, b_s_t,
      w["b_gate"], w["w_pa"], w["w_pb"], w["w_o"], w["ln1_g"], w["ln1_b"],
      w_xq, w["w_xo"], w["ln2_g"], w["ln2_b"])


def _sgu_first_row_kernel(zu_ref, zv_ref, g_ref, b_ref, w0_ref, b0_ref, s_ref, vn_ref):
    vn = _layer_norm(_gelu(zv_ref[...]), g_ref[...], b_ref[...])
    vn_ref[...] = vn
    s_ref[...] = (_gelu(zu_ref[...]) * (vn * w0_ref[...] + b0_ref[...])).astype(s_ref.dtype)


def _sgu_first_rows(rest, ln_g, ln_b, w0, b0):
    m = rest.shape[0]
    vec = pl.BlockSpec((1, SGU_WIDTH), lambda i: (0, 0))
    return pl.pallas_call(
        _sgu_first_row_kernel,
        out_shape=[jax.ShapeDtypeStruct((m, SGU_WIDTH), BF16), jax.ShapeDtypeStruct((m, SGU_WIDTH), F32)],
        grid=(1,),
        in_specs=[pl.BlockSpec((m, SGU_WIDTH), lambda i: (0, 0)),
                  pl.BlockSpec((m, SGU_WIDTH), lambda i: (0, 1)), vec, vec, vec, vec],
        out_specs=[pl.BlockSpec((m, SGU_WIDTH), lambda i: (0, 0)), pl.BlockSpec((m, SGU_WIDTH), lambda i: (0, 0))],
        compiler_params=_params("arbitrary"),
        name="sgu_first_rows",
    )(rest, rest, ln_g, ln_b, w0, b0)


def _merge_kernel(a_ref, s_ref, ga_ref, gb_ref, bg_ref, wa_ref, wb_ref, o_ref):
    g_a = jax.nn.sigmoid(ga_ref[...] + bg_ref[0:1, :])
    g_b = jax.nn.sigmoid(gb_ref[...] + bg_ref[1:2, :])
    mix = g_a * _dot(a_ref[...], wa_ref[...]) + g_b * _dot(s_ref[...], wb_ref[...])
    o_ref[...] = mix.astype(o_ref.dtype)


def _merge(a, s, rest, b_gate, w_pa, w_pb, layer, tm, tn):
    m = a.shape[0]
    tm = min(tm, m)
    ga0 = 2 * SGU_WIDTH // tn
    gb0 = (2 * SGU_WIDTH + D_MODEL) // tn
    return pl.pallas_call(
        _merge_kernel,
        out_shape=jax.ShapeDtypeStruct((m, D_MODEL), BF16),
        grid=(D_MODEL // tn, m // tm),
        in_specs=[pl.BlockSpec((tm, Q_W), lambda j, i: (i, 0)),
                  pl.BlockSpec((tm, SGU_WIDTH), lambda j, i: (i, 0)),
                  pl.BlockSpec((tm, tn), lambda j, i: (i, ga0 + j)),
                  pl.BlockSpec((tm, tn), lambda j, i: (i, gb0 + j)),
                  _layer_spec(layer, (2, tn), lambda j, i: (0, j)),
                  _layer_spec(layer, (Q_W, tn), lambda j, i: (0, j)),
                  _layer_spec(layer, (SGU_WIDTH, tn), lambda j, i: (0, j))],
        out_specs=pl.BlockSpec((tm, tn), lambda j, i: (i, j)),
        compiler_params=_params("parallel", "parallel"),
        name="merge",
    )(a, s, rest, rest, b_gate, w_pa, w_pb)


def _proj_ln_kernel(a_ref, w_ref, x_ref, g_ref, b_ref, of_ref, ob_ref):
    z = DN_ALPHA * x_ref[...] + _dot(a_ref[...], w_ref[...])
    y = _layer_norm(z, g_ref[...], b_ref[...])
    of_ref[...] = y
    ob_ref[...] = y.astype(ob_ref.dtype)


def _proj_ln(a, w, x, g, b, layer, tm):
    m, k = a.shape
    tm = min(tm, m)
    vec = _layer_spec(layer, (1, D_MODEL), lambda i: (0, 0))
    return pl.pallas_call(
        _proj_ln_kernel,
        out_shape=[jax.ShapeDtypeStruct((m, D_MODEL), F32), jax.ShapeDtypeStruct((m, D_MODEL), BF16)],
        grid=(m // tm,),
        in_specs=[pl.BlockSpec((tm, k), lambda i: (i, 0)),
                  _layer_spec(layer, (k, D_MODEL), lambda i: (0, 0)),
                  pl.BlockSpec((tm, D_MODEL), lambda i: (i, 0)), vec, vec],
        out_specs=[pl.BlockSpec((tm, D_MODEL), lambda i: (i, 0)), pl.BlockSpec((tm, D_MODEL), lambda i: (i, 0))],
        compiler_params=_params("parallel"),
        name="proj_ln",
    )(a, w, x, g, b)


def _xattn_single_kernel(q_ref, mk_ref, mv_ref, o_ref):
    scale = X_HEAD_DIM ** -0.5
    rows = N_MEM * X_HEADS
    q8 = _rows_to_sublanes(q_ref, X_HEADS, X_HEAD_DIM).astype(BF16)
    head = lax.broadcasted_iota(jnp.int32, (SUBLANES, rows), 0)
    row_head = jnp.bitwise_and(lax.broadcasted_iota(jnp.int32, (SUBLANES, rows), 1), X_HEADS - 1)
    s = jnp.where(row_head == head, _dot_nt(q8, mk_ref[0].astype(BF16)) * scale, MASKED)
    e = jnp.exp(s - jnp.max(s, axis=-1, keepdims=True))
    p = (e / jnp.sum(e, axis=-1, keepdims=True)).astype(BF16)
    o = _dot(p, mv_ref[0].astype(BF16))
    for h in range(X_HEADS):
        o_ref[0, :, h * X_HEAD_DIM:(h + 1) * X_HEAD_DIM] = o[h:h + 1].astype(o_ref.dtype)


def _xattn_single(q3, mk_rows, mv_rows, mem_off):
    n = q3.shape[0]
    rows = N_MEM * X_HEADS
    return pl.pallas_call(
        _xattn_single_kernel,
        out_shape=jax.ShapeDtypeStruct((n, 1, X_W), F32),
        grid=(n,),
        in_specs=[pl.BlockSpec((1, 1, X_W), lambda b: (b, 0, 0)),
                  pl.BlockSpec((1, rows, X_HEAD_DIM), lambda b: (mem_off + b, 0, 0)),
                  pl.BlockSpec((1, rows, X_HEAD_DIM), lambda b: (mem_off + b, 0, 0))],
        out_specs=pl.BlockSpec((1, 1, X_W), lambda b: (b, 0, 0)),
        compiler_params=_params("parallel"),
        name="xattn_single",
    )(q3, mk_rows, mv_rows)


def _mlp_kernel(xb_ref, xf_ref, wu_ref, wd_ref, g_ref, b_ref, of_ref, ob_ref):
    f = pl.program_id(1)

    @pl.when(f == 0)
    def _():
        of_ref[...] = jnp.zeros_like(of_ref)

    h = jnp.maximum(_dot(xb_ref[...], wu_ref[...]), 0.0)
    of_ref[...] += _dot((h * h).astype(BF16), wd_ref[...])

    @pl.when(f == pl.num_programs(1) - 1)
    def _():
        y = _layer_norm(DN_ALPHA * xf_ref[...] + of_ref[...], g_ref[...], b_ref[...])
        of_ref[...] = y
        ob_ref[...] = y.astype(ob_ref.dtype)


def _mlp(xb, xf, w_up, w_down, g, b, layer, tm, tf):
    m = xb.shape[0]
    tm = min(tm, m)
    vec = _layer_spec(layer, (1, D_MODEL), lambda i, f: (0, 0))
    return pl.pallas_call(
        _mlp_kernel,
        out_shape=[jax.ShapeDtypeStruct((m, D_MODEL), F32), jax.ShapeDtypeStruct((m, D_MODEL), BF16)],
        grid=(m // tm, D_FF // tf),
        in_specs=[pl.BlockSpec((tm, D_MODEL), lambda i, f: (i, 0)),
                  pl.BlockSpec((tm, D_MODEL), lambda i, f: (i, 0)),
                  _layer_spec(layer, (D_MODEL, tf), lambda i, f: (0, f)),
                  _layer_spec(layer, (tf, D_MODEL), lambda i, f: (f, 0)), vec, vec],
        out_specs=[pl.BlockSpec((tm, D_MODEL), lambda i, f: (i, 0)),
                   pl.BlockSpec((tm, D_MODEL), lambda i, f: (i, 0))],
        compiler_params=_params("parallel", "arbitrary"),
        name="mlp",
    )(xb, xf, w_up, w_down, g, b)


_MLP_TM = 512
_MLP_TF = 1024
_MID_TM = 256
_PROJ_TM = 1024


def kernel(x_prompt, x_sample, mem_prompt, cache_k, cache_v, cache_mem_k, cache_mem_v, page_table,
           w_in, b_gate, sgu_ln_g, sgu_ln_b, w_s, b_s, w_pa, w_pb, w_o, ln1_g, ln1_b,
           w_xq, w_xk, w_xv, w_xo, ln2_g, ln2_b, w_up, w_down, ln3_g, ln3_b):
    batch, seq, _ = x_prompt.shape
    n_samples, dec_seq, _ = x_sample.shape
    assert dec_seq == 1 and seq % MOBA_BLOCK == 0
    assert N_KV_HEADS & (N_KV_HEADS - 1) == 0 and X_HEADS & (X_HEADS - 1) == 0 and X_HEADS <= SUBLANES
    n_pool = cache_k.shape[1]
    n_pages = page_table.shape[1]
    n_blocks = seq // MOBA_BLOCK
    assert n_pages % PAGES_PER_BLOCK == 0
    mp, ms = batch * seq, n_samples

    pt_flat = page_table.reshape(-1).astype(jnp.int32)
    cache_k_rows = cache_k.reshape(DEPTH * n_pool, PAGE_ROWS, HEAD_DIM)
    cache_mk_rows = cache_mem_k.reshape(DEPTH * n_samples, N_MEM * X_HEADS, X_HEAD_DIM)
    cache_mv_rows = cache_mem_v.reshape(DEPTH * n_samples, N_MEM * X_HEADS, X_HEAD_DIM)
    mem_b = mem_prompt.reshape(batch * N_MEM, D_MODEL).astype(BF16)

    xpf = x_prompt.reshape(mp, D_MODEL)
    xsf = x_sample.reshape(ms, D_MODEL)
    xpb, xsb = xpf.astype(BF16), xsf.astype(BF16)

    w = dict(
        b_gate=b_gate, w_pa=_to_bf16(w_pa), w_pb=_to_bf16(w_pb), w_o=_to_bf16(w_o), w_xo=_to_bf16(w_xo),
        w_up=_to_bf16(w_up), w_down=_to_bf16(w_down),
        ln1_g=ln1_g[:, None], ln1_b=ln1_b[:, None], ln2_g=ln2_g[:, None], ln2_b=ln2_b[:, None],
        ln3_g=ln3_g[:, None], ln3_b=ln3_b[:, None])
    w_in_b, w_xq_b, w_xk_b, w_xv_b = _to_bf16(w_in), _to_bf16(w_xq), _to_bf16(w_xk), _to_bf16(w_xv)

    sgu_g3, sgu_b3 = sgu_ln_g[:, None], sgu_ln_b[:, None]
    b_s_t = jnp.swapaxes(b_s, 1, 2)

    kp_l, vp_l, mkp_l, mvp_l, ks_l, vs_l, vns_l = [], [], [], [], [], [], []
    for l in range(DEPTH):
        mk_f, mk_b = _matmul(mem_b, w_xk_b, l, 0, X_W, (F32, BF16), 512, X_W)
        mv_f, mv_b = _matmul(mem_b, w_xv_b, l, 0, X_W, (F32, BF16), 512, X_W)
        (q_b,) = _matmul(xpb, w_in_b, l, 0, Q_W, (BF16,), _PROJ_TM, Q_W)
        k_rows, k_b, kmean = _kv_proj(xpb, w_in_b, l, Q_W, _PROJ_TM)
        v_rows, v_b, _ = _kv_proj(xpb, w_in_b, l, Q_W + KV_W, _PROJ_TM)
        (rest,) = _matmul(xpb, w_in_b, l, QKV_W, REST_W, (F32,), _PROJ_TM, 1024)
        vt = jnp.transpose(v_b.reshape(batch, n_blocks, MOBA_BLOCK, N_KV_HEADS, HEAD_DIM), (0, 3, 1, 4, 2))
        a = _moba_prompt(q_b, k_b, vt, kmean.reshape(batch, n_blocks, KV_W), batch, seq)
        x2f, x2b = _prompt_mid(a, rest, xpf, mk_b.reshape(batch, N_MEM, X_W), mv_b.reshape(batch, N_MEM, X_W),
                               sgu_g3, sgu_b3, w_s, b_s_t, w, w_xq_b, l, seq, _MID_TM)
        xpf, xpb = _mlp(x2b, x2f, w["w_up"], w["w_down"], w["ln3_g"], w["ln3_b"], l, _MLP_TM, _MLP_TF)
        kp_l.append(k_rows.reshape(batch, seq // PAGE_SIZE, PAGE_SIZE, N_KV_HEADS, HEAD_DIM))
        vp_l.append(v_rows.reshape(batch, seq // PAGE_SIZE, PAGE_SIZE, N_KV_HEADS, HEAD_DIM))
        mkp_l.append(mk_f.reshape(batch, N_MEM, X_HEADS, X_HEAD_DIM))
        mvp_l.append(mv_f.reshape(batch, N_MEM, X_HEADS, X_HEAD_DIM))

        (qkv_s,) = _matmul(xsb, w_in_b, l, 0, QKV_W, (F32,), ms, 1024)
        (rest_s,) = _matmul(xsb, w_in_b, l, QKV_W, REST_W, (F32,), ms, 1024)
        qkv_s3 = qkv_s.reshape(ms, 1, QKV_W)
        a_s = _cached_moba(qkv_s3, cache_k_rows, cache_v, pt_flat, l, n_pool, n_pages)
        a_s = a_s.reshape(ms, Q_W).astype(BF16)
        w0 = jnp.repeat(w_s[l][:, 0, 0], SGU_GROUP_DIM)[None]
        b0 = jnp.repeat(b_s[l][:, 0], SGU_GROUP_DIM)[None]
        s_s, vn_s = _sgu_first_rows(rest_s, sgu_ln_g[l][None], sgu_ln_b[l][None], w0, b0)
        mix_s = _merge(a_s, s_s, rest_s, w["b_gate"], w["w_pa"], w["w_pb"], l, ms, 1024)
        x1f, x1b = _proj_ln(mix_s, w["w_o"], xsf, w["ln1_g"], w["ln1_b"], l, ms)
        (qx,) = _matmul(x1b, w_xq_b, l, 0, X_W, (F32,), ms, X_W)
        o_s = _xattn_single(qx.reshape(ms, 1, X_W), cache_mk_rows, cache_mv_rows, l * n_samples)
        x2f, x2b = _proj_ln(o_s.reshape(ms, X_W).astype(BF16), w["w_xo"], x1f, w["ln2_g"], w["ln2_b"], l, ms)
        xsf, xsb = _mlp(x2b, x2f, w["w_up"], w["w_down"], w["ln3_g"], w["ln3_b"], l, ms, _MLP_TF)
        ks_l.append(qkv_s[:, Q_W:Q_W + KV_W].reshape(ms, 1, N_KV_HEADS, HEAD_DIM))
        vs_l.append(qkv_s[:, Q_W + KV_W:].reshape(ms, 1, N_KV_HEADS, HEAD_DIM))
        vns_l.append(vn_s.reshape(ms, 1, SGU_WIDTH))

    return (xpf.reshape(batch, seq, D_MODEL), xsf.reshape(ms, 1, D_MODEL),
            jnp.stack(kp_l), jnp.stack(vp_l), jnp.stack(mkp_l), jnp.stack(mvp_l),
            jnp.stack(ks_l), jnp.stack(vs_l), jnp.stack(vns_l))
```

```python
import functools

import jax
import jax.numpy as jnp
import numpy as np
from jax import lax
from jax.experimental import pallas as pl
from jax.experimental.pallas import tpu as pltpu

D_MODEL = 2048
DEPTH = 2
PAGE_SIZE = 128
N_HEADS = 8
N_KV_HEADS = 4
HEAD_DIM = 128
MOBA_BLOCK = 256
MOBA_TOPK = 3
SGU_WIDTH = 1024
SGU_GROUPS = 8
SGU_GROUP_DIM = SGU_WIDTH // SGU_GROUPS
SGU_CHUNK = 128
N_MEM = 256
X_HEADS = 4
X_HEAD_DIM = 128
D_FF = 4 * D_MODEL
DN_ALPHA = (2 * DEPTH) ** 0.25
LN_EPS = 1e-5
Q_W = N_HEADS * HEAD_DIM
KV_W = N_KV_HEADS * HEAD_DIM
X_W = X_HEADS * X_HEAD_DIM
QKV_W = Q_W + 2 * KV_W
REST_W = 2 * SGU_WIDTH + 2 * D_MODEL
PAGES_PER_BLOCK = MOBA_BLOCK // PAGE_SIZE
PAGE_ROWS = PAGE_SIZE * N_KV_HEADS
MASKED = -1e30

LANES = 128
SUBLANES = 8
VMEM_LIMIT = 56 * 1024 * 1024

BF16 = jnp.bfloat16
F32 = jnp.float32
_NT = (((1,), (1,)), ((), ()))


def _params(*sem):
    return pltpu.CompilerParams(dimension_semantics=sem, vmem_limit_bytes=VMEM_LIMIT)


def _dot(a, b):
    return jnp.dot(a, b, preferred_element_type=F32)


def _dot_nt(a, b):
    return lax.dot_general(a, b, _NT, preferred_element_type=F32)


def _gelu(x):
    c = np.float32(np.sqrt(2 / np.pi))
    return x * (0.5 * (1.0 + jnp.tanh(c * (x + 0.044715 * (x * x * x)))))


def _layer_norm(z, g, b):
    mu = jnp.mean(z, axis=-1, keepdims=True)
    d = z - mu
    var = jnp.mean(d * d, axis=-1, keepdims=True)
    return d * lax.rsqrt(var + LN_EPS) * g + b


def _top_blocks(gate, valid_f, idx_f, axis):
    sel = jnp.zeros(gate.shape, F32)
    for _ in range(MOBA_TOPK):
        m = jnp.max(gate, axis=axis, keepdims=True)
        first = jnp.min(jnp.where(gate == m, idx_f, float(gate.shape[axis])), axis=axis, keepdims=True)
        pick = idx_f == first
        sel = jnp.where(pick, valid_f, sel)
        gate = jnp.where(pick, -jnp.inf, gate)
    return sel


def _rows_to_sublanes(ref, n_rows, width):
    sub = lax.broadcasted_iota(jnp.int32, (SUBLANES, width), 0)
    out = jnp.zeros((SUBLANES, width), F32)
    for r in range(n_rows):
        out = jnp.where(sub == r, ref[0, :, r * width:(r + 1) * width].astype(F32), out)
    return out


def _mm_kernel(x_ref, w_ref, *out_refs):
    acc = _dot(x_ref[...], w_ref[...])
    for o in out_refs:
        o[...] = acc.astype(o.dtype)


def _cast_kernel(x_ref, o_ref):
    o_ref[...] = x_ref[...].astype(o_ref.dtype)


_CAST_BLOCK_BYTES = 4 * 1024 * 1024


def _to_bf16(w):
    d, k, n = w.shape
    rows = d * k
    tr = min(rows, max(SUBLANES, _CAST_BLOCK_BYTES // (4 * n)))
    assert rows % tr == 0
    out = pl.pallas_call(
        _cast_kernel,
        out_shape=jax.ShapeDtypeStruct((rows, n), BF16),
        grid=(rows // tr,),
        in_specs=[pl.BlockSpec((tr, n), lambda i: (i, 0))],
        out_specs=pl.BlockSpec((tr, n), lambda i: (i, 0)),
        compiler_params=_params("parallel"),
        name="to_bf16",
    )(w.reshape(rows, n))
    return out.reshape(d, k, n)


def _layer_spec(layer, block, index):
    return pl.BlockSpec((None,) + block, lambda *g: (layer,) + index(*g))


def _matmul(x, w, layer, col_off, ncols, out_dtypes, tm, tn):
    m, k = x.shape
    tm = min(tm, m)
    tn = min(tn, ncols)
    assert m % tm == 0 and ncols % tn == 0 and col_off % tn == 0
    joff = col_off // tn
    outs = pl.pallas_call(
        _mm_kernel,
        out_shape=[jax.ShapeDtypeStruct((m, ncols), dt) for dt in out_dtypes],
        grid=(ncols // tn, m // tm),
        in_specs=[pl.BlockSpec((tm, k), lambda j, i: (i, 0)),
                  _layer_spec(layer, (k, tn), lambda j, i: (0, j + joff))],
        out_specs=[pl.BlockSpec((tm, tn), lambda j, i: (i, j)) for _ in out_dtypes],
        compiler_params=_params("parallel", "parallel"),
        name="matmul",
    )(x, w)
    return outs


def _mm_cast_kernel(x_ref, w_ref, o_ref, wb_ref):
    wb = w_ref[...].astype(BF16)
    wb_ref[...] = wb
    o_ref[...] = _dot(x_ref[...], wb)


def _matmul_casting(x, w_f32, layer, tn):
    m, k = x.shape
    n = w_f32.shape[2]
    assert n % tn == 0
    return pl.pallas_call(
        _mm_cast_kernel,
        out_shape=[jax.ShapeDtypeStruct((m, n), F32), jax.ShapeDtypeStruct((1, k, n), BF16)],
        grid=(n // tn,),
        in_specs=[pl.BlockSpec((m, k), lambda j: (0, 0)),
                  _layer_spec(layer, (k, tn), lambda j: (0, j))],
        out_specs=[pl.BlockSpec((m, tn), lambda j: (0, j)),
                   pl.BlockSpec((None, k, tn), lambda j: (0, 0, j))],
        compiler_params=_params("parallel"),
        name="matmul_casting",
    )(x, w_f32)


def _kv_proj_kernel(x_ref, w_ref, rows_ref, ob_ref, km_ref):
    acc = _dot(x_ref[...], w_ref[...])
    tm = acc.shape[0]
    for h in range(N_KV_HEADS):
        rows_ref[pl.ds(h, tm, stride=N_KV_HEADS), :] = acc[:, h * HEAD_DIM:(h + 1) * HEAD_DIM]
    ob_ref[...] = acc.astype(ob_ref.dtype)
    for c in range(tm // MOBA_BLOCK):
        blk = acc[c * MOBA_BLOCK:(c + 1) * MOBA_BLOCK]
        km_ref[c] = jnp.sum(blk, axis=0, keepdims=True) * (1.0 / MOBA_BLOCK)


def _kv_proj(x, w, layer, col_off, tm):
    m, k = x.shape
    assert m % tm == 0 and tm % MOBA_BLOCK == 0 and col_off % KV_W == 0
    return pl.pallas_call(
        _kv_proj_kernel,
        out_shape=[jax.ShapeDtypeStruct((m * N_KV_HEADS, HEAD_DIM), F32),
                   jax.ShapeDtypeStruct((m, KV_W), BF16),
                   jax.ShapeDtypeStruct((m // MOBA_BLOCK, 1, KV_W), F32)],
        grid=(m // tm,),
        in_specs=[pl.BlockSpec((tm, k), lambda i: (i, 0)),
                  _layer_spec(layer, (k, KV_W), lambda i: (0, col_off // KV_W))],
        out_specs=[pl.BlockSpec((tm * N_KV_HEADS, HEAD_DIM), lambda i: (i, 0)),
                   pl.BlockSpec((tm, KV_W), lambda i: (i, 0)),
                   pl.BlockSpec((tm // MOBA_BLOCK, 1, KV_W), lambda i: (i, 0, 0))],
        compiler_params=_params("parallel"),
        name="kv_proj",
    )(x, w)


_MOBA_KV_PER_STEP = 4


def _moba_prompt_kernel(q_ref, k_ref, vt_ref, km_ref, o_ref, sel_ref, m_ref, l_ref, acc_ref):
    i = pl.program_id(2)
    rep = N_HEADS // N_KV_HEADS
    heads = _MOBA_KV_PER_STEP * rep
    blk = MOBA_BLOCK
    scale_log2e = np.float32(HEAD_DIM ** -0.5 * np.log2(np.e))

    def cols(c):
        return slice(c * HEAD_DIM, (c + 1) * HEAD_DIM)

    qs = [q_ref[:, cols(c)] for c in range(heads)]
    blk_id = lax.broadcasted_iota(jnp.int32, (km_ref.shape[0], blk), 0)
    valid = blk_id < i
    for c in range(heads):
        km = km_ref[:, cols(c // rep)].astype(BF16)
        gate = jnp.where(valid, _dot_nt(km, qs[c]), MASKED)
        sel_ref[c] = _top_blocks(gate, valid.astype(F32), blk_id.astype(F32), 0)

    def attend(j, masks, first):
        kjs = [k_ref[pl.ds(pl.multiple_of(j * blk, blk), blk), cols(g)] for g in range(_MOBA_KV_PER_STEP)]
        scores = [_dot_nt(kjs[c // rep], qs[c]) for c in range(heads)]
        ps, m_news, l_blks = [], [], []
        for c in range(heads):
            s = jnp.where(masks[c], scores[c] * scale_log2e, MASKED)
            m_blk = jnp.max(s, axis=0, keepdims=True)
            m_new = m_blk if first else jnp.maximum(m_ref[c], m_blk)
            p = jnp.exp2(s - m_new)
            l_blks.append(jnp.sum(p, axis=0, keepdims=True))
            ps.append(p.astype(BF16))
            m_news.append(m_new)
        pvs = [_dot(vt_ref[c // rep, j], ps[c]) for c in range(heads)]
        for c in range(heads):
            if first:
                l_ref[c] = l_blks[c]
                acc_ref[c] = pvs[c]
            else:
                a = jnp.exp2(m_ref[c] - m_news[c])
                l_ref[c] = a * l_ref[c] + l_blks[c]
                acc_ref[c] = a * acc_ref[c] + pvs[c]
            m_ref[c] = m_news[c]

    key = lax.broadcasted_iota(jnp.int32, (blk, blk), 0)
    qry = lax.broadcasted_iota(jnp.int32, (blk, blk), 1)
    attend(i, [key <= qry] * heads, True)

    def body(j, carry):
        attend(j, [sel_ref[c, pl.ds(j, 1), :] > 0.5 for c in range(heads)], False)
        return carry

    lax.fori_loop(0, i, body, 0)
    for c in range(heads):
        o_ref[:, cols(c)] = (acc_ref[c] / l_ref[c]).T.astype(o_ref.dtype)


def _moba_prompt(q_b, k_b, vt, kmean, batch, seq):
    nq = seq // MOBA_BLOCK
    g = _MOBA_KV_PER_STEP
    heads = g * (N_HEADS // N_KV_HEADS)
    assert N_KV_HEADS % g == 0
    return pl.pallas_call(
        _moba_prompt_kernel,
        out_shape=jax.ShapeDtypeStruct((batch * seq, Q_W), BF16),
        grid=(batch, N_KV_HEADS // g, nq),
        in_specs=[
            pl.BlockSpec((MOBA_BLOCK, heads * HEAD_DIM), lambda b, gg, i: (b * nq + i, gg)),
            pl.BlockSpec((seq, g * HEAD_DIM), lambda b, gg, i: (b, gg)),
            pl.BlockSpec((None, g, nq, HEAD_DIM, MOBA_BLOCK), lambda b, gg, i: (b, gg, 0, 0, 0)),
            pl.BlockSpec((None, nq, g * HEAD_DIM), lambda b, gg, i: (b, 0, gg)),
        ],
        out_specs=pl.BlockSpec((MOBA_BLOCK, heads * HEAD_DIM), lambda b, gg, i: (b * nq + i, gg)),
        scratch_shapes=[pltpu.VMEM((heads, nq, MOBA_BLOCK), F32),
                        pltpu.VMEM((heads, 1, MOBA_BLOCK), F32),
                        pltpu.VMEM((heads, 1, MOBA_BLOCK), F32),
                        pltpu.VMEM((heads, HEAD_DIM, MOBA_BLOCK), F32)],
        compiler_params=_params("parallel", "parallel", "arbitrary"),
        name="moba_prompt",
    )(q_b, k_b, vt, kmean)


def _cached_moba_kernel(pt_ref, qkv_ref, ck_hbm, cv_hbm, o_ref, kbuf, vbuf, km_ref, ksem, vsem,
                        *, layer, n_pool, n_pages):
    b = pl.program_id(0)
    n = pl.num_programs(0)
    slot = b % 2
    rep = N_HEADS // N_KV_HEADS
    n_blocks = n_pages // PAGES_PER_BLOCK
    scale = HEAD_DIM ** -0.5
    base = layer * n_pool

    def k_copy(sample, page_slot, sl):
        page = base + pt_ref[sample * n_pages + page_slot]
        return pltpu.make_async_copy(ck_hbm.at[page], kbuf.at[sl, page_slot], ksem.at[sl])

    def start_keys(sample, sl):
        for p in range(n_pages):
            k_copy(sample, p, sl).start()

    @pl.when(b == 0)
    def _():
        km_ref[...] = jnp.zeros_like(km_ref)
        start_keys(0, 0)

    @pl.when(b + 1 < n)
    def _():
        start_keys(b + 1, 1 - slot)

    for p in range(n_pages):
        k_copy(b, p, slot).wait()

    for blk in range(n_blocks):
        tot = jnp.zeros((SUBLANES, HEAD_DIM), F32)
        for r in range(PAGES_PER_BLOCK):
            page = kbuf[slot, PAGES_PER_BLOCK * blk + r]
            tot = tot + jnp.sum(page.reshape(PAGE_ROWS // SUBLANES, SUBLANES, HEAD_DIM), axis=0)
        km_ref[blk * SUBLANES:(blk + 1) * SUBLANES, :] = tot + pltpu.roll(tot, N_KV_HEADS, axis=0)

    q8 = _rows_to_sublanes(qkv_ref, N_HEADS, HEAD_DIM)
    q8b = q8.astype(BF16)
    hrow = lax.broadcasted_iota(jnp.int32, (N_HEADS, LANES), 0)
    lane = lax.broadcasted_iota(jnp.int32, (N_HEADS, LANES), 1)
    gate = jnp.zeros((N_HEADS, LANES), F32)
    for kvh in range(N_KV_HEADS):
        km = km_ref[pl.ds(kvh, LANES, stride=SUBLANES), :] * (1.0 / MOBA_BLOCK)
        gate = jnp.where(hrow >= kvh * rep, _dot_nt(q8b, km.astype(BF16)), gate)
    gate = jnp.where(lane < n_blocks, gate, MASKED)
    lane_f = lane.astype(F32)
    picks = []
    for _ in range(MOBA_TOPK):
        m = jnp.max(gate, axis=-1, keepdims=True)
        first = jnp.min(jnp.where(gate == m, lane_f, float(LANES)), axis=-1, keepdims=True)
        picks.append(first.astype(jnp.int32))
        gate = jnp.where(lane_f == first, -jnp.inf, gate)
    blocks = [[picks[t][h, 0] for t in range(MOBA_TOPK)] for h in range(N_HEADS)]

    def v_copy(h, t, r):
        page = pt_ref[b * n_pages + blocks[h][t] * PAGES_PER_BLOCK + r]
        return pltpu.make_async_copy(cv_hbm.at[layer, page, :, h // rep, :],
                                     vbuf.at[(h * MOBA_TOPK + t) * PAGES_PER_BLOCK + r], vsem)

    sel = [(h, t, r) for h in range(N_HEADS) for t in range(MOBA_TOPK) for r in range(PAGES_PER_BLOCK)]
    for h, t, r in sel:
        v_copy(h, t, r).start()

    k_new = qkv_ref[0, :, Q_W:Q_W + KV_W]
    v_new = qkv_ref[0, :, Q_W + KV_W:]
    scores = {}
    for h, t, r in sel:
        kvh = h // rep
        kp = kbuf[slot, blocks[h][t] * PAGES_PER_BLOCK + r, pl.ds(kvh, PAGE_SIZE, stride=N_KV_HEADS), :]
        qh = jnp.broadcast_to(q8b[h:h + 1], (SUBLANES, HEAD_DIM))
        scores[h, t, r] = _dot_nt(qh, kp.astype(BF16))[0:1] * scale
    probs, p_new = {}, []
    for h in range(N_HEADS):
        kvh = h // rep
        kn = k_new[:, kvh * HEAD_DIM:(kvh + 1) * HEAD_DIM].astype(BF16).astype(F32)
        s_new = jnp.sum(q8b[h:h + 1].astype(F32) * kn, axis=-1, keepdims=True) * scale
        mine = [scores[h, t, r] for t in range(MOBA_TOPK) for r in range(PAGES_PER_BLOCK)]
        m = s_new
        for s in mine:
            m = jnp.maximum(m, jnp.max(s, axis=-1, keepdims=True))
        es = [jnp.exp(s - m) for s in mine]
        e_new = jnp.exp(s_new - m)
        denom = e_new
        for e in es:
            denom = denom + jnp.sum(e, axis=-1, keepdims=True)
        p_new.append((e_new / denom).astype(BF16).astype(F32))
        for idx, (t, r) in enumerate((t, r) for t in range(MOBA_TOPK) for r in range(PAGES_PER_BLOCK)):
            probs[h, t, r] = (es[idx] / denom).astype(BF16)

    for h, t, r in sel:
        v_copy(h, t, r).wait()

    for h in range(N_HEADS):
        kvh = h // rep
        out = p_new[h] * v_new[:, kvh * HEAD_DIM:(kvh + 1) * HEAD_DIM].astype(BF16).astype(F32)
        for t in range(MOBA_TOPK):
            for r in range(PAGES_PER_BLOCK):
                p8 = jnp.broadcast_to(probs[h, t, r], (SUBLANES, PAGE_SIZE))
                vp = vbuf[(h * MOBA_TOPK + t) * PAGES_PER_BLOCK + r]
                out = out + _dot(p8, vp.astype(BF16))[0:1]
        o_ref[0, :, h * HEAD_DIM:(h + 1) * HEAD_DIM] = out


def _cached_moba(qkv_f3, cache_k_rows, cache_v, pt_flat, layer, n_pool, n_pages):
    n = qkv_f3.shape[0]
    assert SUBLANES == 2 * N_KV_HEADS and n_pages // PAGES_PER_BLOCK <= LANES
    return pl.pallas_call(
        functools.partial(_cached_moba_kernel, layer=layer, n_pool=n_pool, n_pages=n_pages),
        out_shape=jax.ShapeDtypeStruct((n, 1, Q_W), F32),
        grid_spec=pltpu.PrefetchScalarGridSpec(
            num_scalar_prefetch=1,
            grid=(n,),
            in_specs=[pl.BlockSpec((1, 1, QKV_W), lambda b, pt: (b, 0, 0)),
                      pl.BlockSpec(memory_space=pl.ANY),
                      pl.BlockSpec(memory_space=pl.ANY)],
            out_specs=pl.BlockSpec((1, 1, Q_W), lambda b, pt: (b, 0, 0)),
            scratch_shapes=[pltpu.VMEM((2, n_pages, PAGE_ROWS, HEAD_DIM), F32),
                            pltpu.VMEM((N_HEADS * MOBA_TOPK * PAGES_PER_BLOCK, PAGE_SIZE, HEAD_DIM), F32),
                            pltpu.VMEM((LANES * SUBLANES, HEAD_DIM), F32),
                            pltpu.SemaphoreType.DMA((2,)),
                            pltpu.SemaphoreType.DMA(())],
        ),
        compiler_params=_params("arbitrary"),
        name="cached_moba",
    )(pt_flat, qkv_f3, cache_k_rows, cache_v)


def _sgu_chunk_kernel(zu_ref, zv_ref, g_ref, b_ref, ws_ref, bs_ref, s_ref):
    vn = _layer_norm(_gelu(zv_ref[...]), g_ref[...], b_ref[...])
    gu = _gelu(zu_ref[...])
    t = SGU_CHUNK
    row = lax.broadcasted_iota(jnp.int32, (t, t), 0)
    col = lax.broadcasted_iota(jnp.int32, (t, t), 1)
    for g in range(SGU_GROUPS):
        cs = slice(g * SGU_GROUP_DIM, (g + 1) * SGU_GROUP_DIM)
        ws = jnp.where(col <= row, ws_ref[g], 0.0).astype(BF16)
        bias = bs_ref[:, g:g + 1]
        for c in range(zu_ref.shape[0] // t):
            rs = slice(c * t, (c + 1) * t)
            mixed = _dot(ws, vn[rs, cs].astype(BF16)) + bias
            s_ref[rs, cs] = (gu[rs, cs] * mixed).astype(s_ref.dtype)


def _prompt_mid_kernel(a_ref, zu_ref, zv_ref, ga_ref, gb_ref, x_ref, mk_ref, mv_ref,
                       sg_ref, sb_ref, ws_ref, bs_ref, bg_ref, wpa_ref, wpb_ref, wo_ref, g1_ref, b1_ref,
                       wxq_ref, wxo_ref, g2_ref, b2_ref, of_ref, ob_ref, s_scr, o_scr):
    _sgu_chunk_kernel(zu_ref, zv_ref, sg_ref, sb_ref, ws_ref, bs_ref, s_scr)
    g_a = jax.nn.sigmoid(ga_ref[...] + bg_ref[0:1, :])
    g_b = jax.nn.sigmoid(gb_ref[...] + bg_ref[1:2, :])
    mix = g_a * _dot(a_ref[...], wpa_ref[...]) + g_b * _dot(s_scr[...], wpb_ref[...])
    x1 = _layer_norm(DN_ALPHA * x_ref[...] + _dot(mix.astype(BF16), wo_ref[...]), g1_ref[...], b1_ref[...])
    qx = _dot(x1.astype(BF16), wxq_ref[...]).astype(BF16)
    scale = X_HEAD_DIM ** -0.5
    for h in range(X_HEADS):
        cs = slice(h * X_HEAD_DIM, (h + 1) * X_HEAD_DIM)
        s = _dot_nt(qx[:, cs], mk_ref[:, cs]) * scale
        e = jnp.exp(s - jnp.max(s, axis=-1, keepdims=True))
        p = (e / jnp.sum(e, axis=-1, keepdims=True)).astype(BF16)
        o_scr[:, cs] = _dot(p, mv_ref[:, cs]).astype(o_scr.dtype)
    y = _layer_norm(DN_ALPHA * x1 + _dot(o_scr[...], wxo_ref[...]), g2_ref[...], b2_ref[...])
    of_ref[...] = y
    ob_ref[...] = y.astype(ob_ref.dtype)


def _prompt_mid(a, rest, x, mk, mv, sgu_g, sgu_b, w_s, b_s_t, w, w_xq, layer, seq, tm):
    m = a.shape[0]
    assert m % tm == 0 and seq % tm == 0 and tm % SGU_CHUNK == 0
    tiles_per_batch = seq // tm

    def const(block):
        return pl.BlockSpec((None,) + block, lambda i: (layer,) + (0,) * len(block), pipeline_mode=pl.Buffered(1))

    mem = pl.BlockSpec((None, N_MEM, X_W), lambda i: (i // tiles_per_batch, 0, 0))
    vec_d = const((1, D_MODEL))
    return pl.pallas_call(
        _prompt_mid_kernel,
        out_shape=[jax.ShapeDtypeStruct((m, D_MODEL), F32), jax.ShapeDtypeStruct((m, D_MODEL), BF16)],
        grid=(m // tm,),
        in_specs=[pl.BlockSpec((tm, Q_W), lambda i: (i, 0)),
                  pl.BlockSpec((tm, SGU_WIDTH), lambda i: (i, 0)),
                  pl.BlockSpec((tm, SGU_WIDTH), lambda i: (i, 1)),
                  pl.BlockSpec((tm, D_MODEL), lambda i: (i, 2 * SGU_WIDTH // D_MODEL)),
                  pl.BlockSpec((tm, D_MODEL), lambda i: (i, 2 * SGU_WIDTH // D_MODEL + 1)),
                  pl.BlockSpec((tm, D_MODEL), lambda i: (i, 0)),
                  mem, mem,
                  const((1, SGU_WIDTH)), const((1, SGU_WIDTH)),
                  const((SGU_GROUPS, SGU_CHUNK, SGU_CHUNK)), const((SGU_CHUNK, SGU_GROUPS)),
                  const((2, D_MODEL)), const((Q_W, D_MODEL)), const((SGU_WIDTH, D_MODEL)),
                  const((D_MODEL, D_MODEL)), vec_d, vec_d,
                  const((D_MODEL, X_W)), const((X_W, D_MODEL)), vec_d, vec_d],
        out_specs=[pl.BlockSpec((tm, D_MODEL), lambda i: (i, 0)), pl.BlockSpec((tm, D_MODEL), lambda i: (i, 0))],
        scratch_shapes=[pltpu.VMEM((tm, SGU_WIDTH), BF16), pltpu.VMEM((tm, X_W), BF16)],
        compiler_params=_params("parallel"),
        name="prompt_mid",
    )(a, rest, rest, rest, rest, x, mk, mv, sgu_g, sgu_b, w_s, b_s_t,
      w["b_gate"], w["w_pa"], w["w_pb"], w["w_o"], w["ln1_g"], w["ln1_b"],
      w_xq, w["w_xo"], w["ln2_g"], w["ln2_b"])


def _sgu_first_row_kernel(zu_ref, zv_ref, g_ref, b_ref, w0_ref, b0_ref, s_ref, vn_ref):
    vn = _layer_norm(_gelu(zv_ref[...]), g_ref[...], b_ref[...])
    vn_ref[...] = vn
    s_ref[...] = (_gelu(zu_ref[...]) * (vn * w0_ref[...] + b0_ref[...])).astype(s_ref.dtype)


def _sgu_first_rows(rest, ln_g, ln_b, w0, b0):
    m = rest.shape[0]
    vec = pl.BlockSpec((1, SGU_WIDTH), lambda i: (0, 0))
    return pl.pallas_call(
        _sgu_first_row_kernel,
        out_shape=[jax.ShapeDtypeStruct((m, SGU_WIDTH), BF16), jax.ShapeDtypeStruct((m, SGU_WIDTH), F32)],
        grid=(1,),
        in_specs=[pl.BlockSpec((m, SGU_WIDTH), lambda i: (0, 0)),
                  pl.BlockSpec((m, SGU_WIDTH), lambda i: (0, 1)), vec, vec, vec, vec],
        out_specs=[pl.BlockSpec((m, SGU_WIDTH), lambda i: (0, 0)), pl.BlockSpec((m, SGU_WIDTH), lambda i: (0, 0))],
        compiler_params=_params("arbitrary"),
        name="sgu_first_rows",
    )(rest, rest, ln_g, ln_b, w0, b0)


def _merge_kernel(a_ref, s_ref, ga_ref, gb_ref, bg_ref, wa_ref, wb_ref, o_ref):
    g_a = jax.nn.sigmoid(ga_ref[...] + bg_ref[0:1, :])
    g_b = jax.nn.sigmoid(gb_ref[...] + bg_ref[1:2, :])
    mix = g_a * _dot(a_ref[...], wa_ref[...]) + g_b * _dot(s_ref[...], wb_ref[...])
    o_ref[...] = mix.astype(o_ref.dtype)


def _merge(a, s, rest, b_gate, w_pa, w_pb, layer, tm, tn):
    m = a.shape[0]
    tm = min(tm, m)
    ga0 = 2 * SGU_WIDTH // tn
    gb0 = (2 * SGU_WIDTH + D_MODEL) // tn
    return pl.pallas_call(
        _merge_kernel,
        out_shape=jax.ShapeDtypeStruct((m, D_MODEL), BF16),
        grid=(D_MODEL // tn, m // tm),
        in_specs=[pl.BlockSpec((tm, Q_W), lambda j, i: (i, 0)),
                  pl.BlockSpec((tm, SGU_WIDTH), lambda j, i: (i, 0)),
                  pl.BlockSpec((tm, tn), lambda j, i: (i, ga0 + j)),
                  pl.BlockSpec((tm, tn), lambda j, i: (i, gb0 + j)),
                  _layer_spec(layer, (2, tn), lambda j, i: (0, j)),
                  _layer_spec(layer, (Q_W, tn), lambda j, i: (0, j)),
                  _layer_spec(layer, (SGU_WIDTH, tn), lambda j, i: (0, j))],
        out_specs=pl.BlockSpec((tm, tn), lambda j, i: (i, j)),
        compiler_params=_params("parallel", "parallel"),
        name="merge",
    )(a, s, rest, rest, b_gate, w_pa, w_pb)


def _proj_ln_kernel(a_ref, w_ref, x_ref, g_ref, b_ref, of_ref, ob_ref):
    z = DN_ALPHA * x_ref[...] + _dot(a_ref[...], w_ref[...])
    y = _layer_norm(z, g_ref[...], b_ref[...])
    of_ref[...] = y
    ob_ref[...] = y.astype(ob_ref.dtype)


def _proj_ln(a, w, x, g, b, layer, tm):
    m, k = a.shape
    tm = min(tm, m)
    vec = _layer_spec(layer, (1, D_MODEL), lambda i: (0, 0))
    return pl.pallas_call(
        _proj_ln_kernel,
        out_shape=[jax.ShapeDtypeStruct((m, D_MODEL), F32), jax.ShapeDtypeStruct((m, D_MODEL), BF16)],
        grid=(m // tm,),
        in_specs=[pl.BlockSpec((tm, k), lambda i: (i, 0)),
                  _layer_spec(layer, (k, D_MODEL), lambda i: (0, 0)),
                  pl.BlockSpec((tm, D_MODEL), lambda i: (i, 0)), vec, vec],
        out_specs=[pl.BlockSpec((tm, D_MODEL), lambda i: (i, 0)), pl.BlockSpec((tm, D_MODEL), lambda i: (i, 0))],
        compiler_params=_params("parallel"),
        name="proj_ln",
    )(a, w, x, g, b)


def _xattn_single_kernel(q_ref, mk_ref, mv_ref, o_ref):
    scale = X_HEAD_DIM ** -0.5
    rows = N_MEM * X_HEADS
    q8 = _rows_to_sublanes(q_ref, X_HEADS, X_HEAD_DIM).astype(BF16)
    head = lax.broadcasted_iota(jnp.int32, (SUBLANES, rows), 0)
    row_head = jnp.bitwise_and(lax.broadcasted_iota(jnp.int32, (SUBLANES, rows), 1), X_HEADS - 1)
    s = jnp.where(row_head == head, _dot_nt(q8, mk_ref[0].astype(BF16)) * scale, MASKED)
    e = jnp.exp(s - jnp.max(s, axis=-1, keepdims=True))
    p = (e / jnp.sum(e, axis=-1, keepdims=True)).astype(BF16)
    o = _dot(p, mv_ref[0].astype(BF16))
    for h in range(X_HEADS):
        o_ref[0, :, h * X_HEAD_DIM:(h + 1) * X_HEAD_DIM] = o[h:h + 1].astype(o_ref.dtype)


def _xattn_single(q3, mk_rows, mv_rows, mem_off):
    n = q3.shape[0]
    rows = N_MEM * X_HEADS
    return pl.pallas_call(
        _xattn_single_kernel,
        out_shape=jax.ShapeDtypeStruct((n, 1, X_W), F32),
        grid=(n,),
        in_specs=[pl.BlockSpec((1, 1, X_W), lambda b: (b, 0, 0)),
                  pl.BlockSpec((1, rows, X_HEAD_DIM), lambda b: (mem_off + b, 0, 0)),
                  pl.BlockSpec((1, rows, X_HEAD_DIM), lambda b: (mem_off + b, 0, 0))],
        out_specs=pl.BlockSpec((1, 1, X_W), lambda b: (b, 0, 0)),
        compiler_params=_params("parallel"),
        name="xattn_single",
    )(q3, mk_rows, mv_rows)


def _mlp_kernel(xb_ref, xf_ref, wu_ref, wd_ref, g_ref, b_ref, of_ref, ob_ref):
    f = pl.program_id(1)

    @pl.when(f == 0)
    def _():
        of_ref[...] = jnp.zeros_like(of_ref)

    h = jnp.maximum(_dot(xb_ref[...], wu_ref[...]), 0.0)
    of_ref[...] += _dot((h * h).astype(BF16), wd_ref[...])

    @pl.when(f == pl.num_programs(1) - 1)
    def _():
        y = _layer_norm(DN_ALPHA * xf_ref[...] + of_ref[...], g_ref[...], b_ref[...])
        of_ref[...] = y
        ob_ref[...] = y.astype(ob_ref.dtype)


def _mlp(xb, xf, w_up, w_down, w_layer, g, b, layer, tm, tf):
    m = xb.shape[0]
    tm = min(tm, m)
    vec = _layer_spec(layer, (1, D_MODEL), lambda i, f: (0, 0))
    return pl.pallas_call(
        _mlp_kernel,
        out_shape=[jax.ShapeDtypeStruct((m, D_MODEL), F32), jax.ShapeDtypeStruct((m, D_MODEL), BF16)],
        grid=(m // tm, D_FF // tf),
        in_specs=[pl.BlockSpec((tm, D_MODEL), lambda i, f: (i, 0)),
                  pl.BlockSpec((tm, D_MODEL), lambda i, f: (i, 0)),
                  _layer_spec(w_layer, (D_MODEL, tf), lambda i, f: (0, f)),
                  _layer_spec(w_layer, (tf, D_MODEL), lambda i, f: (f, 0)), vec, vec],
        out_specs=[pl.BlockSpec((tm, D_MODEL), lambda i, f: (i, 0)),
                   pl.BlockSpec((tm, D_MODEL), lambda i, f: (i, 0))],
        compiler_params=_params("parallel", "arbitrary"),
        name="mlp",
    )(xb, xf, w_up, w_down, g, b)


def _mlp_cast_kernel(xb_ref, xf_ref, wu_ref, wd_ref, g_ref, b_ref, of_ref, ob_ref, wub_ref, wdb_ref):
    wub_ref[...] = wu_ref[...].astype(BF16)
    wdb_ref[...] = wd_ref[...].astype(BF16)
    _mlp_kernel(xb_ref, xf_ref, wub_ref, wdb_ref, g_ref, b_ref, of_ref, ob_ref)


def _mlp_casting(xb, xf, w_up_f32, w_down_f32, g, b, layer, tf):
    m = xb.shape[0]
    vec = _layer_spec(layer, (1, D_MODEL), lambda i, f: (0, 0))
    rows = pl.BlockSpec((m, D_MODEL), lambda i, f: (0, 0))
    return pl.pallas_call(
        _mlp_cast_kernel,
        out_shape=[jax.ShapeDtypeStruct((m, D_MODEL), F32), jax.ShapeDtypeStruct((m, D_MODEL), BF16),
                   jax.ShapeDtypeStruct((1, D_MODEL, D_FF), BF16), jax.ShapeDtypeStruct((1, D_FF, D_MODEL), BF16)],
        grid=(1, D_FF // tf),
        in_specs=[rows, rows,
                  _layer_spec(layer, (D_MODEL, tf), lambda i, f: (0, f)),
                  _layer_spec(layer, (tf, D_MODEL), lambda i, f: (f, 0)), vec, vec],
        out_specs=[rows, rows,
                   pl.BlockSpec((None, D_MODEL, tf), lambda i, f: (0, 0, f)),
                   pl.BlockSpec((None, tf, D_MODEL), lambda i, f: (0, f, 0))],
        compiler_params=_params("parallel", "arbitrary"),
        name="mlp_casting",
    )(xb, xf, w_up_f32, w_down_f32, g, b)


_MLP_TM = 512
_MLP_TF = 1024
_MLP_CAST_TF = 512
_MID_TM = 256
_PROJ_TM = 1024


def kernel(x_prompt, x_sample, mem_prompt, cache_k, cache_v, cache_mem_k, cache_mem_v, page_table,
           w_in, b_gate, sgu_ln_g, sgu_ln_b, w_s, b_s, w_pa, w_pb, w_o, ln1_g, ln1_b,
           w_xq, w_xk, w_xv, w_xo, ln2_g, ln2_b, w_up, w_down, ln3_g, ln3_b):
    batch, seq, _ = x_prompt.shape
    n_samples, dec_seq, _ = x_sample.shape
    assert dec_seq == 1 and seq % MOBA_BLOCK == 0
    assert N_KV_HEADS & (N_KV_HEADS - 1) == 0 and X_HEADS & (X_HEADS - 1) == 0 and X_HEADS <= SUBLANES
    n_pool = cache_k.shape[1]
    n_pages = page_table.shape[1]
    n_blocks = seq // MOBA_BLOCK
    assert n_pages % PAGES_PER_BLOCK == 0
    mp, ms = batch * seq, n_samples

    pt_flat = page_table.reshape(-1).astype(jnp.int32)
    cache_k_rows = cache_k.reshape(DEPTH * n_pool, PAGE_ROWS, HEAD_DIM)
    cache_mk_rows = cache_mem_k.reshape(DEPTH * n_samples, N_MEM * X_HEADS, X_HEAD_DIM)
    cache_mv_rows = cache_mem_v.reshape(DEPTH * n_samples, N_MEM * X_HEADS, X_HEAD_DIM)
    mem_b = mem_prompt.reshape(batch * N_MEM, D_MODEL).astype(BF16)

    xpf = x_prompt.reshape(mp, D_MODEL)
    xsf = x_sample.reshape(ms, D_MODEL)
    xpb, xsb = xpf.astype(BF16), xsf.astype(BF16)

    w = dict(
        b_gate=b_gate, w_pa=_to_bf16(w_pa), w_pb=_to_bf16(w_pb), w_o=_to_bf16(w_o), w_xo=_to_bf16(w_xo),
        ln1_g=ln1_g[:, None], ln1_b=ln1_b[:, None], ln2_g=ln2_g[:, None], ln2_b=ln2_b[:, None],
        ln3_g=ln3_g[:, None], ln3_b=ln3_b[:, None])
    w_xq_b, w_xk_b, w_xv_b = _to_bf16(w_xq), _to_bf16(w_xk), _to_bf16(w_xv)

    sgu_g3, sgu_b3 = sgu_ln_g[:, None], sgu_ln_b[:, None]
    b_s_t = jnp.swapaxes(b_s, 1, 2)

    kp_l, vp_l, mkp_l, mvp_l, ks_l, vs_l, vns_l = [], [], [], [], [], [], []
    for l in range(DEPTH):
        h_s, w_in_b = _matmul_casting(xsb, w_in, l, 1024)
        qkv_s, rest_s = h_s[:, :QKV_W], h_s[:, QKV_W:]
        qkv_s3 = qkv_s.reshape(ms, 1, QKV_W)
        a_s = _cached_moba(qkv_s3, cache_k_rows, cache_v, pt_flat, l, n_pool, n_pages)
        a_s = a_s.reshape(ms, Q_W).astype(BF16)
        w0 = jnp.repeat(w_s[l][:, 0, 0], SGU_GROUP_DIM)[None]
        b0 = jnp.repeat(b_s[l][:, 0], SGU_GROUP_DIM)[None]
        s_s, vn_s = _sgu_first_rows(rest_s, sgu_ln_g[l][None], sgu_ln_b[l][None], w0, b0)
        mix_s = _merge(a_s, s_s, rest_s, w["b_gate"], w["w_pa"], w["w_pb"], l, ms, 1024)
        x1f, x1b = _proj_ln(mix_s, w["w_o"], xsf, w["ln1_g"], w["ln1_b"], l, ms)
        (qx,) = _matmul(x1b, w_xq_b, l, 0, X_W, (F32,), ms, X_W)
        o_s = _xattn_single(qx.reshape(ms, 1, X_W), cache_mk_rows, cache_mv_rows, l * n_samples)
        x2f, x2b = _proj_ln(o_s.reshape(ms, X_W).astype(BF16), w["w_xo"], x1f, w["ln2_g"], w["ln2_b"], l, ms)
        xsf, xsb, w_up_b, w_down_b = _mlp_casting(x2b, x2f, w_up, w_down, w["ln3_g"], w["ln3_b"], l, _MLP_CAST_TF)
        ks_l.append(qkv_s[:, Q_W:Q_W + KV_W].reshape(ms, 1, N_KV_HEADS, HEAD_DIM))
        vs_l.append(qkv_s[:, Q_W + KV_W:].reshape(ms, 1, N_KV_HEADS, HEAD_DIM))
        vns_l.append(vn_s.reshape(ms, 1, SGU_WIDTH))

        mk_f, mk_b = _matmul(mem_b, w_xk_b, l, 0, X_W, (F32, BF16), 512, X_W)
        mv_f, mv_b = _matmul(mem_b, w_xv_b, l, 0, X_W, (F32, BF16), 512, X_W)
        (q_b,) = _matmul(xpb, w_in_b, 0, 0, Q_W, (BF16,), _PROJ_TM, Q_W)
        k_rows, k_b, kmean = _kv_proj(xpb, w_in_b, 0, Q_W, _PROJ_TM)
        v_rows, v_b, _ = _kv_proj(xpb, w_in_b, 0, Q_W + KV_W, _PROJ_TM)
        (rest,) = _matmul(xpb, w_in_b, 0, QKV_W, REST_W, (F32,), _PROJ_TM, 1024)
        vt = jnp.transpose(v_b.reshape(batch, n_blocks, MOBA_BLOCK, N_KV_HEADS, HEAD_DIM), (0, 3, 1, 4, 2))
        a = _moba_prompt(q_b, k_b, vt, kmean.reshape(batch, n_blocks, KV_W), batch, seq)
        x2f, x2b = _prompt_mid(a, rest, xpf, mk_b.reshape(batch, N_MEM, X_W), mv_b.reshape(batch, N_MEM, X_W),
                               sgu_g3, sgu_b3, w_s, b_s_t, w, w_xq_b, l, seq, _MID_TM)
        xpf, xpb = _mlp(x2b, x2f, w_up_b, w_down_b, 0, w["ln3_g"], w["ln3_b"], l, _MLP_TM, _MLP_TF)
        kp_l.append(k_rows.reshape(batch, seq // PAGE_SIZE, PAGE_SIZE, N_KV_HEADS, HEAD_DIM))
        vp_l.append(v_rows.reshape(batch, seq // PAGE_SIZE, PAGE_SIZE, N_KV_HEADS, HEAD_DIM))
        mkp_l.append(mk_f.reshape(batch, N_MEM, X_HEADS, X_HEAD_DIM))
        mvp_l.append(mv_f.reshape(batch, N_MEM, X_HEADS, X_HEAD_DIM))

    return (xpf.reshape(batch, seq, D_MODEL), xsf.reshape(ms, 1, D_MODEL),
            jnp.stack(kp_l), jnp.stack(vp_l), jnp.stack(mkp_l), jnp.stack(mvp_l),
            jnp.stack(ks_l), jnp.stack(vs_l), jnp.stack(vns_l))
```

```python
import functools

import jax
import jax.numpy as jnp
import numpy as np
from jax import lax
from jax.experimental import pallas as pl
from jax.experimental.pallas import tpu as pltpu

D_MODEL = 2048
DEPTH = 2
PAGE_SIZE = 128
N_HEADS = 8
N_KV_HEADS = 4
HEAD_DIM = 128
MOBA_BLOCK = 256
MOBA_TOPK = 3
SGU_WIDTH = 1024
SGU_GROUPS = 8
SGU_GROUP_DIM = SGU_WIDTH // SGU_GROUPS
SGU_CHUNK = 128
N_MEM = 256
X_HEADS = 4
X_HEAD_DIM = 128
D_FF = 4 * D_MODEL
DN_ALPHA = (2 * DEPTH) ** 0.25
LN_EPS = 1e-5
Q_W = N_HEADS * HEAD_DIM
KV_W = N_KV_HEADS * HEAD_DIM
X_W = X_HEADS * X_HEAD_DIM
QKV_W = Q_W + 2 * KV_W
REST_W = 2 * SGU_WIDTH + 2 * D_MODEL
PAGES_PER_BLOCK = MOBA_BLOCK // PAGE_SIZE
PAGE_ROWS = PAGE_SIZE * N_KV_HEADS
MASKED = -1e30

LANES = 128
SUBLANES = 8
VMEM_LIMIT = 56 * 1024 * 1024

BF16 = jnp.bfloat16
F32 = jnp.float32
_NT = (((1,), (1,)), ((), ()))


def _params(*sem):
    return pltpu.CompilerParams(dimension_semantics=sem, vmem_limit_bytes=VMEM_LIMIT)


def _dot(a, b):
    return jnp.dot(a, b, preferred_element_type=F32)


def _dot_nt(a, b):
    return lax.dot_general(a, b, _NT, preferred_element_type=F32)


def _gelu(x):
    c = np.float32(np.sqrt(2 / np.pi))
    return x * (0.5 * (1.0 + jnp.tanh(c * (x + 0.044715 * (x * x * x)))))


def _layer_norm(z, g, b):
    mu = jnp.mean(z, axis=-1, keepdims=True)
    d = z - mu
    var = jnp.mean(d * d, axis=-1, keepdims=True)
    return d * lax.rsqrt(var + LN_EPS) * g + b


def _top_blocks(gate, valid_f, idx_f, axis):
    sel = jnp.zeros(gate.shape, F32)
    for _ in range(MOBA_TOPK):
        m = jnp.max(gate, axis=axis, keepdims=True)
        first = jnp.min(jnp.where(gate == m, idx_f, float(gate.shape[axis])), axis=axis, keepdims=True)
        pick = idx_f == first
        sel = jnp.where(pick, valid_f, sel)
        gate = jnp.where(pick, -jnp.inf, gate)
    return sel


def _rows_to_sublanes(ref, n_rows, width, lead=0):
    sub = lax.broadcasted_iota(jnp.int32, (SUBLANES, width), 0)
    out = jnp.zeros((SUBLANES, width), F32)
    for r in range(n_rows):
        out = jnp.where(sub == r, ref[lead, :, r * width:(r + 1) * width].astype(F32), out)
    return out


def _mm_kernel(x_ref, w_ref, *out_refs):
    acc = _dot(x_ref[...], w_ref[...])
    for o in out_refs:
        o[...] = acc.astype(o.dtype)


def _cast_kernel(x_ref, o_ref):
    o_ref[...] = x_ref[...].astype(o_ref.dtype)


_CAST_BLOCK_BYTES = 4 * 1024 * 1024


def _to_bf16(w):
    d, k, n = w.shape
    rows = d * k
    tr = min(rows, max(SUBLANES, _CAST_BLOCK_BYTES // (4 * n)))
    assert rows % tr == 0
    out = pl.pallas_call(
        _cast_kernel,
        out_shape=jax.ShapeDtypeStruct((rows, n), BF16),
        grid=(rows // tr,),
        in_specs=[pl.BlockSpec((tr, n), lambda i: (i, 0))],
        out_specs=pl.BlockSpec((tr, n), lambda i: (i, 0)),
        compiler_params=_params("parallel"),
        name="to_bf16",
    )(w.reshape(rows, n))
    return out.reshape(d, k, n)


def _layer_spec(layer, block, index):
    return pl.BlockSpec((None,) + block, lambda *g: (layer,) + index(*g))


def _matmul(x, w, layer, col_off, ncols, out_dtypes, tm, tn):
    m, k = x.shape
    tm = min(tm, m)
    tn = min(tn, ncols)
    assert m % tm == 0 and ncols % tn == 0 and col_off % tn == 0
    joff = col_off // tn
    outs = pl.pallas_call(
        _mm_kernel,
        out_shape=[jax.ShapeDtypeStruct((m, ncols), dt) for dt in out_dtypes],
        grid=(ncols // tn, m // tm),
        in_specs=[pl.BlockSpec((tm, k), lambda j, i: (i, 0)),
                  _layer_spec(layer, (k, tn), lambda j, i: (0, j + joff))],
        out_specs=[pl.BlockSpec((tm, tn), lambda j, i: (i, j)) for _ in out_dtypes],
        compiler_params=_params("parallel", "parallel"),
        name="matmul",
    )(x, w)
    return outs


def _mm_cast_kernel(x_ref, w_ref, o_ref, wb_ref):
    wb = w_ref[...].astype(BF16)
    wb_ref[...] = wb
    o_ref[...] = _dot(x_ref[...], wb)


def _matmul_casting(x, w_f32, layer, tn):
    m, k = x.shape
    n = w_f32.shape[2]
    assert n % tn == 0
    return pl.pallas_call(
        _mm_cast_kernel,
        out_shape=[jax.ShapeDtypeStruct((m, n), F32), jax.ShapeDtypeStruct((1, k, n), BF16)],
        grid=(n // tn,),
        in_specs=[pl.BlockSpec((m, k), lambda j: (0, 0)),
                  _layer_spec(layer, (k, tn), lambda j: (0, j))],
        out_specs=[pl.BlockSpec((m, tn), lambda j: (0, j)),
                   pl.BlockSpec((None, k, tn), lambda j: (0, 0, j))],
        compiler_params=_params("parallel"),
        name="matmul_casting",
    )(x, w_f32)


def _kv_proj_kernel(x_ref, w_ref, rows_ref, ob_ref, km_ref):
    acc = _dot(x_ref[...], w_ref[...])
    tm = acc.shape[0]
    for h in range(N_KV_HEADS):
        rows_ref[pl.ds(h, tm, stride=N_KV_HEADS), :] = acc[:, h * HEAD_DIM:(h + 1) * HEAD_DIM]
    ob_ref[...] = acc.astype(ob_ref.dtype)
    for c in range(tm // MOBA_BLOCK):
        blk = acc[c * MOBA_BLOCK:(c + 1) * MOBA_BLOCK]
        km_ref[c] = jnp.sum(blk, axis=0, keepdims=True) * (1.0 / MOBA_BLOCK)


def _kv_proj(x, w, layer, col_off, tm):
    m, k = x.shape
    assert m % tm == 0 and tm % MOBA_BLOCK == 0 and col_off % KV_W == 0
    return pl.pallas_call(
        _kv_proj_kernel,
        out_shape=[jax.ShapeDtypeStruct((m * N_KV_HEADS, HEAD_DIM), F32),
                   jax.ShapeDtypeStruct((m, KV_W), BF16),
                   jax.ShapeDtypeStruct((m // MOBA_BLOCK, 1, KV_W), F32)],
        grid=(m // tm,),
        in_specs=[pl.BlockSpec((tm, k), lambda i: (i, 0)),
                  _layer_spec(layer, (k, KV_W), lambda i: (0, col_off // KV_W))],
        out_specs=[pl.BlockSpec((tm * N_KV_HEADS, HEAD_DIM), lambda i: (i, 0)),
                   pl.BlockSpec((tm, KV_W), lambda i: (i, 0)),
                   pl.BlockSpec((tm // MOBA_BLOCK, 1, KV_W), lambda i: (i, 0, 0))],
        compiler_params=_params("parallel"),
        name="kv_proj",
    )(x, w)


_MOBA_KV_PER_STEP = 4


def _moba_prompt_kernel(q_ref, k_ref, vt_ref, km_ref, o_ref, sel_ref, m_ref, l_ref, acc_ref):
    i = pl.program_id(2)
    rep = N_HEADS // N_KV_HEADS
    heads = _MOBA_KV_PER_STEP * rep
    blk = MOBA_BLOCK
    scale_log2e = np.float32(HEAD_DIM ** -0.5 * np.log2(np.e))

    def cols(c):
        return slice(c * HEAD_DIM, (c + 1) * HEAD_DIM)

    qs = [q_ref[:, cols(c)] for c in range(heads)]
    blk_id = lax.broadcasted_iota(jnp.int32, (km_ref.shape[0], blk), 0)
    valid = blk_id < i
    for c in range(heads):
        km = km_ref[:, cols(c // rep)].astype(BF16)
        gate = jnp.where(valid, _dot_nt(km, qs[c]), MASKED)
        sel_ref[c] = _top_blocks(gate, valid.astype(F32), blk_id.astype(F32), 0)

    def attend(j, masks, first):
        kjs = [k_ref[pl.ds(pl.multiple_of(j * blk, blk), blk), cols(g)] for g in range(_MOBA_KV_PER_STEP)]
        scores = [_dot_nt(kjs[c // rep], qs[c]) for c in range(heads)]
        ps, m_news, l_blks = [], [], []
        for c in range(heads):
            s = scores[c] * scale_log2e
            if first:
                s = jnp.where(masks[c], s, MASKED)
                m_new = jnp.max(s, axis=0, keepdims=True)
                p = jnp.exp2(s - m_new)
            else:
                m_blk = jnp.where(masks[c], jnp.max(s, axis=0, keepdims=True), MASKED)
                m_new = jnp.maximum(m_ref[c], m_blk)
                p = jnp.exp2(s - jnp.where(masks[c], m_new, -MASKED))
            l_blks.append(jnp.sum(p, axis=0, keepdims=True))
            ps.append(p.astype(BF16))
            m_news.append(m_new)
        pvs = [_dot(vt_ref[c // rep, j], ps[c]) for c in range(heads)]
        for c in range(heads):
            if first:
                l_ref[c] = l_blks[c]
                acc_ref[c] = pvs[c]
            else:
                a = jnp.exp2(m_ref[c] - m_news[c])
                l_ref[c] = a * l_ref[c] + l_blks[c]
                acc_ref[c] = a * acc_ref[c] + pvs[c]
            m_ref[c] = m_news[c]

    key = lax.broadcasted_iota(jnp.int32, (blk, blk), 0)
    qry = lax.broadcasted_iota(jnp.int32, (blk, blk), 1)
    attend(i, [key <= qry] * heads, True)

    def body(j, carry):
        attend(j, [sel_ref[c, pl.ds(j, 1), :] > 0.5 for c in range(heads)], False)
        return carry

    lax.fori_loop(0, i, body, 0)
    for c in range(heads):
        o_ref[:, cols(c)] = (acc_ref[c] / l_ref[c]).T.astype(o_ref.dtype)


def _moba_prompt(q_b, k_b, vt, kmean, batch, seq):
    nq = seq // MOBA_BLOCK
    g = _MOBA_KV_PER_STEP
    heads = g * (N_HEADS // N_KV_HEADS)
    assert N_KV_HEADS % g == 0
    return pl.pallas_call(
        _moba_prompt_kernel,
        out_shape=jax.ShapeDtypeStruct((batch * seq, Q_W), BF16),
        grid=(batch, N_KV_HEADS // g, nq),
        in_specs=[
            pl.BlockSpec((MOBA_BLOCK, heads * HEAD_DIM), lambda b, gg, i: (b * nq + i, gg)),
            pl.BlockSpec((seq, g * HEAD_DIM), lambda b, gg, i: (b, gg)),
            pl.BlockSpec((None, g, nq, HEAD_DIM, MOBA_BLOCK), lambda b, gg, i: (b, gg, 0, 0, 0)),
            pl.BlockSpec((None, nq, g * HEAD_DIM), lambda b, gg, i: (b, 0, gg)),
        ],
        out_specs=pl.BlockSpec((MOBA_BLOCK, heads * HEAD_DIM), lambda b, gg, i: (b * nq + i, gg)),
        scratch_shapes=[pltpu.VMEM((heads, nq, MOBA_BLOCK), F32),
                        pltpu.VMEM((heads, 1, MOBA_BLOCK), F32),
                        pltpu.VMEM((heads, 1, MOBA_BLOCK), F32),
                        pltpu.VMEM((heads, HEAD_DIM, MOBA_BLOCK), F32)],
        compiler_params=_params("parallel", "parallel", "arbitrary"),
        name="moba_prompt",
    )(q_b, k_b, vt, kmean)


def _cached_moba_kernel(pt_ref, qkv_ref, ck_hbm, cv_hbm, o_ref, kbuf, vbuf, km_ref, ksem, vsem,
                        *, layer, n_pool, n_pages):
    b = pl.program_id(0)
    n = pl.num_programs(0)
    slot = b % 2
    rep = N_HEADS // N_KV_HEADS
    n_blocks = n_pages // PAGES_PER_BLOCK
    scale = HEAD_DIM ** -0.5
    base = layer * n_pool

    def k_copy(sample, page_slot, sl):
        page = base + pt_ref[sample * n_pages + page_slot]
        return pltpu.make_async_copy(ck_hbm.at[page], kbuf.at[sl, page_slot], ksem.at[sl])

    def start_keys(sample, sl):
        for p in range(n_pages):
            k_copy(sample, p, sl).start(priority=1)

    @pl.when(b == 0)
    def _():
        km_ref[...] = jnp.zeros_like(km_ref)
        start_keys(0, 0)

    @pl.when(b + 1 < n)
    def _():
        start_keys(b + 1, 1 - slot)

    for p in range(n_pages):
        k_copy(b, p, slot).wait()

    for blk in range(n_blocks):
        tot = jnp.zeros((SUBLANES, HEAD_DIM), F32)
        for r in range(PAGES_PER_BLOCK):
            page = kbuf[slot, PAGES_PER_BLOCK * blk + r]
            tot = tot + jnp.sum(page.reshape(PAGE_ROWS // SUBLANES, SUBLANES, HEAD_DIM), axis=0)
        km_ref[blk * SUBLANES:(blk + 1) * SUBLANES, :] = tot + pltpu.roll(tot, N_KV_HEADS, axis=0)

    q8 = _rows_to_sublanes(qkv_ref, N_HEADS, HEAD_DIM)
    q8b = q8.astype(BF16)
    hrow = lax.broadcasted_iota(jnp.int32, (N_HEADS, LANES), 0)
    lane = lax.broadcasted_iota(jnp.int32, (N_HEADS, LANES), 1)
    gate = jnp.zeros((N_HEADS, LANES), F32)
    for kvh in range(N_KV_HEADS):
        km = km_ref[pl.ds(kvh, LANES, stride=SUBLANES), :] * (1.0 / MOBA_BLOCK)
        gate = jnp.where(hrow >= kvh * rep, _dot_nt(q8b, km.astype(BF16)), gate)
    gate = jnp.where(lane < n_blocks, gate, MASKED)
    lane_f = lane.astype(F32)
    picks = []
    for _ in range(MOBA_TOPK):
        m = jnp.max(gate, axis=-1, keepdims=True)
        first = jnp.min(jnp.where(gate == m, lane_f, float(LANES)), axis=-1, keepdims=True)
        picks.append(first.astype(jnp.int32))
        gate = jnp.where(lane_f == first, -jnp.inf, gate)
    blocks = [[picks[t][h, 0] for t in range(MOBA_TOPK)] for h in range(N_HEADS)]

    def v_copy(h, t, r):
        page = pt_ref[b * n_pages + blocks[h][t] * PAGES_PER_BLOCK + r]
        return pltpu.make_async_copy(cv_hbm.at[layer, page, :, h // rep, :],
                                     vbuf.at[(h * MOBA_TOPK + t) * PAGES_PER_BLOCK + r], vsem)

    sel = [(h, t, r) for h in range(N_HEADS) for t in range(MOBA_TOPK) for r in range(PAGES_PER_BLOCK)]
    for h, t, r in sel:
        v_copy(h, t, r).start()

    k_new = qkv_ref[0, :, Q_W:Q_W + KV_W]
    v_new = qkv_ref[0, :, Q_W + KV_W:]
    scores = {}
    for h, t, r in sel:
        kvh = h // rep
        kp = kbuf[slot, blocks[h][t] * PAGES_PER_BLOCK + r, pl.ds(kvh, PAGE_SIZE, stride=N_KV_HEADS), :]
        qh = jnp.broadcast_to(q8b[h:h + 1], (SUBLANES, HEAD_DIM))
        scores[h, t, r] = _dot_nt(qh, kp.astype(BF16))[0:1] * scale
    probs, p_new = {}, []
    for h in range(N_HEADS):
        kvh = h // rep
        kn = k_new[:, kvh * HEAD_DIM:(kvh + 1) * HEAD_DIM].astype(BF16).astype(F32)
        s_new = jnp.sum(q8b[h:h + 1].astype(F32) * kn, axis=-1, keepdims=True) * scale
        mine = [scores[h, t, r] for t in range(MOBA_TOPK) for r in range(PAGES_PER_BLOCK)]
        m = s_new
        for s in mine:
            m = jnp.maximum(m, jnp.max(s, axis=-1, keepdims=True))
        es = [jnp.exp(s - m) for s in mine]
        e_new = jnp.exp(s_new - m)
        denom = e_new
        for e in es:
            denom = denom + jnp.sum(e, axis=-1, keepdims=True)
        p_new.append((e_new / denom).astype(BF16).astype(F32))
        for idx, (t, r) in enumerate((t, r) for t in range(MOBA_TOPK) for r in range(PAGES_PER_BLOCK)):
            probs[h, t, r] = (es[idx] / denom).astype(BF16)

    for h, t, r in sel:
        v_copy(h, t, r).wait()

    for h in range(N_HEADS):
        kvh = h // rep
        out = p_new[h] * v_new[:, kvh * HEAD_DIM:(kvh + 1) * HEAD_DIM].astype(BF16).astype(F32)
        for t in range(MOBA_TOPK):
            for r in range(PAGES_PER_BLOCK):
                p8 = jnp.broadcast_to(probs[h, t, r], (SUBLANES, PAGE_SIZE))
                vp = vbuf[(h * MOBA_TOPK + t) * PAGES_PER_BLOCK + r]
                out = out + _dot(p8, vp.astype(BF16))[0:1]
        o_ref[0, :, h * HEAD_DIM:(h + 1) * HEAD_DIM] = out


def _cached_moba(qkv_f3, cache_k_rows, cache_v, pt_flat, layer, n_pool, n_pages):
    n = qkv_f3.shape[0]
    assert SUBLANES == 2 * N_KV_HEADS and n_pages // PAGES_PER_BLOCK <= LANES
    return pl.pallas_call(
        functools.partial(_cached_moba_kernel, layer=layer, n_pool=n_pool, n_pages=n_pages),
        out_shape=jax.ShapeDtypeStruct((n, 1, Q_W), F32),
        grid_spec=pltpu.PrefetchScalarGridSpec(
            num_scalar_prefetch=1,
            grid=(n,),
            in_specs=[pl.BlockSpec((1, 1, QKV_W), lambda b, pt: (b, 0, 0)),
                      pl.BlockSpec(memory_space=pl.ANY),
                      pl.BlockSpec(memory_space=pl.ANY)],
            out_specs=pl.BlockSpec((1, 1, Q_W), lambda b, pt: (b, 0, 0)),
            scratch_shapes=[pltpu.VMEM((2, n_pages, PAGE_ROWS, HEAD_DIM), F32),
                            pltpu.VMEM((N_HEADS * MOBA_TOPK * PAGES_PER_BLOCK, PAGE_SIZE, HEAD_DIM), F32),
                            pltpu.VMEM((LANES * SUBLANES, HEAD_DIM), F32),
                            pltpu.SemaphoreType.DMA((2,)),
                            pltpu.SemaphoreType.DMA(())],
        ),
        compiler_params=_params("arbitrary"),
        name="cached_moba",
    )(pt_flat, qkv_f3, cache_k_rows, cache_v)


def _proj_act_kernel(x_ref, w_ref, *refs, act):
    o_ref = refs[-1]
    acc = _dot(x_ref[...], w_ref[...])
    if act == "gelu":
        o_ref[...] = _gelu(acc)
    elif act == "gelu_ln":
        o_ref[...] = _layer_norm(_gelu(acc), refs[0][...], refs[1][...])
    else:
        o_ref[...] = jax.nn.sigmoid(acc + refs[0][...])


def _proj_act(x, w, w_layer, col_off, ncols, act, vecs, layer, tm, tn):
    m, k = x.shape
    assert m % tm == 0 and ncols % tn == 0 and col_off % tn == 0
    assert act != "gelu_ln" or tn == ncols
    joff = col_off // tn
    return pl.pallas_call(
        functools.partial(_proj_act_kernel, act=act),
        out_shape=jax.ShapeDtypeStruct((m, ncols), F32),
        grid=(ncols // tn, m // tm),
        in_specs=[pl.BlockSpec((tm, k), lambda j, i: (i, 0)),
                  _layer_spec(w_layer, (k, tn), lambda j, i: (0, j + joff))]
        + [_layer_spec(layer, (1, tn), lambda j, i: (0, j)) for _ in vecs],
        out_specs=pl.BlockSpec((tm, tn), lambda j, i: (i, j)),
        compiler_params=_params("parallel", "parallel"),
        name="proj_" + act,
    )(x, w, *vecs)


def _sgu_mix_kernel(gu_ref, vn_ref, ws_ref, bs_ref, s_ref):
    t = SGU_CHUNK
    row = lax.broadcasted_iota(jnp.int32, (t, t), 0)
    col = lax.broadcasted_iota(jnp.int32, (t, t), 1)
    for g in range(SGU_GROUPS):
        cs = slice(g * SGU_GROUP_DIM, (g + 1) * SGU_GROUP_DIM)
        ws = jnp.where(col <= row, ws_ref[g], 0.0).astype(BF16)
        bias = bs_ref[:, g:g + 1]
        for c in range(gu_ref.shape[0] // t):
            rs = slice(c * t, (c + 1) * t)
            mixed = _dot(ws, vn_ref[rs, cs].astype(BF16)) + bias
            s_ref[rs, cs] = (gu_ref[rs, cs] * mixed).astype(s_ref.dtype)


def _prompt_mid_kernel(a_ref, gu_ref, vn_ref, ga_ref, gb_ref, x_ref, mk_ref, mv_ref,
                       ws_ref, bs_ref, wpa_ref, wpb_ref, wo_ref, g1_ref, b1_ref,
                       wxq_ref, wxo_ref, g2_ref, b2_ref, of_ref, ob_ref, s_scr, o_scr):
    _sgu_mix_kernel(gu_ref, vn_ref, ws_ref, bs_ref, s_scr)
    mix = ga_ref[...] * _dot(a_ref[...], wpa_ref[...]) + gb_ref[...] * _dot(s_scr[...], wpb_ref[...])
    x1 = _layer_norm(DN_ALPHA * x_ref[...] + _dot(mix.astype(BF16), wo_ref[...]), g1_ref[...], b1_ref[...])
    qx = _dot(x1.astype(BF16), wxq_ref[...]).astype(BF16)
    scale = X_HEAD_DIM ** -0.5
    for h in range(X_HEADS):
        cs = slice(h * X_HEAD_DIM, (h + 1) * X_HEAD_DIM)
        s = _dot_nt(qx[:, cs], mk_ref[:, cs]) * scale
        e = jnp.exp(s - jnp.max(s, axis=-1, keepdims=True))
        p = (e / jnp.sum(e, axis=-1, keepdims=True)).astype(BF16)
        o_scr[:, cs] = _dot(p, mv_ref[:, cs]).astype(o_scr.dtype)
    y = _layer_norm(DN_ALPHA * x1 + _dot(o_scr[...], wxo_ref[...]), g2_ref[...], b2_ref[...])
    of_ref[...] = y
    ob_ref[...] = y.astype(ob_ref.dtype)


def _prompt_mid(a, gu, vn, gates, x, mk, mv, w_s, b_s_t, w, w_xq, layer, seq, tm):
    m = a.shape[0]
    assert m % tm == 0 and seq % tm == 0 and tm % SGU_CHUNK == 0
    tiles_per_batch = seq // tm

    def const(block):
        return pl.BlockSpec((None,) + block, lambda i: (layer,) + (0,) * len(block), pipeline_mode=pl.Buffered(1))

    mem = pl.BlockSpec((None, N_MEM, X_W), lambda i: (i // tiles_per_batch, 0, 0))
    vec_d = const((1, D_MODEL))
    return pl.pallas_call(
        _prompt_mid_kernel,
        out_shape=[jax.ShapeDtypeStruct((m, D_MODEL), F32), jax.ShapeDtypeStruct((m, D_MODEL), BF16)],
        grid=(m // tm,),
        in_specs=[pl.BlockSpec((tm, Q_W), lambda i: (i, 0)),
                  pl.BlockSpec((tm, SGU_WIDTH), lambda i: (i, 0)),
                  pl.BlockSpec((tm, SGU_WIDTH), lambda i: (i, 0)),
                  pl.BlockSpec((tm, D_MODEL), lambda i: (i, 0)),
                  pl.BlockSpec((tm, D_MODEL), lambda i: (i, 1)),
                  pl.BlockSpec((tm, D_MODEL), lambda i: (i, 0)),
                  mem, mem,
                  const((SGU_GROUPS, SGU_CHUNK, SGU_CHUNK)), const((SGU_CHUNK, SGU_GROUPS)),
                  const((Q_W, D_MODEL)), const((SGU_WIDTH, D_MODEL)),
                  const((D_MODEL, D_MODEL)), vec_d, vec_d,
                  const((D_MODEL, X_W)), const((X_W, D_MODEL)), vec_d, vec_d],
        out_specs=[pl.BlockSpec((tm, D_MODEL), lambda i: (i, 0)), pl.BlockSpec((tm, D_MODEL), lambda i: (i, 0))],
        scratch_shapes=[pltpu.VMEM((tm, SGU_WIDTH), BF16), pltpu.VMEM((tm, X_W), BF16)],
        compiler_params=_params("parallel"),
        name="prompt_mid",
    )(a, gu, vn, gates, gates, x, mk, mv, w_s, b_s_t,
      w["w_pa"], w["w_pb"], w["w_o"], w["ln1_g"], w["ln1_b"],
      w_xq, w["w_xo"], w["ln2_g"], w["ln2_b"])


def _sgu_first_row_kernel(zu_ref, zv_ref, g_ref, b_ref, w0_ref, b0_ref, s_ref, vn_ref):
    vn = _layer_norm(_gelu(zv_ref[...]), g_ref[...], b_ref[...])
    vn_ref[...] = vn
    s_ref[...] = (_gelu(zu_ref[...]) * (vn * w0_ref[...] + b0_ref[...])).astype(s_ref.dtype)


def _sgu_first_rows(rest, ln_g, ln_b, w0, b0):
    m = rest.shape[0]
    vec = pl.BlockSpec((1, SGU_WIDTH), lambda i: (0, 0))
    return pl.pallas_call(
        _sgu_first_row_kernel,
        out_shape=[jax.ShapeDtypeStruct((m, SGU_WIDTH), BF16), jax.ShapeDtypeStruct((m, SGU_WIDTH), F32)],
        grid=(1,),
        in_specs=[pl.BlockSpec((m, SGU_WIDTH), lambda i: (0, 0)),
                  pl.BlockSpec((m, SGU_WIDTH), lambda i: (0, 1)), vec, vec, vec, vec],
        out_specs=[pl.BlockSpec((m, SGU_WIDTH), lambda i: (0, 0)), pl.BlockSpec((m, SGU_WIDTH), lambda i: (0, 0))],
        compiler_params=_params("arbitrary"),
        name="sgu_first_rows",
    )(rest, rest, ln_g, ln_b, w0, b0)


def _merge_kernel(a_ref, s_ref, ga_ref, gb_ref, bg_ref, wa_ref, wb_ref, o_ref):
    g_a = jax.nn.sigmoid(ga_ref[...] + bg_ref[0:1, :])
    g_b = jax.nn.sigmoid(gb_ref[...] + bg_ref[1:2, :])
    mix = g_a * _dot(a_ref[...], wa_ref[...]) + g_b * _dot(s_ref[...], wb_ref[...])
    o_ref[...] = mix.astype(o_ref.dtype)


def _merge(a, s, rest, b_gate, w_pa, w_pb, layer, tm, tn):
    m = a.shape[0]
    tm = min(tm, m)
    ga0 = 2 * SGU_WIDTH // tn
    gb0 = (2 * SGU_WIDTH + D_MODEL) // tn
    return pl.pallas_call(
        _merge_kernel,
        out_shape=jax.ShapeDtypeStruct((m, D_MODEL), BF16),
        grid=(D_MODEL // tn, m // tm),
        in_specs=[pl.BlockSpec((tm, Q_W), lambda j, i: (i, 0)),
                  pl.BlockSpec((tm, SGU_WIDTH), lambda j, i: (i, 0)),
                  pl.BlockSpec((tm, tn), lambda j, i: (i, ga0 + j)),
                  pl.BlockSpec((tm, tn), lambda j, i: (i, gb0 + j)),
                  _layer_spec(layer, (2, tn), lambda j, i: (0, j)),
                  _layer_spec(layer, (Q_W, tn), lambda j, i: (0, j)),
                  _layer_spec(layer, (SGU_WIDTH, tn), lambda j, i: (0, j))],
        out_specs=pl.BlockSpec((tm, tn), lambda j, i: (i, j)),
        compiler_params=_params("parallel", "parallel"),
        name="merge",
    )(a, s, rest, rest, b_gate, w_pa, w_pb)


def _proj_ln_kernel(a_ref, w_ref, x_ref, g_ref, b_ref, of_ref, ob_ref):
    z = DN_ALPHA * x_ref[...] + _dot(a_ref[...], w_ref[...])
    y = _layer_norm(z, g_ref[...], b_ref[...])
    of_ref[...] = y
    ob_ref[...] = y.astype(ob_ref.dtype)


def _proj_ln(a, w, x, g, b, layer, tm):
    m, k = a.shape
    tm = min(tm, m)
    vec = _layer_spec(layer, (1, D_MODEL), lambda i: (0, 0))
    return pl.pallas_call(
        _proj_ln_kernel,
        out_shape=[jax.ShapeDtypeStruct((m, D_MODEL), F32), jax.ShapeDtypeStruct((m, D_MODEL), BF16)],
        grid=(m // tm,),
        in_specs=[pl.BlockSpec((tm, k), lambda i: (i, 0)),
                  _layer_spec(layer, (k, D_MODEL), lambda i: (0, 0)),
                  pl.BlockSpec((tm, D_MODEL), lambda i: (i, 0)), vec, vec],
        out_specs=[pl.BlockSpec((tm, D_MODEL), lambda i: (i, 0)), pl.BlockSpec((tm, D_MODEL), lambda i: (i, 0))],
        compiler_params=_params("parallel"),
        name="proj_ln",
    )(a, w, x, g, b)


_XATTN_SAMPLES_PER_STEP = 4


def _xattn_single_kernel(q_ref, mk_ref, mv_ref, o_ref):
    scale = X_HEAD_DIM ** -0.5
    rows = N_MEM * X_HEADS
    head = lax.broadcasted_iota(jnp.int32, (SUBLANES, rows), 0)
    row_head = jnp.bitwise_and(lax.broadcasted_iota(jnp.int32, (SUBLANES, rows), 1), X_HEADS - 1)
    for b in range(q_ref.shape[0]):
        q8 = _rows_to_sublanes(q_ref, X_HEADS, X_HEAD_DIM, b).astype(BF16)
        s = jnp.where(row_head == head, _dot_nt(q8, mk_ref[b].astype(BF16)) * scale, MASKED)
        e = jnp.exp(s - jnp.max(s, axis=-1, keepdims=True))
        p = (e / jnp.sum(e, axis=-1, keepdims=True)).astype(BF16)
        o = _dot(p, mv_ref[b].astype(BF16))
        for h in range(X_HEADS):
            o_ref[b, :, h * X_HEAD_DIM:(h + 1) * X_HEAD_DIM] = o[h:h + 1].astype(o_ref.dtype)


def _xattn_single(q3, mk_rows, mv_rows, mem_off):
    n = q3.shape[0]
    rows = N_MEM * X_HEADS
    per = _XATTN_SAMPLES_PER_STEP
    assert n % per == 0 and mem_off % per == 0
    return pl.pallas_call(
        _xattn_single_kernel,
        out_shape=jax.ShapeDtypeStruct((n, 1, X_W), F32),
        grid=(n // per,),
        in_specs=[pl.BlockSpec((per, 1, X_W), lambda b: (b, 0, 0)),
                  pl.BlockSpec((per, rows, X_HEAD_DIM), lambda b: (mem_off // per + b, 0, 0)),
                  pl.BlockSpec((per, rows, X_HEAD_DIM), lambda b: (mem_off // per + b, 0, 0))],
        out_specs=pl.BlockSpec((per, 1, X_W), lambda b: (b, 0, 0)),
        compiler_params=_params("parallel"),
        name="xattn_single",
    )(q3, mk_rows, mv_rows)


def _mlp_kernel(xb_ref, xf_ref, wu_ref, wd_ref, g_ref, b_ref, of_ref, ob_ref):
    f = pl.program_id(1)

    @pl.when(f == 0)
    def _():
        of_ref[...] = jnp.zeros_like(of_ref)

    h = jnp.maximum(_dot(xb_ref[...], wu_ref[...]), 0.0)
    of_ref[...] += _dot((h * h).astype(BF16), wd_ref[...])

    @pl.when(f == pl.num_programs(1) - 1)
    def _():
        y = _layer_norm(DN_ALPHA * xf_ref[...] + of_ref[...], g_ref[...], b_ref[...])
        of_ref[...] = y
        ob_ref[...] = y.astype(ob_ref.dtype)


def _mlp(xb, xf, w_up, w_down, w_layer, g, b, layer, tm, tf):
    m = xb.shape[0]
    tm = min(tm, m)
    vec = _layer_spec(layer, (1, D_MODEL), lambda i, f: (0, 0))
    return pl.pallas_call(
        _mlp_kernel,
        out_shape=[jax.ShapeDtypeStruct((m, D_MODEL), F32), jax.ShapeDtypeStruct((m, D_MODEL), BF16)],
        grid=(m // tm, D_FF // tf),
        in_specs=[pl.BlockSpec((tm, D_MODEL), lambda i, f: (i, 0)),
                  pl.BlockSpec((tm, D_MODEL), lambda i, f: (i, 0)),
                  _layer_spec(w_layer, (D_MODEL, tf), lambda i, f: (0, f)),
                  _layer_spec(w_layer, (tf, D_MODEL), lambda i, f: (f, 0)), vec, vec],
        out_specs=[pl.BlockSpec((tm, D_MODEL), lambda i, f: (i, 0)),
                   pl.BlockSpec((tm, D_MODEL), lambda i, f: (i, 0))],
        compiler_params=_params("parallel", "arbitrary"),
        name="mlp",
    )(xb, xf, w_up, w_down, g, b)


def _mlp_cast_kernel(xb_ref, xf_ref, wu_ref, wd_ref, g_ref, b_ref, of_ref, ob_ref, wub_ref, wdb_ref):
    wub_ref[...] = wu_ref[...].astype(BF16)
    wdb_ref[...] = wd_ref[...].astype(BF16)
    _mlp_kernel(xb_ref, xf_ref, wub_ref, wdb_ref, g_ref, b_ref, of_ref, ob_ref)


def _mlp_casting(xb, xf, w_up_f32, w_down_f32, g, b, layer, tf):
    m = xb.shape[0]
    vec = _layer_spec(layer, (1, D_MODEL), lambda i, f: (0, 0))
    rows = pl.BlockSpec((m, D_MODEL), lambda i, f: (0, 0))
    return pl.pallas_call(
        _mlp_cast_kernel,
        out_shape=[jax.ShapeDtypeStruct((m, D_MODEL), F32), jax.ShapeDtypeStruct((m, D_MODEL), BF16),
                   jax.ShapeDtypeStruct((1, D_MODEL, D_FF), BF16), jax.ShapeDtypeStruct((1, D_FF, D_MODEL), BF16)],
        grid=(1, D_FF // tf),
        in_specs=[rows, rows,
                  _layer_spec(layer, (D_MODEL, tf), lambda i, f: (0, f)),
                  _layer_spec(layer, (tf, D_MODEL), lambda i, f: (f, 0)), vec, vec],
        out_specs=[rows, rows,
                   pl.BlockSpec((None, D_MODEL, tf), lambda i, f: (0, 0, f)),
                   pl.BlockSpec((None, tf, D_MODEL), lambda i, f: (0, f, 0))],
        compiler_params=_params("parallel", "arbitrary"),
        name="mlp_casting",
    )(xb, xf, w_up_f32, w_down_f32, g, b)


_MLP_TM = 512
_MLP_TF = 1024
_MLP_CAST_TF = 512
_MID_TM = 256
_PROJ_TM = 1024


def kernel(x_prompt, x_sample, mem_prompt, cache_k, cache_v, cache_mem_k, cache_mem_v, page_table,
           w_in, b_gate, sgu_ln_g, sgu_ln_b, w_s, b_s, w_pa, w_pb, w_o, ln1_g, ln1_b,
           w_xq, w_xk, w_xv, w_xo, ln2_g, ln2_b, w_up, w_down, ln3_g, ln3_b):
    batch, seq, _ = x_prompt.shape
    n_samples, dec_seq, _ = x_sample.shape
    assert dec_seq == 1 and seq % MOBA_BLOCK == 0
    assert N_KV_HEADS & (N_KV_HEADS - 1) == 0 and X_HEADS & (X_HEADS - 1) == 0 and X_HEADS <= SUBLANES
    n_pool = cache_k.shape[1]
    n_pages = page_table.shape[1]
    n_blocks = seq // MOBA_BLOCK
    assert n_pages % PAGES_PER_BLOCK == 0
    mp, ms = batch * seq, n_samples

    pt_flat = page_table.reshape(-1).astype(jnp.int32)
    cache_k_rows = cache_k.reshape(DEPTH * n_pool, PAGE_ROWS, HEAD_DIM)
    cache_mk_rows = cache_mem_k.reshape(DEPTH * n_samples, N_MEM * X_HEADS, X_HEAD_DIM)
    cache_mv_rows = cache_mem_v.reshape(DEPTH * n_samples, N_MEM * X_HEADS, X_HEAD_DIM)
    mem_b = mem_prompt.reshape(batch * N_MEM, D_MODEL).astype(BF16)

    xpf = x_prompt.reshape(mp, D_MODEL)
    xsf = x_sample.reshape(ms, D_MODEL)
    xpb, xsb = xpf.astype(BF16), xsf.astype(BF16)

    w = dict(
        b_gate=b_gate, w_pa=_to_bf16(w_pa), w_pb=_to_bf16(w_pb), w_o=_to_bf16(w_o), w_xo=_to_bf16(w_xo),
        ln1_g=ln1_g[:, None], ln1_b=ln1_b[:, None], ln2_g=ln2_g[:, None], ln2_b=ln2_b[:, None],
        ln3_g=ln3_g[:, None], ln3_b=ln3_b[:, None])
    w_xq_b, w_xk_b, w_xv_b = _to_bf16(w_xq), _to_bf16(w_xk), _to_bf16(w_xv)

    sgu_g3, sgu_b3 = sgu_ln_g[:, None], sgu_ln_b[:, None]
    b_gate_row = b_gate.reshape(DEPTH, 1, 2 * D_MODEL)
    b_s_t = jnp.swapaxes(b_s, 1, 2)

    kp_l, vp_l, mkp_l, mvp_l, ks_l, vs_l, vns_l = [], [], [], [], [], [], []
    for l in range(DEPTH):
        h_s, w_in_b = _matmul_casting(xsb, w_in, l, 1024)
        qkv_s, rest_s = h_s[:, :QKV_W], h_s[:, QKV_W:]
        qkv_s3 = qkv_s.reshape(ms, 1, QKV_W)
        a_s = _cached_moba(qkv_s3, cache_k_rows, cache_v, pt_flat, l, n_pool, n_pages)
        a_s = a_s.reshape(ms, Q_W).astype(BF16)
        w0 = jnp.repeat(w_s[l][:, 0, 0], SGU_GROUP_DIM)[None]
        b0 = jnp.repeat(b_s[l][:, 0], SGU_GROUP_DIM)[None]
        s_s, vn_s = _sgu_first_rows(rest_s, sgu_ln_g[l][None], sgu_ln_b[l][None], w0, b0)
        mix_s = _merge(a_s, s_s, rest_s, w["b_gate"], w["w_pa"], w["w_pb"], l, ms, 1024)
        x1f, x1b = _proj_ln(mix_s, w["w_o"], xsf, w["ln1_g"], w["ln1_b"], l, ms)
        (qx,) = _matmul(x1b, w_xq_b, l, 0, X_W, (F32,), ms, X_W)
        o_s = _xattn_single(qx.reshape(ms, 1, X_W), cache_mk_rows, cache_mv_rows, l * n_samples)
        x2f, x2b = _proj_ln(o_s.reshape(ms, X_W).astype(BF16), w["w_xo"], x1f, w["ln2_g"], w["ln2_b"], l, ms)
        xsf, xsb, w_up_b, w_down_b = _mlp_casting(x2b, x2f, w_up, w_down, w["ln3_g"], w["ln3_b"], l, _MLP_CAST_TF)
        ks_l.append(qkv_s[:, Q_W:Q_W + KV_W].reshape(ms, 1, N_KV_HEADS, HEAD_DIM))
        vs_l.append(qkv_s[:, Q_W + KV_W:].reshape(ms, 1, N_KV_HEADS, HEAD_DIM))
        vns_l.append(vn_s.reshape(ms, 1, SGU_WIDTH))

        mk_f, mk_b = _matmul(mem_b, w_xk_b, l, 0, X_W, (F32, BF16), 512, X_W)
        mv_f, mv_b = _matmul(mem_b, w_xv_b, l, 0, X_W, (F32, BF16), 512, X_W)
        (q_b,) = _matmul(xpb, w_in_b, 0, 0, Q_W, (BF16,), _PROJ_TM, Q_W)
        k_rows, k_b, kmean = _kv_proj(xpb, w_in_b, 0, Q_W, _PROJ_TM)
        v_rows, v_b, _ = _kv_proj(xpb, w_in_b, 0, Q_W + KV_W, _PROJ_TM)
        gu = _proj_act(xpb, w_in_b, 0, QKV_W, SGU_WIDTH, "gelu", (), l, _PROJ_TM, SGU_WIDTH)
        vn = _proj_act(xpb, w_in_b, 0, QKV_W + SGU_WIDTH, SGU_WIDTH, "gelu_ln", (sgu_g3, sgu_b3), l,
                       _PROJ_TM, SGU_WIDTH)
        gates = _proj_act(xpb, w_in_b, 0, QKV_W + 2 * SGU_WIDTH, 2 * D_MODEL, "gate", (b_gate_row,), l,
                          _PROJ_TM, 1024)
        vt = jnp.transpose(v_b.reshape(batch, n_blocks, MOBA_BLOCK, N_KV_HEADS, HEAD_DIM), (0, 3, 1, 4, 2))
        a = _moba_prompt(q_b, k_b, vt, kmean.reshape(batch, n_blocks, KV_W), batch, seq)
        x2f, x2b = _prompt_mid(a, gu, vn, gates, xpf, mk_b.reshape(batch, N_MEM, X_W),
                               mv_b.reshape(batch, N_MEM, X_W), w_s, b_s_t, w, w_xq_b, l, seq, _MID_TM)
        xpf, xpb = _mlp(x2b, x2f, w_up_b, w_down_b, 0, w["ln3_g"], w["ln3_b"], l, _MLP_TM, _MLP_TF)
        kp_l.append(k_rows.reshape(batch, seq // PAGE_SIZE, PAGE_SIZE, N_KV_HEADS, HEAD_DIM))
        vp_l.append(v_rows.reshape(batch, seq // PAGE_SIZE, PAGE_SIZE, N_KV_HEADS, HEAD_DIM))
        mkp_l.append(mk_f.reshape(batch, N_MEM, X_HEADS, X_HEAD_DIM))
        mvp_l.append(mv_f.reshape(batch, N_MEM, X_HEADS, X_HEAD_DIM))

    return (xpf.reshape(batch, seq, D_MODEL), xsf.reshape(ms, 1, D_MODEL),
            jnp.stack(kp_l), jnp.stack(vp_l), jnp.stack(mkp_l), jnp.stack(mvp_l),
            jnp.stack(ks_l), jnp.stack(vs_l), jnp.stack(vns_l))
```

```python
import functools

import jax
import jax.numpy as jnp
import numpy as np
from jax import lax
from jax.experimental import pallas as pl
from jax.experimental.pallas import tpu as pltpu

D_MODEL = 2048
DEPTH = 2
PAGE_SIZE = 128
N_HEADS = 8
N_KV_HEADS = 4
HEAD_DIM = 128
MOBA_BLOCK = 256
MOBA_TOPK = 3
SGU_WIDTH = 1024
SGU_GROUPS = 8
SGU_GROUP_DIM = SGU_WIDTH // SGU_GROUPS
SGU_CHUNK = 128
N_MEM = 256
X_HEADS = 4
X_HEAD_DIM = 128
D_FF = 4 * D_MODEL
DN_ALPHA = (2 * DEPTH) ** 0.25
LN_EPS = 1e-5
Q_W = N_HEADS * HEAD_DIM
KV_W = N_KV_HEADS * HEAD_DIM
X_W = X_HEADS * X_HEAD_DIM
QKV_W = Q_W + 2 * KV_W
REST_W = 2 * SGU_WIDTH + 2 * D_MODEL
PAGES_PER_BLOCK = MOBA_BLOCK // PAGE_SIZE
PAGE_ROWS = PAGE_SIZE * N_KV_HEADS
MASKED = -1e30

LANES = 128
SUBLANES = 8
VMEM_LIMIT = 56 * 1024 * 1024

BF16 = jnp.bfloat16
F32 = jnp.float32
_NT = (((1,), (1,)), ((), ()))


def _params(*sem):
    return pltpu.CompilerParams(dimension_semantics=sem, vmem_limit_bytes=VMEM_LIMIT)


def _dot(a, b):
    return jnp.dot(a, b, preferred_element_type=F32)


def _dot_nt(a, b):
    return lax.dot_general(a, b, _NT, preferred_element_type=F32)


def _gelu(x):
    c = np.float32(np.sqrt(2 / np.pi))
    return x * (0.5 * (1.0 + jnp.tanh(c * (x + 0.044715 * (x * x * x)))))


def _layer_norm(z, g, b):
    mu = jnp.mean(z, axis=-1, keepdims=True)
    d = z - mu
    var = jnp.mean(d * d, axis=-1, keepdims=True)
    return d * lax.rsqrt(var + LN_EPS) * g + b


def _top_blocks(gate, valid_f, idx_f, axis):
    sel = jnp.zeros(gate.shape, F32)
    for _ in range(MOBA_TOPK):
        m = jnp.max(gate, axis=axis, keepdims=True)
        first = jnp.min(jnp.where(gate == m, idx_f, float(gate.shape[axis])), axis=axis, keepdims=True)
        pick = idx_f == first
        sel = jnp.where(pick, valid_f, sel)
        gate = jnp.where(pick, -jnp.inf, gate)
    return sel


def _rows_to_sublanes(ref, n_rows, width, lead=0):
    sub = lax.broadcasted_iota(jnp.int32, (SUBLANES, width), 0)
    out = jnp.zeros((SUBLANES, width), F32)
    for r in range(n_rows):
        out = jnp.where(sub == r, ref[lead, :, r * width:(r + 1) * width].astype(F32), out)
    return out


def _mm_kernel(x_ref, w_ref, *out_refs):
    acc = _dot(x_ref[...], w_ref[...])
    for o in out_refs:
        o[...] = acc.astype(o.dtype)


def _cast_kernel(x_ref, o_ref):
    o_ref[...] = x_ref[...].astype(o_ref.dtype)


_CAST_BLOCK_BYTES = 4 * 1024 * 1024


def _to_bf16(w):
    d, k, n = w.shape
    rows = d * k
    tr = min(rows, max(SUBLANES, _CAST_BLOCK_BYTES // (4 * n)))
    assert rows % tr == 0
    out = pl.pallas_call(
        _cast_kernel,
        out_shape=jax.ShapeDtypeStruct((rows, n), BF16),
        grid=(rows // tr,),
        in_specs=[pl.BlockSpec((tr, n), lambda i: (i, 0))],
        out_specs=pl.BlockSpec((tr, n), lambda i: (i, 0)),
        compiler_params=_params("parallel"),
        name="to_bf16",
    )(w.reshape(rows, n))
    return out.reshape(d, k, n)


def _layer_spec(layer, block, index):
    return pl.BlockSpec((None,) + block, lambda *g: (layer,) + index(*g))


def _matmul(x, w, layer, col_off, ncols, out_dtypes, tm, tn):
    m, k = x.shape
    tm = min(tm, m)
    tn = min(tn, ncols)
    assert m % tm == 0 and ncols % tn == 0 and col_off % tn == 0
    joff = col_off // tn
    outs = pl.pallas_call(
        _mm_kernel,
        out_shape=[jax.ShapeDtypeStruct((m, ncols), dt) for dt in out_dtypes],
        grid=(ncols // tn, m // tm),
        in_specs=[pl.BlockSpec((tm, k), lambda j, i: (i, 0)),
                  _layer_spec(layer, (k, tn), lambda j, i: (0, j + joff))],
        out_specs=[pl.BlockSpec((tm, tn), lambda j, i: (i, j)) for _ in out_dtypes],
        compiler_params=_params("parallel", "parallel"),
        name="matmul",
    )(x, w)
    return outs


def _mm_cast_kernel(x_ref, w_ref, o_ref, wb_ref):
    wb = w_ref[...].astype(BF16)
    wb_ref[...] = wb
    o_ref[...] = _dot(x_ref[...], wb)


def _matmul_casting(x, w_f32, layer, tn):
    m, k = x.shape
    n = w_f32.shape[2]
    assert n % tn == 0
    return pl.pallas_call(
        _mm_cast_kernel,
        out_shape=[jax.ShapeDtypeStruct((m, n), F32), jax.ShapeDtypeStruct((1, k, n), BF16)],
        grid=(n // tn,),
        in_specs=[pl.BlockSpec((m, k), lambda j: (0, 0)),
                  _layer_spec(layer, (k, tn), lambda j: (0, j))],
        out_specs=[pl.BlockSpec((m, tn), lambda j: (0, j)),
                   pl.BlockSpec((None, k, tn), lambda j: (0, 0, j))],
        compiler_params=_params("parallel"),
        name="matmul_casting",
    )(x, w_f32)


def _kv_proj_kernel(x_ref, w_ref, rows_ref, ob_ref, km_ref):
    acc = _dot(x_ref[...], w_ref[...])
    tm = acc.shape[0]
    for h in range(N_KV_HEADS):
        rows_ref[pl.ds(h, tm, stride=N_KV_HEADS), :] = acc[:, h * HEAD_DIM:(h + 1) * HEAD_DIM]
    ob_ref[...] = acc.astype(ob_ref.dtype)
    for c in range(tm // MOBA_BLOCK):
        blk = acc[c * MOBA_BLOCK:(c + 1) * MOBA_BLOCK]
        km_ref[c] = jnp.sum(blk, axis=0, keepdims=True) * (1.0 / MOBA_BLOCK)


def _kv_proj(x, w, layer, col_off, tm):
    m, k = x.shape
    assert m % tm == 0 and tm % MOBA_BLOCK == 0 and col_off % KV_W == 0
    return pl.pallas_call(
        _kv_proj_kernel,
        out_shape=[jax.ShapeDtypeStruct((m * N_KV_HEADS, HEAD_DIM), F32),
                   jax.ShapeDtypeStruct((m, KV_W), BF16),
                   jax.ShapeDtypeStruct((m // MOBA_BLOCK, 1, KV_W), F32)],
        grid=(m // tm,),
        in_specs=[pl.BlockSpec((tm, k), lambda i: (i, 0)),
                  _layer_spec(layer, (k, KV_W), lambda i: (0, col_off // KV_W))],
        out_specs=[pl.BlockSpec((tm * N_KV_HEADS, HEAD_DIM), lambda i: (i, 0)),
                   pl.BlockSpec((tm, KV_W), lambda i: (i, 0)),
                   pl.BlockSpec((tm // MOBA_BLOCK, 1, KV_W), lambda i: (i, 0, 0))],
        compiler_params=_params("parallel"),
        name="kv_proj",
    )(x, w)


def _v_proj_kernel(x_ref, w_ref, rows_ref, vt_ref):
    acc = _dot(x_ref[...], w_ref[...])
    tm = acc.shape[0]
    for h in range(N_KV_HEADS):
        vh = acc[:, h * HEAD_DIM:(h + 1) * HEAD_DIM]
        rows_ref[pl.ds(h, tm, stride=N_KV_HEADS), :] = vh
        for c in range(tm // MOBA_BLOCK):
            vt_ref[h, c] = vh[c * MOBA_BLOCK:(c + 1) * MOBA_BLOCK].T.astype(vt_ref.dtype)


def _v_proj(x, w, layer, col_off, seq, tm):
    m, k = x.shape
    assert m % tm == 0 and seq % tm == 0 and tm % MOBA_BLOCK == 0 and col_off % KV_W == 0
    tiles_per_batch = seq // tm
    return pl.pallas_call(
        _v_proj_kernel,
        out_shape=[jax.ShapeDtypeStruct((m * N_KV_HEADS, HEAD_DIM), F32),
                   jax.ShapeDtypeStruct((m // seq, N_KV_HEADS, seq // MOBA_BLOCK, HEAD_DIM, MOBA_BLOCK), BF16)],
        grid=(m // tm,),
        in_specs=[pl.BlockSpec((tm, k), lambda i: (i, 0)),
                  _layer_spec(layer, (k, KV_W), lambda i: (0, col_off // KV_W))],
        out_specs=[pl.BlockSpec((tm * N_KV_HEADS, HEAD_DIM), lambda i: (i, 0)),
                   pl.BlockSpec((None, N_KV_HEADS, tm // MOBA_BLOCK, HEAD_DIM, MOBA_BLOCK),
                                lambda i: (i // tiles_per_batch, 0, i % tiles_per_batch, 0, 0))],
        compiler_params=_params("parallel"),
        name="v_proj",
    )(x, w)


_MOBA_KV_PER_STEP = 4


def _moba_prompt_kernel(q_ref, k_ref, vt_ref, km_ref, o_ref, sel_ref, m_ref, l_ref, acc_ref):
    i = pl.program_id(2)
    rep = N_HEADS // N_KV_HEADS
    heads = _MOBA_KV_PER_STEP * rep
    blk = MOBA_BLOCK
    scale_log2e = np.float32(HEAD_DIM ** -0.5 * np.log2(np.e))

    def cols(c):
        return slice(c * HEAD_DIM, (c + 1) * HEAD_DIM)

    qs = [q_ref[:, cols(c)] for c in range(heads)]
    blk_id = lax.broadcasted_iota(jnp.int32, (km_ref.shape[0], blk), 0)
    valid = blk_id < i
    for c in range(heads):
        km = km_ref[:, cols(c // rep)].astype(BF16)
        gate = jnp.where(valid, _dot_nt(km, qs[c]), MASKED)
        sel_ref[c] = _top_blocks(gate, valid.astype(F32), blk_id.astype(F32), 0)

    def attend(j, masks, first):
        kjs = [k_ref[pl.ds(pl.multiple_of(j * blk, blk), blk), cols(g)] for g in range(_MOBA_KV_PER_STEP)]
        scores = [_dot_nt(kjs[c // rep], qs[c]) for c in range(heads)]
        ps, m_news, l_blks = [], [], []
        for c in range(heads):
            s = scores[c] * scale_log2e
            if first:
                s = jnp.where(masks[c], s, MASKED)
                m_new = jnp.max(s, axis=0, keepdims=True)
                p = jnp.exp2(s - m_new)
            else:
                m_blk = jnp.where(masks[c], jnp.max(s, axis=0, keepdims=True), MASKED)
                m_new = jnp.maximum(m_ref[c], m_blk)
                p = jnp.exp2(s - jnp.where(masks[c], m_new, -MASKED))
            l_blks.append(jnp.sum(p, axis=0, keepdims=True))
            ps.append(p.astype(BF16))
            m_news.append(m_new)
        pvs = [_dot(vt_ref[c // rep, j], ps[c]) for c in range(heads)]
        for c in range(heads):
            if first:
                l_ref[c] = l_blks[c]
                acc_ref[c] = pvs[c]
            else:
                a = jnp.exp2(m_ref[c] - m_news[c])
                l_ref[c] = a * l_ref[c] + l_blks[c]
                acc_ref[c] = a * acc_ref[c] + pvs[c]
            m_ref[c] = m_news[c]

    key = lax.broadcasted_iota(jnp.int32, (blk, blk), 0)
    qry = lax.broadcasted_iota(jnp.int32, (blk, blk), 1)
    attend(i, [key <= qry] * heads, True)

    def body(j, carry):
        attend(j, [sel_ref[c, pl.ds(j, 1), :] > 0.5 for c in range(heads)], False)
        return carry

    lax.fori_loop(0, i, body, 0)
    for c in range(heads):
        o_ref[:, cols(c)] = (acc_ref[c] / l_ref[c]).T.astype(o_ref.dtype)


def _moba_prompt(q_b, k_b, vt, kmean, batch, seq):
    nq = seq // MOBA_BLOCK
    g = _MOBA_KV_PER_STEP
    heads = g * (N_HEADS // N_KV_HEADS)
    assert N_KV_HEADS % g == 0
    return pl.pallas_call(
        _moba_prompt_kernel,
        out_shape=jax.ShapeDtypeStruct((batch * seq, Q_W), BF16),
        grid=(batch, N_KV_HEADS // g, nq),
        in_specs=[
            pl.BlockSpec((MOBA_BLOCK, heads * HEAD_DIM), lambda b, gg, i: (b * nq + i, gg)),
            pl.BlockSpec((seq, g * HEAD_DIM), lambda b, gg, i: (b, gg)),
            pl.BlockSpec((None, g, nq, HEAD_DIM, MOBA_BLOCK), lambda b, gg, i: (b, gg, 0, 0, 0)),
            pl.BlockSpec((None, nq, g * HEAD_DIM), lambda b, gg, i: (b, 0, gg)),
        ],
        out_specs=pl.BlockSpec((MOBA_BLOCK, heads * HEAD_DIM), lambda b, gg, i: (b * nq + i, gg)),
        scratch_shapes=[pltpu.VMEM((heads, nq, MOBA_BLOCK), F32),
                        pltpu.VMEM((heads, 1, MOBA_BLOCK), F32),
                        pltpu.VMEM((heads, 1, MOBA_BLOCK), F32),
                        pltpu.VMEM((heads, HEAD_DIM, MOBA_BLOCK), F32)],
        compiler_params=_params("parallel", "parallel", "arbitrary"),
        name="moba_prompt",
    )(q_b, k_b, vt, kmean)


def _cached_moba_kernel(pt_ref, qkv_ref, ck_hbm, cv_hbm, o_ref, kbuf, vbuf, km_ref, ksem, vsem,
                        *, layer, n_pool, n_pages):
    b = pl.program_id(0)
    n = pl.num_programs(0)
    slot = b % 2
    rep = N_HEADS // N_KV_HEADS
    n_blocks = n_pages // PAGES_PER_BLOCK
    scale = HEAD_DIM ** -0.5
    base = layer * n_pool

    def k_copy(sample, page_slot, sl):
        page = base + pt_ref[sample * n_pages + page_slot]
        return pltpu.make_async_copy(ck_hbm.at[page], kbuf.at[sl, page_slot], ksem.at[sl])

    def start_keys(sample, sl):
        for p in range(n_pages):
            k_copy(sample, p, sl).start(priority=1)

    @pl.when(b == 0)
    def _():
        km_ref[...] = jnp.zeros_like(km_ref)
        start_keys(0, 0)

    @pl.when(b + 1 < n)
    def _():
        start_keys(b + 1, 1 - slot)

    for p in range(n_pages):
        k_copy(b, p, slot).wait()

    for blk in range(n_blocks):
        tot = jnp.zeros((SUBLANES, HEAD_DIM), F32)
        for r in range(PAGES_PER_BLOCK):
            page = kbuf[slot, PAGES_PER_BLOCK * blk + r]
            tot = tot + jnp.sum(page.reshape(PAGE_ROWS // SUBLANES, SUBLANES, HEAD_DIM), axis=0)
        km_ref[blk * SUBLANES:(blk + 1) * SUBLANES, :] = tot + pltpu.roll(tot, N_KV_HEADS, axis=0)

    q8 = _rows_to_sublanes(qkv_ref, N_HEADS, HEAD_DIM)
    q8b = q8.astype(BF16)
    hrow = lax.broadcasted_iota(jnp.int32, (N_HEADS, LANES), 0)
    lane = lax.broadcasted_iota(jnp.int32, (N_HEADS, LANES), 1)
    gate = jnp.zeros((N_HEADS, LANES), F32)
    for kvh in range(N_KV_HEADS):
        km = km_ref[pl.ds(kvh, LANES, stride=SUBLANES), :] * (1.0 / MOBA_BLOCK)
        gate = jnp.where(hrow >= kvh * rep, _dot_nt(q8b, km.astype(BF16)), gate)
    gate = jnp.where(lane < n_blocks, gate, MASKED)
    lane_f = lane.astype(F32)
    picks = []
    for _ in range(MOBA_TOPK):
        m = jnp.max(gate, axis=-1, keepdims=True)
        first = jnp.min(jnp.where(gate == m, lane_f, float(LANES)), axis=-1, keepdims=True)
        picks.append(first.astype(jnp.int32))
        gate = jnp.where(lane_f == first, -jnp.inf, gate)
    blocks = [[picks[t][h, 0] for t in range(MOBA_TOPK)] for h in range(N_HEADS)]

    def v_copy(h, t, r):
        page = pt_ref[b * n_pages + blocks[h][t] * PAGES_PER_BLOCK + r]
        return pltpu.make_async_copy(cv_hbm.at[layer, page, :, h // rep, :],
                                     vbuf.at[(h * MOBA_TOPK + t) * PAGES_PER_BLOCK + r], vsem)

    sel = [(h, t, r) for h in range(N_HEADS) for t in range(MOBA_TOPK) for r in range(PAGES_PER_BLOCK)]
    for h, t, r in sel:
        v_copy(h, t, r).start()

    k_new = qkv_ref[0, :, Q_W:Q_W + KV_W]
    v_new = qkv_ref[0, :, Q_W + KV_W:]
    scores = {}
    for h, t, r in sel:
        kvh = h // rep
        kp = kbuf[slot, blocks[h][t] * PAGES_PER_BLOCK + r, pl.ds(kvh, PAGE_SIZE, stride=N_KV_HEADS), :]
        qh = jnp.broadcast_to(q8b[h:h + 1], (SUBLANES, HEAD_DIM))
        scores[h, t, r] = _dot_nt(qh, kp.astype(BF16))[0:1] * scale
    probs, p_new = {}, []
    for h in range(N_HEADS):
        kvh = h // rep
        kn = k_new[:, kvh * HEAD_DIM:(kvh + 1) * HEAD_DIM].astype(BF16).astype(F32)
        s_new = jnp.sum(q8b[h:h + 1].astype(F32) * kn, axis=-1, keepdims=True) * scale
        mine = [scores[h, t, r] for t in range(MOBA_TOPK) for r in range(PAGES_PER_BLOCK)]
        m = s_new
        for s in mine:
            m = jnp.maximum(m, jnp.max(s, axis=-1, keepdims=True))
        es = [jnp.exp(s - m) for s in mine]
        e_new = jnp.exp(s_new - m)
        denom = e_new
        for e in es:
            denom = denom + jnp.sum(e, axis=-1, keepdims=True)
        p_new.append((e_new / denom).astype(BF16).astype(F32))
        for idx, (t, r) in enumerate((t, r) for t in range(MOBA_TOPK) for r in range(PAGES_PER_BLOCK)):
            probs[h, t, r] = (es[idx] / denom).astype(BF16)

    for h, t, r in sel:
        v_copy(h, t, r).wait()

    for h in range(N_HEADS):
        kvh = h // rep
        out = p_new[h] * v_new[:, kvh * HEAD_DIM:(kvh + 1) * HEAD_DIM].astype(BF16).astype(F32)
        for t in range(MOBA_TOPK):
            for r in range(PAGES_PER_BLOCK):
                p8 = jnp.broadcast_to(probs[h, t, r], (SUBLANES, PAGE_SIZE))
                vp = vbuf[(h * MOBA_TOPK + t) * PAGES_PER_BLOCK + r]
                out = out + _dot(p8, vp.astype(BF16))[0:1]
        o_ref[0, :, h * HEAD_DIM:(h + 1) * HEAD_DIM] = out


def _cached_moba(qkv_f3, cache_k_rows, cache_v, pt_flat, layer, n_pool, n_pages):
    n = qkv_f3.shape[0]
    assert SUBLANES == 2 * N_KV_HEADS and n_pages // PAGES_PER_BLOCK <= LANES
    return pl.pallas_call(
        functools.partial(_cached_moba_kernel, layer=layer, n_pool=n_pool, n_pages=n_pages),
        out_shape=jax.ShapeDtypeStruct((n, 1, Q_W), F32),
        grid_spec=pltpu.PrefetchScalarGridSpec(
            num_scalar_prefetch=1,
            grid=(n,),
            in_specs=[pl.BlockSpec((1, 1, QKV_W), lambda b, pt: (b, 0, 0)),
                      pl.BlockSpec(memory_space=pl.ANY),
                      pl.BlockSpec(memory_space=pl.ANY)],
            out_specs=pl.BlockSpec((1, 1, Q_W), lambda b, pt: (b, 0, 0)),
            scratch_shapes=[pltpu.VMEM((2, n_pages, PAGE_ROWS, HEAD_DIM), F32),
                            pltpu.VMEM((N_HEADS * MOBA_TOPK * PAGES_PER_BLOCK, PAGE_SIZE, HEAD_DIM), F32),
                            pltpu.VMEM((LANES * SUBLANES, HEAD_DIM), F32),
                            pltpu.SemaphoreType.DMA((2,)),
                            pltpu.SemaphoreType.DMA(())],
        ),
        compiler_params=_params("arbitrary"),
        name="cached_moba",
    )(pt_flat, qkv_f3, cache_k_rows, cache_v)


def _proj_act_kernel(x_ref, w_ref, *refs, act):
    o_ref = refs[-1]
    acc = _dot(x_ref[...], w_ref[...])
    if act == "gelu":
        o_ref[...] = _gelu(acc)
    elif act == "gelu_ln":
        o_ref[...] = _layer_norm(_gelu(acc), refs[0][...], refs[1][...])
    else:
        o_ref[...] = jax.nn.sigmoid(acc + refs[0][...])


def _proj_act(x, w, w_layer, col_off, ncols, act, vecs, layer, tm, tn):
    m, k = x.shape
    assert m % tm == 0 and ncols % tn == 0 and col_off % tn == 0
    assert act != "gelu_ln" or tn == ncols
    joff = col_off // tn
    return pl.pallas_call(
        functools.partial(_proj_act_kernel, act=act),
        out_shape=jax.ShapeDtypeStruct((m, ncols), F32),
        grid=(ncols // tn, m // tm),
        in_specs=[pl.BlockSpec((tm, k), lambda j, i: (i, 0)),
                  _layer_spec(w_layer, (k, tn), lambda j, i: (0, j + joff))]
        + [_layer_spec(layer, (1, tn), lambda j, i: (0, j)) for _ in vecs],
        out_specs=pl.BlockSpec((tm, tn), lambda j, i: (i, j)),
        compiler_params=_params("parallel", "parallel"),
        name="proj_" + act,
    )(x, w, *vecs)


def _sgu_mix_kernel(gu_ref, vn_ref, ws_ref, bs_ref, s_ref):
    t = SGU_CHUNK
    row = lax.broadcasted_iota(jnp.int32, (t, t), 0)
    col = lax.broadcasted_iota(jnp.int32, (t, t), 1)
    for g in range(SGU_GROUPS):
        cs = slice(g * SGU_GROUP_DIM, (g + 1) * SGU_GROUP_DIM)
        ws = jnp.where(col <= row, ws_ref[g], 0.0).astype(BF16)
        bias = bs_ref[:, g:g + 1]
        for c in range(gu_ref.shape[0] // t):
            rs = slice(c * t, (c + 1) * t)
            mixed = _dot(ws, vn_ref[rs, cs].astype(BF16)) + bias
            s_ref[rs, cs] = (gu_ref[rs, cs] * mixed).astype(s_ref.dtype)


def _prompt_mid_kernel(a_ref, gu_ref, vn_ref, ga_ref, gb_ref, x_ref, mk_ref, mv_ref,
                       ws_ref, bs_ref, wpa_ref, wpb_ref, wo_ref, g1_ref, b1_ref,
                       wxq_ref, wxo_ref, g2_ref, b2_ref, of_ref, ob_ref, s_scr, o_scr):
    _sgu_mix_kernel(gu_ref, vn_ref, ws_ref, bs_ref, s_scr)
    mix = ga_ref[...] * _dot(a_ref[...], wpa_ref[...]) + gb_ref[...] * _dot(s_scr[...], wpb_ref[...])
    x1 = _layer_norm(DN_ALPHA * x_ref[...] + _dot(mix.astype(BF16), wo_ref[...]), g1_ref[...], b1_ref[...])
    qx = _dot(x1.astype(BF16), wxq_ref[...]).astype(BF16)
    scale = X_HEAD_DIM ** -0.5
    for h in range(X_HEADS):
        cs = slice(h * X_HEAD_DIM, (h + 1) * X_HEAD_DIM)
        s = _dot_nt(qx[:, cs], mk_ref[:, cs]) * scale
        e = jnp.exp(s - jnp.max(s, axis=-1, keepdims=True))
        p = (e / jnp.sum(e, axis=-1, keepdims=True)).astype(BF16)
        o_scr[:, cs] = _dot(p, mv_ref[:, cs]).astype(o_scr.dtype)
    y = _layer_norm(DN_ALPHA * x1 + _dot(o_scr[...], wxo_ref[...]), g2_ref[...], b2_ref[...])
    of_ref[...] = y
    ob_ref[...] = y.astype(ob_ref.dtype)


def _prompt_mid(a, gu, vn, gates, x, mk, mv, w_s, b_s_t, w, w_xq, layer, seq, tm):
    m = a.shape[0]
    assert m % tm == 0 and seq % tm == 0 and tm % SGU_CHUNK == 0
    tiles_per_batch = seq // tm

    def const(block):
        return pl.BlockSpec((None,) + block, lambda i: (layer,) + (0,) * len(block), pipeline_mode=pl.Buffered(1))

    mem = pl.BlockSpec((None, N_MEM, X_W), lambda i: (i // tiles_per_batch, 0, 0))
    vec_d = const((1, D_MODEL))
    return pl.pallas_call(
        _prompt_mid_kernel,
        out_shape=[jax.ShapeDtypeStruct((m, D_MODEL), F32), jax.ShapeDtypeStruct((m, D_MODEL), BF16)],
        grid=(m // tm,),
        in_specs=[pl.BlockSpec((tm, Q_W), lambda i: (i, 0)),
                  pl.BlockSpec((tm, SGU_WIDTH), lambda i: (i, 0)),
                  pl.BlockSpec((tm, SGU_WIDTH), lambda i: (i, 0)),
                  pl.BlockSpec((tm, D_MODEL), lambda i: (i, 0)),
                  pl.BlockSpec((tm, D_MODEL), lambda i: (i, 1)),
                  pl.BlockSpec((tm, D_MODEL), lambda i: (i, 0)),
                  mem, mem,
                  const((SGU_GROUPS, SGU_CHUNK, SGU_CHUNK)), const((SGU_CHUNK, SGU_GROUPS)),
                  const((Q_W, D_MODEL)), const((SGU_WIDTH, D_MODEL)),
                  const((D_MODEL, D_MODEL)), vec_d, vec_d,
                  const((D_MODEL, X_W)), const((X_W, D_MODEL)), vec_d, vec_d],
        out_specs=[pl.BlockSpec((tm, D_MODEL), lambda i: (i, 0)), pl.BlockSpec((tm, D_MODEL), lambda i: (i, 0))],
        scratch_shapes=[pltpu.VMEM((tm, SGU_WIDTH), BF16), pltpu.VMEM((tm, X_W), BF16)],
        compiler_params=_params("parallel"),
        name="prompt_mid",
    )(a, gu, vn, gates, gates, x, mk, mv, w_s, b_s_t,
      w["w_pa"], w["w_pb"], w["w_o"], w["ln1_g"], w["ln1_b"],
      w_xq, w["w_xo"], w["ln2_g"], w["ln2_b"])


def _sgu_first_row_kernel(zu_ref, zv_ref, g_ref, b_ref, w0_ref, b0_ref, s_ref, vn_ref):
    vn = _layer_norm(_gelu(zv_ref[...]), g_ref[...], b_ref[...])
    vn_ref[...] = vn
    s_ref[...] = (_gelu(zu_ref[...]) * (vn * w0_ref[...] + b0_ref[...])).astype(s_ref.dtype)


def _sgu_first_rows(rest, ln_g, ln_b, w0, b0):
    m = rest.shape[0]
    vec = pl.BlockSpec((1, SGU_WIDTH), lambda i: (0, 0))
    return pl.pallas_call(
        _sgu_first_row_kernel,
        out_shape=[jax.ShapeDtypeStruct((m, SGU_WIDTH), BF16), jax.ShapeDtypeStruct((m, SGU_WIDTH), F32)],
        grid=(1,),
        in_specs=[pl.BlockSpec((m, SGU_WIDTH), lambda i: (0, 0)),
                  pl.BlockSpec((m, SGU_WIDTH), lambda i: (0, 1)), vec, vec, vec, vec],
        out_specs=[pl.BlockSpec((m, SGU_WIDTH), lambda i: (0, 0)), pl.BlockSpec((m, SGU_WIDTH), lambda i: (0, 0))],
        compiler_params=_params("arbitrary"),
        name="sgu_first_rows",
    )(rest, rest, ln_g, ln_b, w0, b0)


def _merge_kernel(a_ref, s_ref, ga_ref, gb_ref, bg_ref, wa_ref, wb_ref, o_ref):
    g_a = jax.nn.sigmoid(ga_ref[...] + bg_ref[0:1, :])
    g_b = jax.nn.sigmoid(gb_ref[...] + bg_ref[1:2, :])
    mix = g_a * _dot(a_ref[...], wa_ref[...]) + g_b * _dot(s_ref[...], wb_ref[...])
    o_ref[...] = mix.astype(o_ref.dtype)


def _merge(a, s, rest, b_gate, w_pa, w_pb, layer, tm, tn):
    m = a.shape[0]
    tm = min(tm, m)
    ga0 = 2 * SGU_WIDTH // tn
    gb0 = (2 * SGU_WIDTH + D_MODEL) // tn
    return pl.pallas_call(
        _merge_kernel,
        out_shape=jax.ShapeDtypeStruct((m, D_MODEL), BF16),
        grid=(D_MODEL // tn, m // tm),
        in_specs=[pl.BlockSpec((tm, Q_W), lambda j, i: (i, 0)),
                  pl.BlockSpec((tm, SGU_WIDTH), lambda j, i: (i, 0)),
                  pl.BlockSpec((tm, tn), lambda j, i: (i, ga0 + j)),
                  pl.BlockSpec((tm, tn), lambda j, i: (i, gb0 + j)),
                  _layer_spec(layer, (2, tn), lambda j, i: (0, j)),
                  _layer_spec(layer, (Q_W, tn), lambda j, i: (0, j)),
                  _layer_spec(layer, (SGU_WIDTH, tn), lambda j, i: (0, j))],
        out_specs=pl.BlockSpec((tm, tn), lambda j, i: (i, j)),
        compiler_params=_params("parallel", "parallel"),
        name="merge",
    )(a, s, rest, rest, b_gate, w_pa, w_pb)


def _proj_ln_kernel(a_ref, w_ref, x_ref, g_ref, b_ref, of_ref, ob_ref):
    z = DN_ALPHA * x_ref[...] + _dot(a_ref[...], w_ref[...])
    y = _layer_norm(z, g_ref[...], b_ref[...])
    of_ref[...] = y
    ob_ref[...] = y.astype(ob_ref.dtype)


def _proj_ln(a, w, x, g, b, layer, tm):
    m, k = a.shape
    tm = min(tm, m)
    vec = _layer_spec(layer, (1, D_MODEL), lambda i: (0, 0))
    return pl.pallas_call(
        _proj_ln_kernel,
        out_shape=[jax.ShapeDtypeStruct((m, D_MODEL), F32), jax.ShapeDtypeStruct((m, D_MODEL), BF16)],
        grid=(m // tm,),
        in_specs=[pl.BlockSpec((tm, k), lambda i: (i, 0)),
                  _layer_spec(layer, (k, D_MODEL), lambda i: (0, 0)),
                  pl.BlockSpec((tm, D_MODEL), lambda i: (i, 0)), vec, vec],
        out_specs=[pl.BlockSpec((tm, D_MODEL), lambda i: (i, 0)), pl.BlockSpec((tm, D_MODEL), lambda i: (i, 0))],
        compiler_params=_params("parallel"),
        name="proj_ln",
    )(a, w, x, g, b)


_XATTN_SAMPLES_PER_STEP = 4


def _xattn_single_kernel(q_ref, mk_ref, mv_ref, o_ref):
    scale = X_HEAD_DIM ** -0.5
    rows = N_MEM * X_HEADS
    head = lax.broadcasted_iota(jnp.int32, (SUBLANES, rows), 0)
    row_head = jnp.bitwise_and(lax.broadcasted_iota(jnp.int32, (SUBLANES, rows), 1), X_HEADS - 1)
    for b in range(q_ref.shape[0]):
        q8 = _rows_to_sublanes(q_ref, X_HEADS, X_HEAD_DIM, b).astype(BF16)
        s = jnp.where(row_head == head, _dot_nt(q8, mk_ref[b].astype(BF16)) * scale, MASKED)
        e = jnp.exp(s - jnp.max(s, axis=-1, keepdims=True))
        p = (e / jnp.sum(e, axis=-1, keepdims=True)).astype(BF16)
        o = _dot(p, mv_ref[b].astype(BF16))
        for h in range(X_HEADS):
            o_ref[b, :, h * X_HEAD_DIM:(h + 1) * X_HEAD_DIM] = o[h:h + 1].astype(o_ref.dtype)


def _xattn_single(q3, mk_rows, mv_rows, mem_off):
    n = q3.shape[0]
    rows = N_MEM * X_HEADS
    per = _XATTN_SAMPLES_PER_STEP
    assert n % per == 0 and mem_off % per == 0
    return pl.pallas_call(
        _xattn_single_kernel,
        out_shape=jax.ShapeDtypeStruct((n, 1, X_W), F32),
        grid=(n // per,),
        in_specs=[pl.BlockSpec((per, 1, X_W), lambda b: (b, 0, 0)),
                  pl.BlockSpec((per, rows, X_HEAD_DIM), lambda b: (mem_off // per + b, 0, 0)),
                  pl.BlockSpec((per, rows, X_HEAD_DIM), lambda b: (mem_off // per + b, 0, 0))],
        out_specs=pl.BlockSpec((per, 1, X_W), lambda b: (b, 0, 0)),
        compiler_params=_params("parallel"),
        name="xattn_single",
    )(q3, mk_rows, mv_rows)


def _mlp_kernel(xb_ref, xf_ref, wu_ref, wd_ref, g_ref, b_ref, of_ref, ob_ref):
    f = pl.program_id(1)

    @pl.when(f == 0)
    def _():
        of_ref[...] = jnp.zeros_like(of_ref)

    h = jnp.maximum(_dot(xb_ref[...], wu_ref[...]), 0.0)
    of_ref[...] += _dot((h * h).astype(BF16), wd_ref[...])

    @pl.when(f == pl.num_programs(1) - 1)
    def _():
        y = _layer_norm(DN_ALPHA * xf_ref[...] + of_ref[...], g_ref[...], b_ref[...])
        of_ref[...] = y
        ob_ref[...] = y.astype(ob_ref.dtype)


def _mlp(xb, xf, w_up, w_down, w_layer, g, b, layer, tm, tf):
    m = xb.shape[0]
    tm = min(tm, m)
    vec = _layer_spec(layer, (1, D_MODEL), lambda i, f: (0, 0))
    return pl.pallas_call(
        _mlp_kernel,
        out_shape=[jax.ShapeDtypeStruct((m, D_MODEL), F32), jax.ShapeDtypeStruct((m, D_MODEL), BF16)],
        grid=(m // tm, D_FF // tf),
        in_specs=[pl.BlockSpec((tm, D_MODEL), lambda i, f: (i, 0)),
                  pl.BlockSpec((tm, D_MODEL), lambda i, f: (i, 0)),
                  _layer_spec(w_layer, (D_MODEL, tf), lambda i, f: (0, f)),
                  _layer_spec(w_layer, (tf, D_MODEL), lambda i, f: (f, 0)), vec, vec],
        out_specs=[pl.BlockSpec((tm, D_MODEL), lambda i, f: (i, 0)),
                   pl.BlockSpec((tm, D_MODEL), lambda i, f: (i, 0))],
        compiler_params=_params("parallel", "arbitrary"),
        name="mlp",
    )(xb, xf, w_up, w_down, g, b)


def _mlp_cast_kernel(xb_ref, xf_ref, wu_ref, wd_ref, g_ref, b_ref, of_ref, ob_ref, wub_ref, wdb_ref):
    wub_ref[...] = wu_ref[...].astype(BF16)
    wdb_ref[...] = wd_ref[...].astype(BF16)
    _mlp_kernel(xb_ref, xf_ref, wub_ref, wdb_ref, g_ref, b_ref, of_ref, ob_ref)


def _mlp_casting(xb, xf, w_up_f32, w_down_f32, g, b, layer, tf):
    m = xb.shape[0]
    vec = _layer_spec(layer, (1, D_MODEL), lambda i, f: (0, 0))
    rows = pl.BlockSpec((m, D_MODEL), lambda i, f: (0, 0))
    return pl.pallas_call(
        _mlp_cast_kernel,
        out_shape=[jax.ShapeDtypeStruct((m, D_MODEL), F32), jax.ShapeDtypeStruct((m, D_MODEL), BF16),
                   jax.ShapeDtypeStruct((1, D_MODEL, D_FF), BF16), jax.ShapeDtypeStruct((1, D_FF, D_MODEL), BF16)],
        grid=(1, D_FF // tf),
        in_specs=[rows, rows,
                  _layer_spec(layer, (D_MODEL, tf), lambda i, f: (0, f)),
                  _layer_spec(layer, (tf, D_MODEL), lambda i, f: (f, 0)), vec, vec],
        out_specs=[rows, rows,
                   pl.BlockSpec((None, D_MODEL, tf), lambda i, f: (0, 0, f)),
                   pl.BlockSpec((None, tf, D_MODEL), lambda i, f: (0, f, 0))],
        compiler_params=_params("parallel", "arbitrary"),
        name="mlp_casting",
    )(xb, xf, w_up_f32, w_down_f32, g, b)


_MLP_TM = 512
_MLP_TF = 1024
_MLP_CAST_TF = 512
_MID_TM = 256
_PROJ_TM = 1024
_PROJ_SMALL_TM = 512


def kernel(x_prompt, x_sample, mem_prompt, cache_k, cache_v, cache_mem_k, cache_mem_v, page_table,
           w_in, b_gate, sgu_ln_g, sgu_ln_b, w_s, b_s, w_pa, w_pb, w_o, ln1_g, ln1_b,
           w_xq, w_xk, w_xv, w_xo, ln2_g, ln2_b, w_up, w_down, ln3_g, ln3_b):
    batch, seq, _ = x_prompt.shape
    n_samples, dec_seq, _ = x_sample.shape
    assert dec_seq == 1 and seq % MOBA_BLOCK == 0
    assert N_KV_HEADS & (N_KV_HEADS - 1) == 0 and X_HEADS & (X_HEADS - 1) == 0 and X_HEADS <= SUBLANES
    n_pool = cache_k.shape[1]
    n_pages = page_table.shape[1]
    n_blocks = seq // MOBA_BLOCK
    assert n_pages % PAGES_PER_BLOCK == 0
    mp, ms = batch * seq, n_samples

    pt_flat = page_table.reshape(-1).astype(jnp.int32)
    cache_k_rows = cache_k.reshape(DEPTH * n_pool, PAGE_ROWS, HEAD_DIM)
    cache_mk_rows = cache_mem_k.reshape(DEPTH * n_samples, N_MEM * X_HEADS, X_HEAD_DIM)
    cache_mv_rows = cache_mem_v.reshape(DEPTH * n_samples, N_MEM * X_HEADS, X_HEAD_DIM)
    mem_b = mem_prompt.reshape(batch * N_MEM, D_MODEL).astype(BF16)

    xpf = x_prompt.reshape(mp, D_MODEL)
    xsf = x_sample.reshape(ms, D_MODEL)
    xpb, xsb = xpf.astype(BF16), xsf.astype(BF16)

    w = dict(
        b_gate=b_gate, w_pa=_to_bf16(w_pa), w_pb=_to_bf16(w_pb), w_o=_to_bf16(w_o), w_xo=_to_bf16(w_xo),
        ln1_g=ln1_g[:, None], ln1_b=ln1_b[:, None], ln2_g=ln2_g[:, None], ln2_b=ln2_b[:, None],
        ln3_g=ln3_g[:, None], ln3_b=ln3_b[:, None])
    w_xq_b, w_xk_b, w_xv_b = _to_bf16(w_xq), _to_bf16(w_xk), _to_bf16(w_xv)

    sgu_g3, sgu_b3 = sgu_ln_g[:, None], sgu_ln_b[:, None]
    b_gate_row = b_gate.reshape(DEPTH, 1, 2 * D_MODEL)
    b_s_t = jnp.swapaxes(b_s, 1, 2)

    kp_l, vp_l, mkp_l, mvp_l, ks_l, vs_l, vns_l = [], [], [], [], [], [], []
    for l in range(DEPTH):
        h_s, w_in_b = _matmul_casting(xsb, w_in, l, 1024)
        qkv_s, rest_s = h_s[:, :QKV_W], h_s[:, QKV_W:]
        qkv_s3 = qkv_s.reshape(ms, 1, QKV_W)
        a_s = _cached_moba(qkv_s3, cache_k_rows, cache_v, pt_flat, l, n_pool, n_pages)
        a_s = a_s.reshape(ms, Q_W).astype(BF16)
        w0 = jnp.repeat(w_s[l][:, 0, 0], SGU_GROUP_DIM)[None]
        b0 = jnp.repeat(b_s[l][:, 0], SGU_GROUP_DIM)[None]
        s_s, vn_s = _sgu_first_rows(rest_s, sgu_ln_g[l][None], sgu_ln_b[l][None], w0, b0)
        mix_s = _merge(a_s, s_s, rest_s, w["b_gate"], w["w_pa"], w["w_pb"], l, ms, 1024)
        x1f, x1b = _proj_ln(mix_s, w["w_o"], xsf, w["ln1_g"], w["ln1_b"], l, ms)
        (qx,) = _matmul(x1b, w_xq_b, l, 0, X_W, (F32,), ms, X_W)
        o_s = _xattn_single(qx.reshape(ms, 1, X_W), cache_mk_rows, cache_mv_rows, l * n_samples)
        x2f, x2b = _proj_ln(o_s.reshape(ms, X_W).astype(BF16), w["w_xo"], x1f, w["ln2_g"], w["ln2_b"], l, ms)
        xsf, xsb, w_up_b, w_down_b = _mlp_casting(x2b, x2f, w_up, w_down, w["ln3_g"], w["ln3_b"], l, _MLP_CAST_TF)
        ks_l.append(qkv_s[:, Q_W:Q_W + KV_W].reshape(ms, 1, N_KV_HEADS, HEAD_DIM))
        vs_l.append(qkv_s[:, Q_W + KV_W:].reshape(ms, 1, N_KV_HEADS, HEAD_DIM))
        vns_l.append(vn_s.reshape(ms, 1, SGU_WIDTH))

        mk_f, mk_b = _matmul(mem_b, w_xk_b, l, 0, X_W, (F32, BF16), 512, X_W)
        mv_f, mv_b = _matmul(mem_b, w_xv_b, l, 0, X_W, (F32, BF16), 512, X_W)
        (q_b,) = _matmul(xpb, w_in_b, 0, 0, Q_W, (BF16,), _PROJ_SMALL_TM, Q_W)
        k_rows, k_b, kmean = _kv_proj(xpb, w_in_b, 0, Q_W, _PROJ_SMALL_TM)
        v_rows, vt = _v_proj(xpb, w_in_b, 0, Q_W + KV_W, seq, _PROJ_SMALL_TM)
        gu = _proj_act(xpb, w_in_b, 0, QKV_W, SGU_WIDTH, "gelu", (), l, _PROJ_SMALL_TM, SGU_WIDTH)
        vn = _proj_act(xpb, w_in_b, 0, QKV_W + SGU_WIDTH, SGU_WIDTH, "gelu_ln", (sgu_g3, sgu_b3), l,
                       _PROJ_SMALL_TM, SGU_WIDTH)
        gates = _proj_act(xpb, w_in_b, 0, QKV_W + 2 * SGU_WIDTH, 2 * D_MODEL, "gate", (b_gate_row,), l,
                          _PROJ_TM, 1024)
        a = _moba_prompt(q_b, k_b, vt, kmean.reshape(batch, n_blocks, KV_W), batch, seq)
        x2f, x2b = _prompt_mid(a, gu, vn, gates, xpf, mk_b.reshape(batch, N_MEM, X_W),
                               mv_b.reshape(batch, N_MEM, X_W), w_s, b_s_t, w, w_xq_b, l, seq, _MID_TM)
        xpf, xpb = _mlp(x2b, x2f, w_up_b, w_down_b, 0, w["ln3_g"], w["ln3_b"], l, _MLP_TM, _MLP_TF)
        kp_l.append(k_rows.reshape(batch, seq // PAGE_SIZE, PAGE_SIZE, N_KV_HEADS, HEAD_DIM))
        vp_l.append(v_rows.reshape(batch, seq // PAGE_SIZE, PAGE_SIZE, N_KV_HEADS, HEAD_DIM))
        mkp_l.append(mk_f.reshape(batch, N_MEM, X_HEADS, X_HEAD_DIM))
        mvp_l.append(mv_f.reshape(batch, N_MEM, X_HEADS, X_HEAD_DIM))

    return (xpf.reshape(batch, seq, D_MODEL), xsf.reshape(ms, 1, D_MODEL),
            jnp.stack(kp_l), jnp.stack(vp_l), jnp.stack(mkp_l), jnp.stack(mvp_l),
            jnp.stack(ks_l), jnp.stack(vs_l), jnp.stack(vns_l))
```

```python
import functools

import jax
import jax.numpy as jnp
import numpy as np
from jax import lax
from jax.experimental import pallas as pl
from jax.experimental.pallas import tpu as pltpu

D_MODEL = 2048
DEPTH = 2
PAGE_SIZE = 128
N_HEADS = 8
N_KV_HEADS = 4
HEAD_DIM = 128
MOBA_BLOCK = 256
MOBA_TOPK = 3
SGU_WIDTH = 1024
SGU_GROUPS = 8
SGU_GROUP_DIM = SGU_WIDTH // SGU_GROUPS
SGU_CHUNK = 128
N_MEM = 256
X_HEADS = 4
X_HEAD_DIM = 128
D_FF = 4 * D_MODEL
DN_ALPHA = (2 * DEPTH) ** 0.25
LN_EPS = 1e-5
Q_W = N_HEADS * HEAD_DIM
KV_W = N_KV_HEADS * HEAD_DIM
X_W = X_HEADS * X_HEAD_DIM
QKV_W = Q_W + 2 * KV_W
REST_W = 2 * SGU_WIDTH + 2 * D_MODEL
PAGES_PER_BLOCK = MOBA_BLOCK // PAGE_SIZE
PAGE_ROWS = PAGE_SIZE * N_KV_HEADS
MASKED = -1e30

LANES = 128
SUBLANES = 8
VMEM_LIMIT = 56 * 1024 * 1024

BF16 = jnp.bfloat16
F32 = jnp.float32
_NT = (((1,), (1,)), ((), ()))


def _params(*sem):
    return pltpu.CompilerParams(dimension_semantics=sem, vmem_limit_bytes=VMEM_LIMIT)


def _dot(a, b):
    return jnp.dot(a, b, preferred_element_type=F32)


def _dot_nt(a, b):
    return lax.dot_general(a, b, _NT, preferred_element_type=F32)


def _gelu(x):
    c = np.float32(np.sqrt(2 / np.pi))
    return x * (0.5 * (1.0 + jnp.tanh(c * (x + 0.044715 * (x * x * x)))))


def _layer_norm(z, g, b):
    mu = jnp.mean(z, axis=-1, keepdims=True)
    d = z - mu
    var = jnp.mean(d * d, axis=-1, keepdims=True)
    return d * lax.rsqrt(var + LN_EPS) * g + b


def _top_blocks(gate, valid_f, idx_f, axis):
    sel = jnp.zeros(gate.shape, F32)
    for _ in range(MOBA_TOPK):
        m = jnp.max(gate, axis=axis, keepdims=True)
        first = jnp.min(jnp.where(gate == m, idx_f, float(gate.shape[axis])), axis=axis, keepdims=True)
        pick = idx_f == first
        sel = jnp.where(pick, valid_f, sel)
        gate = jnp.where(pick, -jnp.inf, gate)
    return sel


def _rows_to_sublanes(ref, n_rows, width, lead=0):
    sub = lax.broadcasted_iota(jnp.int32, (SUBLANES, width), 0)
    out = jnp.zeros((SUBLANES, width), F32)
    for r in range(n_rows):
        out = jnp.where(sub == r, ref[lead, :, r * width:(r + 1) * width].astype(F32), out)
    return out


def _mm_kernel(x_ref, w_ref, *out_refs):
    acc = _dot(x_ref[...], w_ref[...])
    for o in out_refs:
        o[...] = acc.astype(o.dtype)


def _cast_kernel(x_ref, o_ref):
    o_ref[...] = x_ref[...].astype(o_ref.dtype)


_CAST_BLOCK_BYTES = 4 * 1024 * 1024


def _to_bf16(w):
    d, k, n = w.shape
    rows = d * k
    tr = min(rows, max(SUBLANES, _CAST_BLOCK_BYTES // (4 * n)))
    assert rows % tr == 0
    out = pl.pallas_call(
        _cast_kernel,
        out_shape=jax.ShapeDtypeStruct((rows, n), BF16),
        grid=(rows // tr,),
        in_specs=[pl.BlockSpec((tr, n), lambda i: (i, 0))],
        out_specs=pl.BlockSpec((tr, n), lambda i: (i, 0)),
        compiler_params=_params("parallel"),
        name="to_bf16",
    )(w.reshape(rows, n))
    return out.reshape(d, k, n)


def _layer_spec(layer, block, index):
    return pl.BlockSpec((None,) + block, lambda *g: (layer,) + index(*g))


def _matmul(x, w, layer, col_off, ncols, out_dtypes, tm, tn):
    m, k = x.shape
    tm = min(tm, m)
    tn = min(tn, ncols)
    assert m % tm == 0 and ncols % tn == 0 and col_off % tn == 0
    joff = col_off // tn
    outs = pl.pallas_call(
        _mm_kernel,
        out_shape=[jax.ShapeDtypeStruct((m, ncols), dt) for dt in out_dtypes],
        grid=(ncols // tn, m // tm),
        in_specs=[pl.BlockSpec((tm, k), lambda j, i: (i, 0)),
                  _layer_spec(layer, (k, tn), lambda j, i: (0, j + joff))],
        out_specs=[pl.BlockSpec((tm, tn), lambda j, i: (i, j)) for _ in out_dtypes],
        compiler_params=_params("parallel", "parallel"),
        name="matmul",
    )(x, w)
    return outs


def _mm_cast_kernel(x_ref, w_ref, o_ref, wb_ref):
    wb = w_ref[...].astype(BF16)
    wb_ref[...] = wb
    o_ref[...] = _dot(x_ref[...], wb)


def _matmul_casting(x, w_f32, layer, tn):
    m, k = x.shape
    n = w_f32.shape[2]
    assert n % tn == 0
    return pl.pallas_call(
        _mm_cast_kernel,
        out_shape=[jax.ShapeDtypeStruct((m, n), F32), jax.ShapeDtypeStruct((1, k, n), BF16)],
        grid=(n // tn,),
        in_specs=[pl.BlockSpec((m, k), lambda j: (0, 0)),
                  _layer_spec(layer, (k, tn), lambda j: (0, j))],
        out_specs=[pl.BlockSpec((m, tn), lambda j: (0, j)),
                   pl.BlockSpec((None, k, tn), lambda j: (0, 0, j))],
        compiler_params=_params("parallel"),
        name="matmul_casting",
    )(x, w_f32)


def _kv_proj_kernel(x_ref, w_ref, rows_ref, ob_ref, km_ref):
    acc = _dot(x_ref[...], w_ref[...])
    tm = acc.shape[0]
    for h in range(N_KV_HEADS):
        rows_ref[pl.ds(h, tm, stride=N_KV_HEADS), :] = acc[:, h * HEAD_DIM:(h + 1) * HEAD_DIM]
    ob_ref[...] = acc.astype(ob_ref.dtype)
    for c in range(tm // MOBA_BLOCK):
        blk = acc[c * MOBA_BLOCK:(c + 1) * MOBA_BLOCK]
        km_ref[c] = jnp.sum(blk, axis=0, keepdims=True) * (1.0 / MOBA_BLOCK)


def _kv_proj(x, w, layer, col_off, tm):
    m, k = x.shape
    assert m % tm == 0 and tm % MOBA_BLOCK == 0 and col_off % KV_W == 0
    return pl.pallas_call(
        _kv_proj_kernel,
        out_shape=[jax.ShapeDtypeStruct((m * N_KV_HEADS, HEAD_DIM), F32),
                   jax.ShapeDtypeStruct((m, KV_W), BF16),
                   jax.ShapeDtypeStruct((m // MOBA_BLOCK, 1, KV_W), F32)],
        grid=(m // tm,),
        in_specs=[pl.BlockSpec((tm, k), lambda i: (i, 0)),
                  _layer_spec(layer, (k, KV_W), lambda i: (0, col_off // KV_W))],
        out_specs=[pl.BlockSpec((tm * N_KV_HEADS, HEAD_DIM), lambda i: (i, 0)),
                   pl.BlockSpec((tm, KV_W), lambda i: (i, 0)),
                   pl.BlockSpec((tm // MOBA_BLOCK, 1, KV_W), lambda i: (i, 0, 0))],
        compiler_params=_params("parallel"),
        name="kv_proj",
    )(x, w)


def _v_proj_kernel(x_ref, w_ref, rows_ref, vt_ref):
    acc = _dot(x_ref[...], w_ref[...])
    tm = acc.shape[0]
    for h in range(N_KV_HEADS):
        vh = acc[:, h * HEAD_DIM:(h + 1) * HEAD_DIM]
        rows_ref[pl.ds(h, tm, stride=N_KV_HEADS), :] = vh
        for c in range(tm // MOBA_BLOCK):
            vt_ref[h, c] = vh[c * MOBA_BLOCK:(c + 1) * MOBA_BLOCK].T.astype(vt_ref.dtype)


def _v_proj(x, w, layer, col_off, seq, tm):
    m, k = x.shape
    assert m % tm == 0 and seq % tm == 0 and tm % MOBA_BLOCK == 0 and col_off % KV_W == 0
    tiles_per_batch = seq // tm
    return pl.pallas_call(
        _v_proj_kernel,
        out_shape=[jax.ShapeDtypeStruct((m * N_KV_HEADS, HEAD_DIM), F32),
                   jax.ShapeDtypeStruct((m // seq, N_KV_HEADS, seq // MOBA_BLOCK, HEAD_DIM, MOBA_BLOCK), BF16)],
        grid=(m // tm,),
        in_specs=[pl.BlockSpec((tm, k), lambda i: (i, 0)),
                  _layer_spec(layer, (k, KV_W), lambda i: (0, col_off // KV_W))],
        out_specs=[pl.BlockSpec((tm * N_KV_HEADS, HEAD_DIM), lambda i: (i, 0)),
                   pl.BlockSpec((None, N_KV_HEADS, tm // MOBA_BLOCK, HEAD_DIM, MOBA_BLOCK),
                                lambda i: (i // tiles_per_batch, 0, i % tiles_per_batch, 0, 0))],
        compiler_params=_params("parallel"),
        name="v_proj",
    )(x, w)


_MOBA_KV_PER_STEP = 4


def _moba_prompt_kernel(q_ref, k_ref, vt_ref, km_ref, o_ref, sel_ref, m_ref, l_ref, acc_ref):
    i = pl.program_id(2)
    rep = N_HEADS // N_KV_HEADS
    heads = _MOBA_KV_PER_STEP * rep
    blk = MOBA_BLOCK
    scale_log2e = np.float32(HEAD_DIM ** -0.5 * np.log2(np.e))

    def cols(c):
        return slice(c * HEAD_DIM, (c + 1) * HEAD_DIM)

    qs = [q_ref[:, cols(c)] for c in range(heads)]
    blk_id = lax.broadcasted_iota(jnp.int32, (km_ref.shape[0], blk), 0)
    valid = blk_id < i
    for c in range(heads):
        km = km_ref[:, cols(c // rep)].astype(BF16)
        gate = jnp.where(valid, _dot_nt(km, qs[c]), MASKED)
        sel_ref[c] = _top_blocks(gate, valid.astype(F32), blk_id.astype(F32), 0)

    def attend(j, masks, first):
        kjs = [k_ref[pl.ds(pl.multiple_of(j * blk, blk), blk), cols(g)] for g in range(_MOBA_KV_PER_STEP)]
        scores = [_dot_nt(kjs[c // rep], qs[c]) for c in range(heads)]
        ps, m_news, l_blks = [], [], []
        for c in range(heads):
            s = scores[c] * scale_log2e
            if first:
                s = jnp.where(masks[c], s, MASKED)
                m_new = jnp.max(s, axis=0, keepdims=True)
                p = jnp.exp2(s - m_new)
            else:
                m_blk = jnp.where(masks[c], jnp.max(s, axis=0, keepdims=True), MASKED)
                m_new = jnp.maximum(m_ref[c], m_blk)
                p = jnp.exp2(s - jnp.where(masks[c], m_new, -MASKED))
            l_blks.append(jnp.sum(p, axis=0, keepdims=True))
            ps.append(p.astype(BF16))
            m_news.append(m_new)
        pvs = [_dot(vt_ref[c // rep, j], ps[c]) for c in range(heads)]
        for c in range(heads):
            if first:
                l_ref[c] = l_blks[c]
                acc_ref[c] = pvs[c]
            else:
                a = jnp.exp2(m_ref[c] - m_news[c])
                l_ref[c] = a * l_ref[c] + l_blks[c]
                acc_ref[c] = a * acc_ref[c] + pvs[c]
            m_ref[c] = m_news[c]

    key = lax.broadcasted_iota(jnp.int32, (blk, blk), 0)
    qry = lax.broadcasted_iota(jnp.int32, (blk, blk), 1)
    attend(i, [key <= qry] * heads, True)

    def body(j, carry):
        attend(j, [sel_ref[c, pl.ds(j, 1), :] > 0.5 for c in range(heads)], False)
        return carry

    lax.fori_loop(0, i, body, 0)
    for c in range(heads):
        o_ref[:, cols(c)] = (acc_ref[c] / l_ref[c]).T.astype(o_ref.dtype)


def _moba_prompt(q_b, k_b, vt, kmean, batch, seq):
    nq = seq // MOBA_BLOCK
    g = _MOBA_KV_PER_STEP
    heads = g * (N_HEADS // N_KV_HEADS)
    assert N_KV_HEADS % g == 0
    return pl.pallas_call(
        _moba_prompt_kernel,
        out_shape=jax.ShapeDtypeStruct((batch * seq, Q_W), BF16),
        grid=(batch, N_KV_HEADS // g, nq),
        in_specs=[
            pl.BlockSpec((MOBA_BLOCK, heads * HEAD_DIM), lambda b, gg, i: (b * nq + i, gg)),
            pl.BlockSpec((seq, g * HEAD_DIM), lambda b, gg, i: (b, gg)),
            pl.BlockSpec((None, g, nq, HEAD_DIM, MOBA_BLOCK), lambda b, gg, i: (b, gg, 0, 0, 0)),
            pl.BlockSpec((None, nq, g * HEAD_DIM), lambda b, gg, i: (b, 0, gg)),
        ],
        out_specs=pl.BlockSpec((MOBA_BLOCK, heads * HEAD_DIM), lambda b, gg, i: (b * nq + i, gg)),
        scratch_shapes=[pltpu.VMEM((heads, nq, MOBA_BLOCK), F32),
                        pltpu.VMEM((heads, 1, MOBA_BLOCK), F32),
                        pltpu.VMEM((heads, 1, MOBA_BLOCK), F32),
                        pltpu.VMEM((heads, HEAD_DIM, MOBA_BLOCK), F32)],
        compiler_params=_params("parallel", "parallel", "arbitrary"),
        name="moba_prompt",
    )(q_b, k_b, vt, kmean)


def _cached_moba_kernel(pt_ref, qkv_ref, ck_hbm, cv_hbm, o_ref, kbuf, vbuf, km_ref, ksem, vsem,
                        *, layer, n_pool, n_pages):
    b = pl.program_id(0)
    n = pl.num_programs(0)
    slot = b % 2
    rep = N_HEADS // N_KV_HEADS
    n_blocks = n_pages // PAGES_PER_BLOCK
    scale = HEAD_DIM ** -0.5
    base = layer * n_pool

    def k_copy(sample, page_slot, sl):
        page = base + pt_ref[sample * n_pages + page_slot]
        return pltpu.make_async_copy(ck_hbm.at[page], kbuf.at[sl, page_slot], ksem.at[sl])

    def start_keys(sample, sl):
        for p in range(n_pages):
            k_copy(sample, p, sl).start(priority=1)

    @pl.when(b == 0)
    def _():
        km_ref[...] = jnp.zeros_like(km_ref)
        start_keys(0, 0)

    @pl.when(b + 1 < n)
    def _():
        start_keys(b + 1, 1 - slot)

    for p in range(n_pages):
        k_copy(b, p, slot).wait()

    for blk in range(n_blocks):
        tot = jnp.zeros((SUBLANES, HEAD_DIM), F32)
        for r in range(PAGES_PER_BLOCK):
            page = kbuf[slot, PAGES_PER_BLOCK * blk + r]
            tot = tot + jnp.sum(page.reshape(PAGE_ROWS // SUBLANES, SUBLANES, HEAD_DIM), axis=0)
        km_ref[blk * SUBLANES:(blk + 1) * SUBLANES, :] = tot + pltpu.roll(tot, N_KV_HEADS, axis=0)

    q8 = _rows_to_sublanes(qkv_ref, N_HEADS, HEAD_DIM)
    q8b = q8.astype(BF16)
    hrow = lax.broadcasted_iota(jnp.int32, (N_HEADS, LANES), 0)
    lane = lax.broadcasted_iota(jnp.int32, (N_HEADS, LANES), 1)
    gate = jnp.zeros((N_HEADS, LANES), F32)
    for kvh in range(N_KV_HEADS):
        km = km_ref[pl.ds(kvh, LANES, stride=SUBLANES), :] * (1.0 / MOBA_BLOCK)
        gate = jnp.where(hrow >= kvh * rep, _dot_nt(q8b, km.astype(BF16)), gate)
    gate = jnp.where(lane < n_blocks, gate, MASKED)
    lane_f = lane.astype(F32)
    picks = []
    for _ in range(MOBA_TOPK):
        m = jnp.max(gate, axis=-1, keepdims=True)
        first = jnp.min(jnp.where(gate == m, lane_f, float(LANES)), axis=-1, keepdims=True)
        picks.append(first.astype(jnp.int32))
        gate = jnp.where(lane_f == first, -jnp.inf, gate)
    blocks = [[picks[t][h, 0] for t in range(MOBA_TOPK)] for h in range(N_HEADS)]

    def v_copy(h, t, r):
        page = pt_ref[b * n_pages + blocks[h][t] * PAGES_PER_BLOCK + r]
        return pltpu.make_async_copy(cv_hbm.at[layer, page, :, h // rep, :],
                                     vbuf.at[(h * MOBA_TOPK + t) * PAGES_PER_BLOCK + r], vsem)

    sel = [(h, t, r) for h in range(N_HEADS) for t in range(MOBA_TOPK) for r in range(PAGES_PER_BLOCK)]
    for h, t, r in sel:
        v_copy(h, t, r).start()

    k_new = qkv_ref[0, :, Q_W:Q_W + KV_W]
    v_new = qkv_ref[0, :, Q_W + KV_W:]
    scores = {}
    for h, t, r in sel:
        kvh = h // rep
        kp = kbuf[slot, blocks[h][t] * PAGES_PER_BLOCK + r, pl.ds(kvh, PAGE_SIZE, stride=N_KV_HEADS), :]
        qh = jnp.broadcast_to(q8b[h:h + 1], (SUBLANES, HEAD_DIM))
        scores[h, t, r] = _dot_nt(qh, kp.astype(BF16))[0:1] * scale
    probs, p_new = {}, []
    for h in range(N_HEADS):
        kvh = h // rep
        kn = k_new[:, kvh * HEAD_DIM:(kvh + 1) * HEAD_DIM].astype(BF16).astype(F32)
        s_new = jnp.sum(q8b[h:h + 1].astype(F32) * kn, axis=-1, keepdims=True) * scale
        mine = [scores[h, t, r] for t in range(MOBA_TOPK) for r in range(PAGES_PER_BLOCK)]
        m = s_new
        for s in mine:
            m = jnp.maximum(m, jnp.max(s, axis=-1, keepdims=True))
        es = [jnp.exp(s - m) for s in mine]
        e_new = jnp.exp(s_new - m)
        denom = e_new
        for e in es:
            denom = denom + jnp.sum(e, axis=-1, keepdims=True)
        p_new.append((e_new / denom).astype(BF16).astype(F32))
        for idx, (t, r) in enumerate((t, r) for t in range(MOBA_TOPK) for r in range(PAGES_PER_BLOCK)):
            probs[h, t, r] = (es[idx] / denom).astype(BF16)

    for h, t, r in sel:
        v_copy(h, t, r).wait()

    for h in range(N_HEADS):
        kvh = h // rep
        out = p_new[h] * v_new[:, kvh * HEAD_DIM:(kvh + 1) * HEAD_DIM].astype(BF16).astype(F32)
        for t in range(MOBA_TOPK):
            for r in range(PAGES_PER_BLOCK):
                p8 = jnp.broadcast_to(probs[h, t, r], (SUBLANES, PAGE_SIZE))
                vp = vbuf[(h * MOBA_TOPK + t) * PAGES_PER_BLOCK + r]
                out = out + _dot(p8, vp.astype(BF16))[0:1]
        o_ref[0, :, h * HEAD_DIM:(h + 1) * HEAD_DIM] = out


def _cached_moba(qkv_f3, cache_k_rows, cache_v, pt_flat, layer, n_pool, n_pages):
    n = qkv_f3.shape[0]
    assert SUBLANES == 2 * N_KV_HEADS and n_pages // PAGES_PER_BLOCK <= LANES
    return pl.pallas_call(
        functools.partial(_cached_moba_kernel, layer=layer, n_pool=n_pool, n_pages=n_pages),
        out_shape=jax.ShapeDtypeStruct((n, 1, Q_W), F32),
        grid_spec=pltpu.PrefetchScalarGridSpec(
            num_scalar_prefetch=1,
            grid=(n,),
            in_specs=[pl.BlockSpec((1, 1, QKV_W), lambda b, pt: (b, 0, 0)),
                      pl.BlockSpec(memory_space=pl.ANY),
                      pl.BlockSpec(memory_space=pl.ANY)],
            out_specs=pl.BlockSpec((1, 1, Q_W), lambda b, pt: (b, 0, 0)),
            scratch_shapes=[pltpu.VMEM((2, n_pages, PAGE_ROWS, HEAD_DIM), F32),
                            pltpu.VMEM((N_HEADS * MOBA_TOPK * PAGES_PER_BLOCK, PAGE_SIZE, HEAD_DIM), F32),
                            pltpu.VMEM((LANES * SUBLANES, HEAD_DIM), F32),
                            pltpu.SemaphoreType.DMA((2,)),
                            pltpu.SemaphoreType.DMA(())],
        ),
        compiler_params=_params("arbitrary"),
        name="cached_moba",
    )(pt_flat, qkv_f3, cache_k_rows, cache_v)


def _proj_act_kernel(x_ref, w_ref, *refs, act):
    o_ref = refs[-1]
    acc = _dot(x_ref[...], w_ref[...])
    if act == "gelu":
        o_ref[...] = _gelu(acc)
    elif act == "gelu_ln":
        o_ref[...] = _layer_norm(_gelu(acc), refs[0][...], refs[1][...])
    else:
        o_ref[...] = jax.nn.sigmoid(acc + refs[0][...])


def _proj_act(x, w, w_layer, col_off, ncols, act, vecs, layer, tm, tn):
    m, k = x.shape
    assert m % tm == 0 and ncols % tn == 0 and col_off % tn == 0
    assert act != "gelu_ln" or tn == ncols
    joff = col_off // tn
    return pl.pallas_call(
        functools.partial(_proj_act_kernel, act=act),
        out_shape=jax.ShapeDtypeStruct((m, ncols), F32),
        grid=(ncols // tn, m // tm),
        in_specs=[pl.BlockSpec((tm, k), lambda j, i: (i, 0)),
                  _layer_spec(w_layer, (k, tn), lambda j, i: (0, j + joff))]
        + [_layer_spec(layer, (1, tn), lambda j, i: (0, j)) for _ in vecs],
        out_specs=pl.BlockSpec((tm, tn), lambda j, i: (i, j)),
        compiler_params=_params("parallel", "parallel"),
        name="proj_" + act,
    )(x, w, *vecs)


def _sgu_mix_kernel(gu_ref, vn_ref, ws_ref, bs_ref, s_ref):
    t = SGU_CHUNK
    row = lax.broadcasted_iota(jnp.int32, (t, t), 0)
    col = lax.broadcasted_iota(jnp.int32, (t, t), 1)
    for g in range(SGU_GROUPS):
        cs = slice(g * SGU_GROUP_DIM, (g + 1) * SGU_GROUP_DIM)
        ws = jnp.where(col <= row, ws_ref[g], 0.0).astype(BF16)
        bias = bs_ref[:, g:g + 1]
        for c in range(gu_ref.shape[0] // t):
            rs = slice(c * t, (c + 1) * t)
            mixed = _dot(ws, vn_ref[rs, cs].astype(BF16)) + bias
            s_ref[rs, cs] = (gu_ref[rs, cs] * mixed).astype(s_ref.dtype)


def _prompt_mid_kernel(a_ref, gu_ref, vn_ref, ga_ref, gb_ref, x_ref, mk_ref, mv_ref,
                       ws_ref, bs_ref, wpa_ref, wpb_ref, wo_ref, g1_ref, b1_ref,
                       wxq_ref, wxo_ref, g2_ref, b2_ref, of_ref, ob_ref, s_scr, o_scr):
    _sgu_mix_kernel(gu_ref, vn_ref, ws_ref, bs_ref, s_scr)
    mix = ga_ref[...] * _dot(a_ref[...], wpa_ref[...]) + gb_ref[...] * _dot(s_scr[...], wpb_ref[...])
    x1 = _layer_norm(DN_ALPHA * x_ref[...] + _dot(mix.astype(BF16), wo_ref[...]), g1_ref[...], b1_ref[...])
    qx = _dot(x1.astype(BF16), wxq_ref[...]).astype(BF16)
    scale = X_HEAD_DIM ** -0.5
    for h in range(X_HEADS):
        cs = slice(h * X_HEAD_DIM, (h + 1) * X_HEAD_DIM)
        s = _dot_nt(qx[:, cs], mk_ref[:, cs]) * scale
        e = jnp.exp(s - jnp.max(s, axis=-1, keepdims=True))
        p = (e / jnp.sum(e, axis=-1, keepdims=True)).astype(BF16)
        o_scr[:, cs] = _dot(p, mv_ref[:, cs]).astype(o_scr.dtype)
    y = _layer_norm(DN_ALPHA * x1 + _dot(o_scr[...], wxo_ref[...]), g2_ref[...], b2_ref[...])
    of_ref[...] = y
    ob_ref[...] = y.astype(ob_ref.dtype)


def _prompt_mid(a, gu, vn, gates, x, mk, mv, w_s, b_s_t, w, w_xq, layer, seq, tm):
    m = a.shape[0]
    assert m % tm == 0 and seq % tm == 0 and tm % SGU_CHUNK == 0
    tiles_per_batch = seq // tm

    def const(block):
        return pl.BlockSpec((None,) + block, lambda i: (layer,) + (0,) * len(block), pipeline_mode=pl.Buffered(1))

    mem = pl.BlockSpec((None, N_MEM, X_W), lambda i: (i // tiles_per_batch, 0, 0))
    vec_d = const((1, D_MODEL))
    return pl.pallas_call(
        _prompt_mid_kernel,
        out_shape=[jax.ShapeDtypeStruct((m, D_MODEL), F32), jax.ShapeDtypeStruct((m, D_MODEL), BF16)],
        grid=(m // tm,),
        in_specs=[pl.BlockSpec((tm, Q_W), lambda i: (i, 0)),
                  pl.BlockSpec((tm, SGU_WIDTH), lambda i: (i, 0)),
                  pl.BlockSpec((tm, SGU_WIDTH), lambda i: (i, 0)),
                  pl.BlockSpec((tm, D_MODEL), lambda i: (i, 0)),
                  pl.BlockSpec((tm, D_MODEL), lambda i: (i, 1)),
                  pl.BlockSpec((tm, D_MODEL), lambda i: (i, 0)),
                  mem, mem,
                  const((SGU_GROUPS, SGU_CHUNK, SGU_CHUNK)), const((SGU_CHUNK, SGU_GROUPS)),
                  const((Q_W, D_MODEL)), const((SGU_WIDTH, D_MODEL)),
                  const((D_MODEL, D_MODEL)), vec_d, vec_d,
                  const((D_MODEL, X_W)), const((X_W, D_MODEL)), vec_d, vec_d],
        out_specs=[pl.BlockSpec((tm, D_MODEL), lambda i: (i, 0)), pl.BlockSpec((tm, D_MODEL), lambda i: (i, 0))],
        scratch_shapes=[pltpu.VMEM((tm, SGU_WIDTH), BF16), pltpu.VMEM((tm, X_W), BF16)],
        compiler_params=_params("parallel"),
        name="prompt_mid",
    )(a, gu, vn, gates, gates, x, mk, mv, w_s, b_s_t,
      w["w_pa"], w["w_pb"], w["w_o"], w["ln1_g"], w["ln1_b"],
      w_xq, w["w_xo"], w["ln2_g"], w["ln2_b"])


def _sgu_first_row_kernel(zu_ref, zv_ref, g_ref, b_ref, w0_ref, b0_ref, s_ref, vn_ref):
    vn = _layer_norm(_gelu(zv_ref[...]), g_ref[...], b_ref[...])
    vn_ref[...] = vn
    s_ref[...] = (_gelu(zu_ref[...]) * (vn * w0_ref[...] + b0_ref[...])).astype(s_ref.dtype)


def _sgu_first_rows(rest, ln_g, ln_b, w0, b0):
    m = rest.shape[0]
    vec = pl.BlockSpec((1, SGU_WIDTH), lambda i: (0, 0))
    return pl.pallas_call(
        _sgu_first_row_kernel,
        out_shape=[jax.ShapeDtypeStruct((m, SGU_WIDTH), BF16), jax.ShapeDtypeStruct((m, SGU_WIDTH), F32)],
        grid=(1,),
        in_specs=[pl.BlockSpec((m, SGU_WIDTH), lambda i: (0, 0)),
                  pl.BlockSpec((m, SGU_WIDTH), lambda i: (0, 1)), vec, vec, vec, vec],
        out_specs=[pl.BlockSpec((m, SGU_WIDTH), lambda i: (0, 0)), pl.BlockSpec((m, SGU_WIDTH), lambda i: (0, 0))],
        compiler_params=_params("arbitrary"),
        name="sgu_first_rows",
    )(rest, rest, ln_g, ln_b, w0, b0)


def _merge_kernel(a_ref, s_ref, ga_ref, gb_ref, bg_ref, wa_ref, wb_ref, o_ref):
    g_a = jax.nn.sigmoid(ga_ref[...] + bg_ref[0:1, :])
    g_b = jax.nn.sigmoid(gb_ref[...] + bg_ref[1:2, :])
    mix = g_a * _dot(a_ref[...], wa_ref[...]) + g_b * _dot(s_ref[...], wb_ref[...])
    o_ref[...] = mix.astype(o_ref.dtype)


def _merge(a, s, rest, b_gate, w_pa, w_pb, layer, tm, tn):
    m = a.shape[0]
    tm = min(tm, m)
    ga0 = 2 * SGU_WIDTH // tn
    gb0 = (2 * SGU_WIDTH + D_MODEL) // tn
    return pl.pallas_call(
        _merge_kernel,
        out_shape=jax.ShapeDtypeStruct((m, D_MODEL), BF16),
        grid=(D_MODEL // tn, m // tm),
        in_specs=[pl.BlockSpec((tm, Q_W), lambda j, i: (i, 0)),
                  pl.BlockSpec((tm, SGU_WIDTH), lambda j, i: (i, 0)),
                  pl.BlockSpec((tm, tn), lambda j, i: (i, ga0 + j)),
                  pl.BlockSpec((tm, tn), lambda j, i: (i, gb0 + j)),
                  _layer_spec(layer, (2, tn), lambda j, i: (0, j)),
                  _layer_spec(layer, (Q_W, tn), lambda j, i: (0, j)),
                  _layer_spec(layer, (SGU_WIDTH, tn), lambda j, i: (0, j))],
        out_specs=pl.BlockSpec((tm, tn), lambda j, i: (i, j)),
        compiler_params=_params("parallel", "parallel"),
        name="merge",
    )(a, s, rest, rest, b_gate, w_pa, w_pb)


def _proj_ln_kernel(a_ref, w_ref, x_ref, g_ref, b_ref, of_ref, ob_ref):
    z = DN_ALPHA * x_ref[...] + _dot(a_ref[...], w_ref[...])
    y = _layer_norm(z, g_ref[...], b_ref[...])
    of_ref[...] = y
    ob_ref[...] = y.astype(ob_ref.dtype)


def _proj_ln(a, w, x, g, b, layer, tm):
    m, k = a.shape
    tm = min(tm, m)
    vec = _layer_spec(layer, (1, D_MODEL), lambda i: (0, 0))
    return pl.pallas_call(
        _proj_ln_kernel,
        out_shape=[jax.ShapeDtypeStruct((m, D_MODEL), F32), jax.ShapeDtypeStruct((m, D_MODEL), BF16)],
        grid=(m // tm,),
        in_specs=[pl.BlockSpec((tm, k), lambda i: (i, 0)),
                  _layer_spec(layer, (k, D_MODEL), lambda i: (0, 0)),
                  pl.BlockSpec((tm, D_MODEL), lambda i: (i, 0)), vec, vec],
        out_specs=[pl.BlockSpec((tm, D_MODEL), lambda i: (i, 0)), pl.BlockSpec((tm, D_MODEL), lambda i: (i, 0))],
        compiler_params=_params("parallel"),
        name="proj_ln",
    )(a, w, x, g, b)


_XATTN_SAMPLES_PER_STEP = 4


def _xattn_single_kernel(q_ref, mk_ref, mv_ref, o_ref):
    scale = X_HEAD_DIM ** -0.5
    rows = N_MEM * X_HEADS
    head = lax.broadcasted_iota(jnp.int32, (SUBLANES, rows), 0)
    row_head = jnp.bitwise_and(lax.broadcasted_iota(jnp.int32, (SUBLANES, rows), 1), X_HEADS - 1)
    for b in range(q_ref.shape[0]):
        q8 = _rows_to_sublanes(q_ref, X_HEADS, X_HEAD_DIM, b).astype(BF16)
        s = jnp.where(row_head == head, _dot_nt(q8, mk_ref[b].astype(BF16)) * scale, MASKED)
        e = jnp.exp(s - jnp.max(s, axis=-1, keepdims=True))
        p = (e / jnp.sum(e, axis=-1, keepdims=True)).astype(BF16)
        o = _dot(p, mv_ref[b].astype(BF16))
        for h in range(X_HEADS):
            o_ref[b, :, h * X_HEAD_DIM:(h + 1) * X_HEAD_DIM] = o[h:h + 1].astype(o_ref.dtype)


def _xattn_single(q3, mk_rows, mv_rows, mem_off):
    n = q3.shape[0]
    rows = N_MEM * X_HEADS
    per = _XATTN_SAMPLES_PER_STEP
    assert n % per == 0 and mem_off % per == 0
    return pl.pallas_call(
        _xattn_single_kernel,
        out_shape=jax.ShapeDtypeStruct((n, 1, X_W), F32),
        grid=(n // per,),
        in_specs=[pl.BlockSpec((per, 1, X_W), lambda b: (b, 0, 0)),
                  pl.BlockSpec((per, rows, X_HEAD_DIM), lambda b: (mem_off // per + b, 0, 0)),
                  pl.BlockSpec((per, rows, X_HEAD_DIM), lambda b: (mem_off // per + b, 0, 0))],
        out_specs=pl.BlockSpec((per, 1, X_W), lambda b: (b, 0, 0)),
        compiler_params=_params("parallel"),
        name="xattn_single",
    )(q3, mk_rows, mv_rows)


def _mlp_kernel(xb_ref, xf_ref, wu_ref, wd_ref, g_ref, b_ref, of_ref, ob_ref):
    f = pl.program_id(1)

    @pl.when(f == 0)
    def _():
        of_ref[...] = jnp.zeros_like(of_ref)

    h = jnp.maximum(_dot(xb_ref[...], wu_ref[...]), 0.0)
    of_ref[...] += _dot((h * h).astype(BF16), wd_ref[...])

    @pl.when(f == pl.num_programs(1) - 1)
    def _():
        y = _layer_norm(DN_ALPHA * xf_ref[...] + of_ref[...], g_ref[...], b_ref[...])
        of_ref[...] = y
        ob_ref[...] = y.astype(ob_ref.dtype)


def _mlp_rows_kernel(xb_ref, xf_hbm, wu_ref, wd_ref, g_ref, b_ref, of_ref, ob_ref, xf_buf, sem):
    i, f = pl.program_id(0), pl.program_id(1)
    tm = xb_ref.shape[0]
    residual = pltpu.make_async_copy(xf_hbm.at[pl.ds(pl.multiple_of(i * tm, tm), tm), :], xf_buf, sem)

    @pl.when(f == 0)
    def _():
        residual.start()
        of_ref[...] = jnp.zeros_like(of_ref)

    h = jnp.maximum(_dot(xb_ref[...], wu_ref[...]), 0.0)
    of_ref[...] += _dot((h * h).astype(BF16), wd_ref[...])

    @pl.when(f == pl.num_programs(1) - 1)
    def _():
        residual.wait()
        y = _layer_norm(DN_ALPHA * xf_buf[...] + of_ref[...], g_ref[...], b_ref[...])
        of_ref[...] = y
        ob_ref[...] = y.astype(ob_ref.dtype)


def _mlp(xb, xf, w_up, w_down, w_layer, g, b, layer, tm, tf):
    m = xb.shape[0]
    assert m % tm == 0 and D_FF // tf >= 2
    vec = _layer_spec(layer, (1, D_MODEL), lambda i, f: (0, 0))
    return pl.pallas_call(
        _mlp_rows_kernel,
        out_shape=[jax.ShapeDtypeStruct((m, D_MODEL), F32), jax.ShapeDtypeStruct((m, D_MODEL), BF16)],
        grid=(m // tm, D_FF // tf),
        in_specs=[pl.BlockSpec((tm, D_MODEL), lambda i, f: (i, 0)),
                  pl.BlockSpec(memory_space=pl.ANY),
                  _layer_spec(w_layer, (D_MODEL, tf), lambda i, f: (0, f)),
                  _layer_spec(w_layer, (tf, D_MODEL), lambda i, f: (f, 0)), vec, vec],
        out_specs=[pl.BlockSpec((tm, D_MODEL), lambda i, f: (i, 0)),
                   pl.BlockSpec((tm, D_MODEL), lambda i, f: (i, 0), pipeline_mode=pl.Buffered(1))],
        scratch_shapes=[pltpu.VMEM((tm, D_MODEL), F32), pltpu.SemaphoreType.DMA(())],
        compiler_params=_params("parallel", "arbitrary"),
        name="mlp",
    )(xb, xf, w_up, w_down, g, b)


def _mlp_cast_kernel(xb_ref, xf_ref, wu_ref, wd_ref, g_ref, b_ref, of_ref, ob_ref, wub_ref, wdb_ref):
    wub_ref[...] = wu_ref[...].astype(BF16)
    wdb_ref[...] = wd_ref[...].astype(BF16)
    _mlp_kernel(xb_ref, xf_ref, wub_ref, wdb_ref, g_ref, b_ref, of_ref, ob_ref)


def _mlp_casting(xb, xf, w_up_f32, w_down_f32, g, b, layer, tf):
    m = xb.shape[0]
    vec = _layer_spec(layer, (1, D_MODEL), lambda i, f: (0, 0))
    rows = pl.BlockSpec((m, D_MODEL), lambda i, f: (0, 0))
    return pl.pallas_call(
        _mlp_cast_kernel,
        out_shape=[jax.ShapeDtypeStruct((m, D_MODEL), F32), jax.ShapeDtypeStruct((m, D_MODEL), BF16),
                   jax.ShapeDtypeStruct((1, D_MODEL, D_FF), BF16), jax.ShapeDtypeStruct((1, D_FF, D_MODEL), BF16)],
        grid=(1, D_FF // tf),
        in_specs=[rows, rows,
                  _layer_spec(layer, (D_MODEL, tf), lambda i, f: (0, f)),
                  _layer_spec(layer, (tf, D_MODEL), lambda i, f: (f, 0)), vec, vec],
        out_specs=[rows, rows,
                   pl.BlockSpec((None, D_MODEL, tf), lambda i, f: (0, 0, f)),
                   pl.BlockSpec((None, tf, D_MODEL), lambda i, f: (0, f, 0))],
        compiler_params=_params("parallel", "arbitrary"),
        name="mlp_casting",
    )(xb, xf, w_up_f32, w_down_f32, g, b)


_MLP_TM = 1024
_MLP_TF = 512
_MLP_CAST_TF = 512
_MID_TM = 256
_PROJ_TM = 1024
_PROJ_SMALL_TM = 1024


def kernel(x_prompt, x_sample, mem_prompt, cache_k, cache_v, cache_mem_k, cache_mem_v, page_table,
           w_in, b_gate, sgu_ln_g, sgu_ln_b, w_s, b_s, w_pa, w_pb, w_o, ln1_g, ln1_b,
           w_xq, w_xk, w_xv, w_xo, ln2_g, ln2_b, w_up, w_down, ln3_g, ln3_b):
    batch, seq, _ = x_prompt.shape
    n_samples, dec_seq, _ = x_sample.shape
    assert dec_seq == 1 and seq % MOBA_BLOCK == 0
    assert N_KV_HEADS & (N_KV_HEADS - 1) == 0 and X_HEADS & (X_HEADS - 1) == 0 and X_HEADS <= SUBLANES
    n_pool = cache_k.shape[1]
    n_pages = page_table.shape[1]
    n_blocks = seq // MOBA_BLOCK
    assert n_pages % PAGES_PER_BLOCK == 0
    mp, ms = batch * seq, n_samples

    pt_flat = page_table.reshape(-1).astype(jnp.int32)
    cache_k_rows = cache_k.reshape(DEPTH * n_pool, PAGE_ROWS, HEAD_DIM)
    cache_mk_rows = cache_mem_k.reshape(DEPTH * n_samples, N_MEM * X_HEADS, X_HEAD_DIM)
    cache_mv_rows = cache_mem_v.reshape(DEPTH * n_samples, N_MEM * X_HEADS, X_HEAD_DIM)
    mem_b = mem_prompt.reshape(batch * N_MEM, D_MODEL).astype(BF16)

    xpf = x_prompt.reshape(mp, D_MODEL)
    xsf = x_sample.reshape(ms, D_MODEL)
    xpb, xsb = xpf.astype(BF16), xsf.astype(BF16)

    w = dict(
        b_gate=b_gate, w_pa=_to_bf16(w_pa), w_pb=_to_bf16(w_pb), w_o=_to_bf16(w_o), w_xo=_to_bf16(w_xo),
        ln1_g=ln1_g[:, None], ln1_b=ln1_b[:, None], ln2_g=ln2_g[:, None], ln2_b=ln2_b[:, None],
        ln3_g=ln3_g[:, None], ln3_b=ln3_b[:, None])
    w_xq_b, w_xk_b, w_xv_b = _to_bf16(w_xq), _to_bf16(w_xk), _to_bf16(w_xv)

    sgu_g3, sgu_b3 = sgu_ln_g[:, None], sgu_ln_b[:, None]
    b_gate_row = b_gate.reshape(DEPTH, 1, 2 * D_MODEL)
    b_s_t = jnp.swapaxes(b_s, 1, 2)

    kp_l, vp_l, mkp_l, mvp_l, ks_l, vs_l, vns_l = [], [], [], [], [], [], []
    for l in range(DEPTH):
        h_s, w_in_b = _matmul_casting(xsb, w_in, l, 1024)
        qkv_s, rest_s = h_s[:, :QKV_W], h_s[:, QKV_W:]
        qkv_s3 = qkv_s.reshape(ms, 1, QKV_W)
        a_s = _cached_moba(qkv_s3, cache_k_rows, cache_v, pt_flat, l, n_pool, n_pages)
        a_s = a_s.reshape(ms, Q_W).astype(BF16)
        w0 = jnp.repeat(w_s[l][:, 0, 0], SGU_GROUP_DIM)[None]
        b0 = jnp.repeat(b_s[l][:, 0], SGU_GROUP_DIM)[None]
        s_s, vn_s = _sgu_first_rows(rest_s, sgu_ln_g[l][None], sgu_ln_b[l][None], w0, b0)
        mix_s = _merge(a_s, s_s, rest_s, w["b_gate"], w["w_pa"], w["w_pb"], l, ms, 1024)
        x1f, x1b = _proj_ln(mix_s, w["w_o"], xsf, w["ln1_g"], w["ln1_b"], l, ms)
        (qx,) = _matmul(x1b, w_xq_b, l, 0, X_W, (F32,), ms, X_W)
        o_s = _xattn_single(qx.reshape(ms, 1, X_W), cache_mk_rows, cache_mv_rows, l * n_samples)
        x2f, x2b = _proj_ln(o_s.reshape(ms, X_W).astype(BF16), w["w_xo"], x1f, w["ln2_g"], w["ln2_b"], l, ms)
        xsf, xsb, w_up_b, w_down_b = _mlp_casting(x2b, x2f, w_up, w_down, w["ln3_g"], w["ln3_b"], l, _MLP_CAST_TF)
        ks_l.append(qkv_s[:, Q_W:Q_W + KV_W].reshape(ms, 1, N_KV_HEADS, HEAD_DIM))
        vs_l.append(qkv_s[:, Q_W + KV_W:].reshape(ms, 1, N_KV_HEADS, HEAD_DIM))
        vns_l.append(vn_s.reshape(ms, 1, SGU_WIDTH))

        mk_f, mk_b = _matmul(mem_b, w_xk_b, l, 0, X_W, (F32, BF16), 512, X_W)
        mv_f, mv_b = _matmul(mem_b, w_xv_b, l, 0, X_W, (F32, BF16), 512, X_W)
        (q_b,) = _matmul(xpb, w_in_b, 0, 0, Q_W, (BF16,), _PROJ_SMALL_TM, Q_W)
        k_rows, k_b, kmean = _kv_proj(xpb, w_in_b, 0, Q_W, _PROJ_SMALL_TM)
        v_rows, vt = _v_proj(xpb, w_in_b, 0, Q_W + KV_W, seq, _PROJ_SMALL_TM)
        gu = _proj_act(xpb, w_in_b, 0, QKV_W, SGU_WIDTH, "gelu", (), l, _PROJ_SMALL_TM, SGU_WIDTH)
        vn = _proj_act(xpb, w_in_b, 0, QKV_W + SGU_WIDTH, SGU_WIDTH, "gelu_ln", (sgu_g3, sgu_b3), l,
                       _PROJ_SMALL_TM, SGU_WIDTH)
        gates = _proj_act(xpb, w_in_b, 0, QKV_W + 2 * SGU_WIDTH, 2 * D_MODEL, "gate", (b_gate_row,), l,
                          _PROJ_TM, 1024)
        a = _moba_prompt(q_b, k_b, vt, kmean.reshape(batch, n_blocks, KV_W), batch, seq)
        x2f, x2b = _prompt_mid(a, gu, vn, gates, xpf, mk_b.reshape(batch, N_MEM, X_W),
                               mv_b.reshape(batch, N_MEM, X_W), w_s, b_s_t, w, w_xq_b, l, seq, _MID_TM)
        xpf, xpb = _mlp(x2b, x2f, w_up_b, w_down_b, 0, w["ln3_g"], w["ln3_b"], l, _MLP_TM, _MLP_TF)
        kp_l.append(k_rows.reshape(batch, seq // PAGE_SIZE, PAGE_SIZE, N_KV_HEADS, HEAD_DIM))
        vp_l.append(v_rows.reshape(batch, seq // PAGE_SIZE, PAGE_SIZE, N_KV_HEADS, HEAD_DIM))
        mkp_l.append(mk_f.reshape(batch, N_MEM, X_HEADS, X_HEAD_DIM))
        mvp_l.append(mv_f.reshape(batch, N_MEM, X_HEADS, X_HEAD_DIM))

    return (xpf.reshape(batch, seq, D_MODEL), xsf.reshape(ms, 1, D_MODEL),
            jnp.stack(kp_l), jnp.stack(vp_l), jnp.stack(mkp_l), jnp.stack(mvp_l),
            jnp.stack(ks_l), jnp.stack(vs_l), jnp.stack(vns_l))
```

```python
import functools

import jax
import jax.numpy as jnp
import numpy as np
from jax import lax
from jax.experimental import pallas as pl
from jax.experimental.pallas import tpu as pltpu

D_MODEL = 2048
DEPTH = 2
PAGE_SIZE = 128
N_HEADS = 8
N_KV_HEADS = 4
HEAD_DIM = 128
MOBA_BLOCK = 256
MOBA_TOPK = 3
SGU_WIDTH = 1024
SGU_GROUPS = 8
SGU_GROUP_DIM = SGU_WIDTH // SGU_GROUPS
SGU_CHUNK = 128
N_MEM = 256
X_HEADS = 4
X_HEAD_DIM = 128
D_FF = 4 * D_MODEL
DN_ALPHA = (2 * DEPTH) ** 0.25
LN_EPS = 1e-5
Q_W = N_HEADS * HEAD_DIM
KV_W = N_KV_HEADS * HEAD_DIM
X_W = X_HEADS * X_HEAD_DIM
QKV_W = Q_W + 2 * KV_W
PAGES_PER_BLOCK = MOBA_BLOCK // PAGE_SIZE
PAGE_ROWS = PAGE_SIZE * N_KV_HEADS
MASKED = -1e30

LANES = 128
SUBLANES = 8
VMEM_LIMIT = 56 * 1024 * 1024

BF16 = jnp.bfloat16
F32 = jnp.float32
_NT = (((1,), (1,)), ((), ()))


def _params(*sem):
    return pltpu.CompilerParams(dimension_semantics=sem, vmem_limit_bytes=VMEM_LIMIT)


def _dot(a, b):
    return jnp.dot(a, b, preferred_element_type=F32)


def _dot_nt(a, b):
    return lax.dot_general(a, b, _NT, preferred_element_type=F32)


def _gelu(x):
    c = np.float32(np.sqrt(2 / np.pi))
    return x * (0.5 * (1.0 + jnp.tanh(c * (x + 0.044715 * (x * x * x)))))


def _layer_norm(z, g, b):
    mu = jnp.mean(z, axis=-1, keepdims=True)
    d = z - mu
    var = jnp.mean(d * d, axis=-1, keepdims=True)
    return d * lax.rsqrt(var + LN_EPS) * g + b


def _top_blocks(gate, valid_f, idx_f, axis):
    sel = jnp.zeros(gate.shape, F32)
    for _ in range(MOBA_TOPK):
        m = jnp.max(gate, axis=axis, keepdims=True)
        first = jnp.min(jnp.where(gate == m, idx_f, float(gate.shape[axis])), axis=axis, keepdims=True)
        pick = idx_f == first
        sel = jnp.where(pick, valid_f, sel)
        gate = jnp.where(pick, -jnp.inf, gate)
    return sel


def _rows_to_sublanes(ref, n_rows, width, lead=0):
    sub = lax.broadcasted_iota(jnp.int32, (SUBLANES, width), 0)
    out = jnp.zeros((SUBLANES, width), F32)
    for r in range(n_rows):
        out = jnp.where(sub == r, ref[lead, :, r * width:(r + 1) * width].astype(F32), out)
    return out


def _mm_kernel(x_ref, w_ref, *out_refs):
    acc = _dot(x_ref[...], w_ref[...])
    for o in out_refs:
        o[...] = acc.astype(o.dtype)


def _cast_kernel(x_ref, o_ref):
    o_ref[...] = x_ref[...].astype(o_ref.dtype)


_CAST_BLOCK_BYTES = 4 * 1024 * 1024


def _to_bf16(w):
    d, k, n = w.shape
    rows = d * k
    tr = min(rows, max(SUBLANES, _CAST_BLOCK_BYTES // (4 * n)))
    assert rows % tr == 0
    out = pl.pallas_call(
        _cast_kernel,
        out_shape=jax.ShapeDtypeStruct((rows, n), BF16),
        grid=(rows // tr,),
        in_specs=[pl.BlockSpec((tr, n), lambda i: (i, 0))],
        out_specs=pl.BlockSpec((tr, n), lambda i: (i, 0)),
        compiler_params=_params("parallel"),
        name="to_bf16",
    )(w.reshape(rows, n))
    return out.reshape(d, k, n)


def _layer_spec(layer, block, index):
    return pl.BlockSpec((None,) + block, lambda *g: (layer,) + index(*g))


def _matmul(x, w, layer, col_off, ncols, out_dtypes, tm, tn):
    m, k = x.shape
    tm = min(tm, m)
    tn = min(tn, ncols)
    assert m % tm == 0 and ncols % tn == 0 and col_off % tn == 0
    joff = col_off // tn
    outs = pl.pallas_call(
        _mm_kernel,
        out_shape=[jax.ShapeDtypeStruct((m, ncols), dt) for dt in out_dtypes],
        grid=(ncols // tn, m // tm),
        in_specs=[pl.BlockSpec((tm, k), lambda j, i: (i, 0)),
                  _layer_spec(layer, (k, tn), lambda j, i: (0, j + joff))],
        out_specs=[pl.BlockSpec((tm, tn), lambda j, i: (i, j)) for _ in out_dtypes],
        compiler_params=_params("parallel", "parallel"),
        name="matmul",
    )(x, w)
    return outs


def _mm_cast_kernel(x_ref, w_ref, o_ref, wb_ref):
    wb = w_ref[...].astype(BF16)
    wb_ref[...] = wb
    o_ref[...] = _dot(x_ref[...], wb)


def _matmul_casting(x, w_f32, layer, tn):
    m, k = x.shape
    n = w_f32.shape[2]
    assert n % tn == 0
    return pl.pallas_call(
        _mm_cast_kernel,
        out_shape=[jax.ShapeDtypeStruct((m, n), F32), jax.ShapeDtypeStruct((1, k, n), BF16)],
        grid=(n // tn,),
        in_specs=[pl.BlockSpec((m, k), lambda j: (0, 0)),
                  _layer_spec(layer, (k, tn), lambda j: (0, j))],
        out_specs=[pl.BlockSpec((m, tn), lambda j: (0, j)),
                   pl.BlockSpec((None, k, tn), lambda j: (0, 0, j))],
        compiler_params=_params("parallel"),
        name="matmul_casting",
    )(x, w_f32)


def _kv_proj_kernel(x_ref, w_ref, rows_ref, ob_ref, km_ref):
    acc = _dot(x_ref[...], w_ref[...])
    tm = acc.shape[0]
    for h in range(N_KV_HEADS):
        rows_ref[pl.ds(h, tm, stride=N_KV_HEADS), :] = acc[:, h * HEAD_DIM:(h + 1) * HEAD_DIM]
    ob_ref[...] = acc.astype(ob_ref.dtype)
    for c in range(tm // MOBA_BLOCK):
        blk = acc[c * MOBA_BLOCK:(c + 1) * MOBA_BLOCK]
        km_ref[c] = jnp.sum(blk, axis=0, keepdims=True) * (1.0 / MOBA_BLOCK)


def _kv_proj(x, w, layer, col_off, tm):
    m, k = x.shape
    assert m % tm == 0 and tm % MOBA_BLOCK == 0 and col_off % KV_W == 0
    return pl.pallas_call(
        _kv_proj_kernel,
        out_shape=[jax.ShapeDtypeStruct((m * N_KV_HEADS, HEAD_DIM), F32),
                   jax.ShapeDtypeStruct((m, KV_W), BF16),
                   jax.ShapeDtypeStruct((m // MOBA_BLOCK, 1, KV_W), F32)],
        grid=(m // tm,),
        in_specs=[pl.BlockSpec((tm, k), lambda i: (i, 0)),
                  _layer_spec(layer, (k, KV_W), lambda i: (0, col_off // KV_W))],
        out_specs=[pl.BlockSpec((tm * N_KV_HEADS, HEAD_DIM), lambda i: (i, 0)),
                   pl.BlockSpec((tm, KV_W), lambda i: (i, 0)),
                   pl.BlockSpec((tm // MOBA_BLOCK, 1, KV_W), lambda i: (i, 0, 0))],
        compiler_params=_params("parallel"),
        name="kv_proj",
    )(x, w)


def _v_proj_kernel(x_ref, w_ref, rows_ref, vt_ref):
    acc = _dot(x_ref[...], w_ref[...])
    tm = acc.shape[0]
    for h in range(N_KV_HEADS):
        vh = acc[:, h * HEAD_DIM:(h + 1) * HEAD_DIM]
        rows_ref[pl.ds(h, tm, stride=N_KV_HEADS), :] = vh
        for c in range(tm // MOBA_BLOCK):
            vt_ref[h, c] = vh[c * MOBA_BLOCK:(c + 1) * MOBA_BLOCK].T.astype(vt_ref.dtype)


def _v_proj(x, w, layer, col_off, seq, tm):
    m, k = x.shape
    assert m % tm == 0 and seq % tm == 0 and tm % MOBA_BLOCK == 0 and col_off % KV_W == 0
    tiles_per_batch = seq // tm
    return pl.pallas_call(
        _v_proj_kernel,
        out_shape=[jax.ShapeDtypeStruct((m * N_KV_HEADS, HEAD_DIM), F32),
                   jax.ShapeDtypeStruct((m // seq, N_KV_HEADS, seq // MOBA_BLOCK, HEAD_DIM, MOBA_BLOCK), BF16)],
        grid=(m // tm,),
        in_specs=[pl.BlockSpec((tm, k), lambda i: (i, 0)),
                  _layer_spec(layer, (k, KV_W), lambda i: (0, col_off // KV_W))],
        out_specs=[pl.BlockSpec((tm * N_KV_HEADS, HEAD_DIM), lambda i: (i, 0)),
                   pl.BlockSpec((None, N_KV_HEADS, tm // MOBA_BLOCK, HEAD_DIM, MOBA_BLOCK),
                                lambda i: (i // tiles_per_batch, 0, i % tiles_per_batch, 0, 0))],
        compiler_params=_params("parallel"),
        name="v_proj",
    )(x, w)


_MOBA_KV_PER_STEP = 4


def _moba_prompt_kernel(q_ref, k_ref, vt_ref, km_ref, o_ref, sel_ref, m_ref, l_ref, acc_ref):
    i = pl.program_id(2)
    rep = N_HEADS // N_KV_HEADS
    heads = _MOBA_KV_PER_STEP * rep
    blk = MOBA_BLOCK
    scale_log2e = np.float32(HEAD_DIM ** -0.5 * np.log2(np.e))

    def cols(c):
        return slice(c * HEAD_DIM, (c + 1) * HEAD_DIM)

    qs = [q_ref[:, cols(c)] for c in range(heads)]
    blk_id = lax.broadcasted_iota(jnp.int32, (km_ref.shape[0], blk), 0)
    valid = blk_id < i
    for c in range(heads):
        km = km_ref[:, cols(c // rep)].astype(BF16)
        gate = jnp.where(valid, _dot_nt(km, qs[c]), MASKED)
        sel_ref[c] = _top_blocks(gate, valid.astype(F32), blk_id.astype(F32), 0)

    def attend(j, masks, first):
        kjs = [k_ref[pl.ds(pl.multiple_of(j * blk, blk), blk), cols(g)] for g in range(_MOBA_KV_PER_STEP)]
        scores = [_dot_nt(kjs[c // rep], qs[c]) for c in range(heads)]
        ps, m_news, l_blks = [], [], []
        for c in range(heads):
            s = scores[c] * scale_log2e
            if first:
                s = jnp.where(masks[c], s, MASKED)
                m_new = jnp.max(s, axis=0, keepdims=True)
                p = jnp.exp2(s - m_new)
            else:
                m_blk = jnp.where(masks[c], jnp.max(s, axis=0, keepdims=True), MASKED)
                m_new = jnp.maximum(m_ref[c], m_blk)
                p = jnp.exp2(s - jnp.where(masks[c], m_new, -MASKED))
            l_blks.append(jnp.sum(p, axis=0, keepdims=True))
            ps.append(p.astype(BF16))
            m_news.append(m_new)
        pvs = [_dot(vt_ref[c // rep, j], ps[c]) for c in range(heads)]
        for c in range(heads):
            if first:
                l_ref[c] = l_blks[c]
                acc_ref[c] = pvs[c]
            else:
                a = jnp.exp2(m_ref[c] - m_news[c])
                l_ref[c] = a * l_ref[c] + l_blks[c]
                acc_ref[c] = a * acc_ref[c] + pvs[c]
            m_ref[c] = m_news[c]

    key = lax.broadcasted_iota(jnp.int32, (blk, blk), 0)
    qry = lax.broadcasted_iota(jnp.int32, (blk, blk), 1)
    attend(i, [key <= qry] * heads, True)

    def body(j, carry):
        attend(j, [sel_ref[c, pl.ds(j, 1), :] > 0.5 for c in range(heads)], False)
        return carry

    lax.fori_loop(0, i, body, 0)
    for c in range(heads):
        o_ref[:, cols(c)] = (acc_ref[c] / l_ref[c]).T.astype(o_ref.dtype)


def _moba_prompt(q_b, k_b, vt, kmean, batch, seq):
    nq = seq // MOBA_BLOCK
    g = _MOBA_KV_PER_STEP
    heads = g * (N_HEADS // N_KV_HEADS)
    assert N_KV_HEADS % g == 0
    return pl.pallas_call(
        _moba_prompt_kernel,
        out_shape=jax.ShapeDtypeStruct((batch * seq, Q_W), BF16),
        grid=(batch, N_KV_HEADS // g, nq),
        in_specs=[
            pl.BlockSpec((MOBA_BLOCK, heads * HEAD_DIM), lambda b, gg, i: (b * nq + i, gg)),
            pl.BlockSpec((seq, g * HEAD_DIM), lambda b, gg, i: (b, gg)),
            pl.BlockSpec((None, g, nq, HEAD_DIM, MOBA_BLOCK), lambda b, gg, i: (b, gg, 0, 0, 0)),
            pl.BlockSpec((None, nq, g * HEAD_DIM), lambda b, gg, i: (b, 0, gg)),
        ],
        out_specs=pl.BlockSpec((MOBA_BLOCK, heads * HEAD_DIM), lambda b, gg, i: (b * nq + i, gg)),
        scratch_shapes=[pltpu.VMEM((heads, nq, MOBA_BLOCK), F32),
                        pltpu.VMEM((heads, 1, MOBA_BLOCK), F32),
                        pltpu.VMEM((heads, 1, MOBA_BLOCK), F32),
                        pltpu.VMEM((heads, HEAD_DIM, MOBA_BLOCK), F32)],
        compiler_params=_params("parallel", "parallel", "arbitrary"),
        name="moba_prompt",
    )(q_b, k_b, vt, kmean)


def _cached_moba_kernel(pt_ref, qkv_ref, ck_hbm, cv_hbm, o_ref, kbuf, vbuf, km_ref, ksem, vsem,
                        *, layer, n_pool, n_pages):
    b = pl.program_id(0)
    n = pl.num_programs(0)
    slot = b % 2
    rep = N_HEADS // N_KV_HEADS
    n_blocks = n_pages // PAGES_PER_BLOCK
    scale = HEAD_DIM ** -0.5
    base = layer * n_pool

    def k_copy(sample, page_slot, sl):
        page = base + pt_ref[sample * n_pages + page_slot]
        return pltpu.make_async_copy(ck_hbm.at[page], kbuf.at[sl, page_slot], ksem.at[sl])

    def start_keys(sample, sl):
        for p in range(n_pages):
            k_copy(sample, p, sl).start(priority=1)

    @pl.when(b == 0)
    def _():
        km_ref[...] = jnp.zeros_like(km_ref)
        start_keys(0, 0)

    @pl.when(b + 1 < n)
    def _():
        start_keys(b + 1, 1 - slot)

    for p in range(n_pages):
        k_copy(b, p, slot).wait()

    for blk in range(n_blocks):
        tot = jnp.zeros((SUBLANES, HEAD_DIM), F32)
        for r in range(PAGES_PER_BLOCK):
            page = kbuf[slot, PAGES_PER_BLOCK * blk + r]
            tot = tot + jnp.sum(page.reshape(PAGE_ROWS // SUBLANES, SUBLANES, HEAD_DIM), axis=0)
        km_ref[blk * SUBLANES:(blk + 1) * SUBLANES, :] = tot + pltpu.roll(tot, N_KV_HEADS, axis=0)

    q8 = _rows_to_sublanes(qkv_ref, N_HEADS, HEAD_DIM)
    q8b = q8.astype(BF16)
    hrow = lax.broadcasted_iota(jnp.int32, (N_HEADS, LANES), 0)
    lane = lax.broadcasted_iota(jnp.int32, (N_HEADS, LANES), 1)
    gate = jnp.zeros((N_HEADS, LANES), F32)
    for kvh in range(N_KV_HEADS):
        km = km_ref[pl.ds(kvh, LANES, stride=SUBLANES), :] * (1.0 / MOBA_BLOCK)
        gate = jnp.where(hrow >= kvh * rep, _dot_nt(q8b, km.astype(BF16)), gate)
    gate = jnp.where(lane < n_blocks, gate, MASKED)
    lane_f = lane.astype(F32)
    picks = []
    for _ in range(MOBA_TOPK):
        m = jnp.max(gate, axis=-1, keepdims=True)
        first = jnp.min(jnp.where(gate == m, lane_f, float(LANES)), axis=-1, keepdims=True)
        picks.append(first.astype(jnp.int32))
        gate = jnp.where(lane_f == first, -jnp.inf, gate)
    blocks = [[picks[t][h, 0] for t in range(MOBA_TOPK)] for h in range(N_HEADS)]

    def v_copy(h, t, r):
        page = pt_ref[b * n_pages + blocks[h][t] * PAGES_PER_BLOCK + r]
        return pltpu.make_async_copy(cv_hbm.at[layer, page, :, h // rep, :],
                                     vbuf.at[(h * MOBA_TOPK + t) * PAGES_PER_BLOCK + r], vsem)

    sel = [(h, t, r) for h in range(N_HEADS) for t in range(MOBA_TOPK) for r in range(PAGES_PER_BLOCK)]
    for h, t, r in sel:
        v_copy(h, t, r).start()

    k_new = qkv_ref[0, :, Q_W:Q_W + KV_W]
    v_new = qkv_ref[0, :, Q_W + KV_W:]
    scores = {}
    for h, t, r in sel:
        kvh = h // rep
        kp = kbuf[slot, blocks[h][t] * PAGES_PER_BLOCK + r, pl.ds(kvh, PAGE_SIZE, stride=N_KV_HEADS), :]
        qh = jnp.broadcast_to(q8b[h:h + 1], (SUBLANES, HEAD_DIM))
        scores[h, t, r] = _dot_nt(qh, kp.astype(BF16))[0:1] * scale
    probs, p_new = {}, []
    for h in range(N_HEADS):
        kvh = h // rep
        kn = k_new[:, kvh * HEAD_DIM:(kvh + 1) * HEAD_DIM].astype(BF16).astype(F32)
        s_new = jnp.sum(q8b[h:h + 1].astype(F32) * kn, axis=-1, keepdims=True) * scale
        mine = [scores[h, t, r] for t in range(MOBA_TOPK) for r in range(PAGES_PER_BLOCK)]
        m = s_new
        for s in mine:
            m = jnp.maximum(m, jnp.max(s, axis=-1, keepdims=True))
        es = [jnp.exp(s - m) for s in mine]
        e_new = jnp.exp(s_new - m)
        denom = e_new
        for e in es:
            denom = denom + jnp.sum(e, axis=-1, keepdims=True)
        p_new.append((e_new / denom).astype(BF16).astype(F32))
        for idx, (t, r) in enumerate((t, r) for t in range(MOBA_TOPK) for r in range(PAGES_PER_BLOCK)):
            probs[h, t, r] = (es[idx] / denom).astype(BF16)

    for h, t, r in sel:
        v_copy(h, t, r).wait()

    for h in range(N_HEADS):
        kvh = h // rep
        out = p_new[h] * v_new[:, kvh * HEAD_DIM:(kvh + 1) * HEAD_DIM].astype(BF16).astype(F32)
        for t in range(MOBA_TOPK):
            for r in range(PAGES_PER_BLOCK):
                p8 = jnp.broadcast_to(probs[h, t, r], (SUBLANES, PAGE_SIZE))
                vp = vbuf[(h * MOBA_TOPK + t) * PAGES_PER_BLOCK + r]
                out = out + _dot(p8, vp.astype(BF16))[0:1]
        o_ref[0, :, h * HEAD_DIM:(h + 1) * HEAD_DIM] = out


def _cached_moba(qkv_f3, cache_k_rows, cache_v, pt_flat, layer, n_pool, n_pages):
    n = qkv_f3.shape[0]
    assert SUBLANES == 2 * N_KV_HEADS and n_pages // PAGES_PER_BLOCK <= LANES
    return pl.pallas_call(
        functools.partial(_cached_moba_kernel, layer=layer, n_pool=n_pool, n_pages=n_pages),
        out_shape=jax.ShapeDtypeStruct((n, 1, Q_W), F32),
        grid_spec=pltpu.PrefetchScalarGridSpec(
            num_scalar_prefetch=1,
            grid=(n,),
            in_specs=[pl.BlockSpec((1, 1, QKV_W), lambda b, pt: (b, 0, 0)),
                      pl.BlockSpec(memory_space=pl.ANY),
                      pl.BlockSpec(memory_space=pl.ANY)],
            out_specs=pl.BlockSpec((1, 1, Q_W), lambda b, pt: (b, 0, 0)),
            scratch_shapes=[pltpu.VMEM((2, n_pages, PAGE_ROWS, HEAD_DIM), F32),
                            pltpu.VMEM((N_HEADS * MOBA_TOPK * PAGES_PER_BLOCK, PAGE_SIZE, HEAD_DIM), F32),
                            pltpu.VMEM((LANES * SUBLANES, HEAD_DIM), F32),
                            pltpu.SemaphoreType.DMA((2,)),
                            pltpu.SemaphoreType.DMA(())],
        ),
        compiler_params=_params("arbitrary"),
        name="cached_moba",
    )(pt_flat, qkv_f3, cache_k_rows, cache_v)


def _proj_act_kernel(x_ref, w_ref, *refs, act):
    o_ref = refs[-1]
    acc = _dot(x_ref[...], w_ref[...])
    if act == "gelu":
        o_ref[...] = _gelu(acc)
    elif act == "gelu_ln":
        o_ref[...] = _layer_norm(_gelu(acc), refs[0][...], refs[1][...])
    else:
        o_ref[...] = jax.nn.sigmoid(acc + refs[0][...])


def _proj_act(x, w, w_layer, col_off, ncols, act, vecs, layer, tm, tn):
    m, k = x.shape
    assert m % tm == 0 and ncols % tn == 0 and col_off % tn == 0
    assert act != "gelu_ln" or tn == ncols
    joff = col_off // tn
    return pl.pallas_call(
        functools.partial(_proj_act_kernel, act=act),
        out_shape=jax.ShapeDtypeStruct((m, ncols), F32),
        grid=(ncols // tn, m // tm),
        in_specs=[pl.BlockSpec((tm, k), lambda j, i: (i, 0)),
                  _layer_spec(w_layer, (k, tn), lambda j, i: (0, j + joff))]
        + [_layer_spec(layer, (1, tn), lambda j, i: (0, j)) for _ in vecs],
        out_specs=pl.BlockSpec((tm, tn), lambda j, i: (i, j)),
        compiler_params=_params("parallel", "parallel"),
        name="proj_" + act,
    )(x, w, *vecs)


def _sgu_mix_kernel(gu_ref, vn_ref, ws_ref, bs_ref, s_ref):
    t = SGU_CHUNK
    row = lax.broadcasted_iota(jnp.int32, (t, t), 0)
    col = lax.broadcasted_iota(jnp.int32, (t, t), 1)
    for g in range(SGU_GROUPS):
        cs = slice(g * SGU_GROUP_DIM, (g + 1) * SGU_GROUP_DIM)
        ws = jnp.where(col <= row, ws_ref[g], 0.0).astype(BF16)
        bias = bs_ref[:, g:g + 1]
        for c in range(gu_ref.shape[0] // t):
            rs = slice(c * t, (c + 1) * t)
            mixed = _dot(ws, vn_ref[rs, cs].astype(BF16)) + bias
            s_ref[rs, cs] = (gu_ref[rs, cs] * mixed).astype(s_ref.dtype)


def _prompt_mid_kernel(a_ref, gu_ref, vn_ref, ga_ref, gb_ref, x_ref, mk_ref, mv_ref,
                       ws_ref, bs_ref, wpa_ref, wpb_ref, wo_ref, g1_ref, b1_ref,
                       wxq_ref, wxo_ref, g2_ref, b2_ref, of_ref, ob_ref, s_scr, o_scr):
    _sgu_mix_kernel(gu_ref, vn_ref, ws_ref, bs_ref, s_scr)
    mix = ga_ref[...] * _dot(a_ref[...], wpa_ref[...]) + gb_ref[...] * _dot(s_scr[...], wpb_ref[...])
    x1 = _layer_norm(DN_ALPHA * x_ref[...] + _dot(mix.astype(BF16), wo_ref[...]), g1_ref[...], b1_ref[...])
    qx = _dot(x1.astype(BF16), wxq_ref[...]).astype(BF16)
    scale = X_HEAD_DIM ** -0.5
    for h in range(X_HEADS):
        cs = slice(h * X_HEAD_DIM, (h + 1) * X_HEAD_DIM)
        s = _dot_nt(qx[:, cs], mk_ref[:, cs]) * scale
        e = jnp.exp(s - jnp.max(s, axis=-1, keepdims=True))
        p = (e / jnp.sum(e, axis=-1, keepdims=True)).astype(BF16)
        o_scr[:, cs] = _dot(p, mv_ref[:, cs]).astype(o_scr.dtype)
    y = _layer_norm(DN_ALPHA * x1 + _dot(o_scr[...], wxo_ref[...]), g2_ref[...], b2_ref[...])
    of_ref[...] = y
    ob_ref[...] = y.astype(ob_ref.dtype)


def _prompt_mid(a, gu, vn, gates, x, mk, mv, w_s, b_s_t, w, w_xq, layer, seq, tm):
    m = a.shape[0]
    assert m % tm == 0 and seq % tm == 0 and tm % SGU_CHUNK == 0
    tiles_per_batch = seq // tm

    def const(block):
        return pl.BlockSpec((None,) + block, lambda i: (layer,) + (0,) * len(block), pipeline_mode=pl.Buffered(1))

    mem = pl.BlockSpec((None, N_MEM, X_W), lambda i: (i // tiles_per_batch, 0, 0))
    vec_d = const((1, D_MODEL))
    return pl.pallas_call(
        _prompt_mid_kernel,
        out_shape=[jax.ShapeDtypeStruct((m, D_MODEL), F32), jax.ShapeDtypeStruct((m, D_MODEL), BF16)],
        grid=(m // tm,),
        in_specs=[pl.BlockSpec((tm, Q_W), lambda i: (i, 0)),
                  pl.BlockSpec((tm, SGU_WIDTH), lambda i: (i, 0)),
                  pl.BlockSpec((tm, SGU_WIDTH), lambda i: (i, 0)),
                  pl.BlockSpec((tm, D_MODEL), lambda i: (i, 0)),
                  pl.BlockSpec((tm, D_MODEL), lambda i: (i, 1)),
                  pl.BlockSpec((tm, D_MODEL), lambda i: (i, 0)),
                  mem, mem,
                  const((SGU_GROUPS, SGU_CHUNK, SGU_CHUNK)), const((SGU_CHUNK, SGU_GROUPS)),
                  const((Q_W, D_MODEL)), const((SGU_WIDTH, D_MODEL)),
                  const((D_MODEL, D_MODEL)), vec_d, vec_d,
                  const((D_MODEL, X_W)), const((X_W, D_MODEL)), vec_d, vec_d],
        out_specs=[pl.BlockSpec((tm, D_MODEL), lambda i: (i, 0)), pl.BlockSpec((tm, D_MODEL), lambda i: (i, 0))],
        scratch_shapes=[pltpu.VMEM((tm, SGU_WIDTH), BF16), pltpu.VMEM((tm, X_W), BF16)],
        compiler_params=_params("parallel"),
        name="prompt_mid",
    )(a, gu, vn, gates, gates, x, mk, mv, w_s, b_s_t,
      w["w_pa"], w["w_pb"], w["w_o"], w["ln1_g"], w["ln1_b"],
      w_xq, w["w_xo"], w["ln2_g"], w["ln2_b"])


def _sgu_first_row_kernel(zu_ref, zv_ref, g_ref, b_ref, w0_ref, b0_ref, s_ref, vn_ref):
    vn = _layer_norm(_gelu(zv_ref[...]), g_ref[...], b_ref[...])
    vn_ref[...] = vn
    s_ref[...] = (_gelu(zu_ref[...]) * (vn * w0_ref[...] + b0_ref[...])).astype(s_ref.dtype)


def _sgu_first_rows(rest, ln_g, ln_b, w0, b0):
    m = rest.shape[0]
    vec = pl.BlockSpec((1, SGU_WIDTH), lambda i: (0, 0))
    return pl.pallas_call(
        _sgu_first_row_kernel,
        out_shape=[jax.ShapeDtypeStruct((m, SGU_WIDTH), BF16), jax.ShapeDtypeStruct((m, SGU_WIDTH), F32)],
        grid=(1,),
        in_specs=[pl.BlockSpec((m, SGU_WIDTH), lambda i: (0, 0)),
                  pl.BlockSpec((m, SGU_WIDTH), lambda i: (0, 1)), vec, vec, vec, vec],
        out_specs=[pl.BlockSpec((m, SGU_WIDTH), lambda i: (0, 0)), pl.BlockSpec((m, SGU_WIDTH), lambda i: (0, 0))],
        compiler_params=_params("arbitrary"),
        name="sgu_first_rows",
    )(rest, rest, ln_g, ln_b, w0, b0)


def _merge_kernel(a_ref, s_ref, ga_ref, gb_ref, bg_ref, wa_ref, wb_ref, o_ref):
    g_a = jax.nn.sigmoid(ga_ref[...] + bg_ref[0:1, :])
    g_b = jax.nn.sigmoid(gb_ref[...] + bg_ref[1:2, :])
    mix = g_a * _dot(a_ref[...], wa_ref[...]) + g_b * _dot(s_ref[...], wb_ref[...])
    o_ref[...] = mix.astype(o_ref.dtype)


def _merge(a, s, rest, b_gate, w_pa, w_pb, layer, tm, tn):
    m = a.shape[0]
    tm = min(tm, m)
    ga0 = 2 * SGU_WIDTH // tn
    gb0 = (2 * SGU_WIDTH + D_MODEL) // tn
    return pl.pallas_call(
        _merge_kernel,
        out_shape=jax.ShapeDtypeStruct((m, D_MODEL), BF16),
        grid=(D_MODEL // tn, m // tm),
        in_specs=[pl.BlockSpec((tm, Q_W), lambda j, i: (i, 0)),
                  pl.BlockSpec((tm, SGU_WIDTH), lambda j, i: (i, 0)),
                  pl.BlockSpec((tm, tn), lambda j, i: (i, ga0 + j)),
                  pl.BlockSpec((tm, tn), lambda j, i: (i, gb0 + j)),
                  _layer_spec(layer, (2, tn), lambda j, i: (0, j)),
                  _layer_spec(layer, (Q_W, tn), lambda j, i: (0, j)),
                  _layer_spec(layer, (SGU_WIDTH, tn), lambda j, i: (0, j))],
        out_specs=pl.BlockSpec((tm, tn), lambda j, i: (i, j)),
        compiler_params=_params("parallel", "parallel"),
        name="merge",
    )(a, s, rest, rest, b_gate, w_pa, w_pb)


def _proj_ln_kernel(a_ref, w_ref, x_ref, g_ref, b_ref, of_ref, ob_ref):
    z = DN_ALPHA * x_ref[...] + _dot(a_ref[...], w_ref[...])
    y = _layer_norm(z, g_ref[...], b_ref[...])
    of_ref[...] = y
    ob_ref[...] = y.astype(ob_ref.dtype)


def _proj_ln(a, w, x, g, b, layer, tm):
    m, k = a.shape
    tm = min(tm, m)
    vec = _layer_spec(layer, (1, D_MODEL), lambda i: (0, 0))
    return pl.pallas_call(
        _proj_ln_kernel,
        out_shape=[jax.ShapeDtypeStruct((m, D_MODEL), F32), jax.ShapeDtypeStruct((m, D_MODEL), BF16)],
        grid=(m // tm,),
        in_specs=[pl.BlockSpec((tm, k), lambda i: (i, 0)),
                  _layer_spec(layer, (k, D_MODEL), lambda i: (0, 0)),
                  pl.BlockSpec((tm, D_MODEL), lambda i: (i, 0)), vec, vec],
        out_specs=[pl.BlockSpec((tm, D_MODEL), lambda i: (i, 0)), pl.BlockSpec((tm, D_MODEL), lambda i: (i, 0))],
        compiler_params=_params("parallel"),
        name="proj_ln",
    )(a, w, x, g, b)


_XATTN_SAMPLES_PER_STEP = 4


def _xattn_single_kernel(q_ref, mk_ref, mv_ref, o_ref):
    scale = X_HEAD_DIM ** -0.5
    rows = N_MEM * X_HEADS
    head = lax.broadcasted_iota(jnp.int32, (SUBLANES, rows), 0)
    row_head = jnp.bitwise_and(lax.broadcasted_iota(jnp.int32, (SUBLANES, rows), 1), X_HEADS - 1)
    for b in range(q_ref.shape[0]):
        q8 = _rows_to_sublanes(q_ref, X_HEADS, X_HEAD_DIM, b).astype(BF16)
        s = jnp.where(row_head == head, _dot_nt(q8, mk_ref[b].astype(BF16)) * scale, MASKED)
        e = jnp.exp(s - jnp.max(s, axis=-1, keepdims=True))
        p = (e / jnp.sum(e, axis=-1, keepdims=True)).astype(BF16)
        o = _dot(p, mv_ref[b].astype(BF16))
        for h in range(X_HEADS):
            o_ref[b, :, h * X_HEAD_DIM:(h + 1) * X_HEAD_DIM] = o[h:h + 1].astype(o_ref.dtype)


def _xattn_single(q3, mk_rows, mv_rows, mem_off):
    n = q3.shape[0]
    rows = N_MEM * X_HEADS
    per = _XATTN_SAMPLES_PER_STEP
    assert n % per == 0 and mem_off % per == 0
    return pl.pallas_call(
        _xattn_single_kernel,
        out_shape=jax.ShapeDtypeStruct((n, 1, X_W), F32),
        grid=(n // per,),
        in_specs=[pl.BlockSpec((per, 1, X_W), lambda b: (b, 0, 0)),
                  pl.BlockSpec((per, rows, X_HEAD_DIM), lambda b: (mem_off // per + b, 0, 0)),
                  pl.BlockSpec((per, rows, X_HEAD_DIM), lambda b: (mem_off // per + b, 0, 0))],
        out_specs=pl.BlockSpec((per, 1, X_W), lambda b: (b, 0, 0)),
        compiler_params=_params("parallel"),
        name="xattn_single",
    )(q3, mk_rows, mv_rows)


def _mlp_kernel(xb_ref, xf_ref, wu_ref, wd_ref, g_ref, b_ref, of_ref, ob_ref):
    f = pl.program_id(1)

    @pl.when(f == 0)
    def _():
        of_ref[...] = jnp.zeros_like(of_ref)

    h = jnp.maximum(_dot(xb_ref[...], wu_ref[...]), 0.0)
    of_ref[...] += _dot((h * h).astype(BF16), wd_ref[...])

    @pl.when(f == pl.num_programs(1) - 1)
    def _():
        y = _layer_norm(DN_ALPHA * xf_ref[...] + of_ref[...], g_ref[...], b_ref[...])
        of_ref[...] = y
        ob_ref[...] = y.astype(ob_ref.dtype)


def _mlp(xb, xf, w_up, w_down, w_layer, g, b, layer, tm, tf):
    m = xb.shape[0]
    tm = min(tm, m)
    vec = _layer_spec(layer, (1, D_MODEL), lambda i, f: (0, 0))
    return pl.pallas_call(
        _mlp_kernel,
        out_shape=[jax.ShapeDtypeStruct((m, D_MODEL), F32), jax.ShapeDtypeStruct((m, D_MODEL), BF16)],
        grid=(m // tm, D_FF // tf),
        in_specs=[pl.BlockSpec((tm, D_MODEL), lambda i, f: (i, 0)),
                  pl.BlockSpec((tm, D_MODEL), lambda i, f: (i, 0)),
                  _layer_spec(w_layer, (D_MODEL, tf), lambda i, f: (0, f)),
                  _layer_spec(w_layer, (tf, D_MODEL), lambda i, f: (f, 0)), vec, vec],
        out_specs=[pl.BlockSpec((tm, D_MODEL), lambda i, f: (i, 0)),
                   pl.BlockSpec((tm, D_MODEL), lambda i, f: (i, 0))],
        compiler_params=_params("parallel", "arbitrary"),
        name="mlp",
    )(xb, xf, w_up, w_down, g, b)


def _mlp_cast_kernel(xb_ref, xf_ref, wu_ref, wd_ref, g_ref, b_ref, of_ref, ob_ref, wub_ref, wdb_ref):
    wub_ref[...] = wu_ref[...].astype(BF16)
    wdb_ref[...] = wd_ref[...].astype(BF16)
    _mlp_kernel(xb_ref, xf_ref, wub_ref, wdb_ref, g_ref, b_ref, of_ref, ob_ref)


def _mlp_casting(xb, xf, w_up_f32, w_down_f32, g, b, layer, tf):
    m = xb.shape[0]
    vec = _layer_spec(layer, (1, D_MODEL), lambda i, f: (0, 0))
    rows = pl.BlockSpec((m, D_MODEL), lambda i, f: (0, 0))
    return pl.pallas_call(
        _mlp_cast_kernel,
        out_shape=[jax.ShapeDtypeStruct((m, D_MODEL), F32), jax.ShapeDtypeStruct((m, D_MODEL), BF16),
                   jax.ShapeDtypeStruct((1, D_MODEL, D_FF), BF16), jax.ShapeDtypeStruct((1, D_FF, D_MODEL), BF16)],
        grid=(1, D_FF // tf),
        in_specs=[rows, rows,
                  _layer_spec(layer, (D_MODEL, tf), lambda i, f: (0, f)),
                  _layer_spec(layer, (tf, D_MODEL), lambda i, f: (f, 0)), vec, vec],
        out_specs=[rows, rows,
                   pl.BlockSpec((None, D_MODEL, tf), lambda i, f: (0, 0, f)),
                   pl.BlockSpec((None, tf, D_MODEL), lambda i, f: (0, f, 0))],
        compiler_params=_params("parallel", "arbitrary"),
        name="mlp_casting",
    )(xb, xf, w_up_f32, w_down_f32, g, b)


_MLP_TM = 512
_MLP_TF = 1024
_MLP_CAST_TF = 512
_MID_TM = 256
_PROJ_TM = 1024


def kernel(x_prompt, x_sample, mem_prompt, cache_k, cache_v, cache_mem_k, cache_mem_v, page_table,
           w_in, b_gate, sgu_ln_g, sgu_ln_b, w_s, b_s, w_pa, w_pb, w_o, ln1_g, ln1_b,
           w_xq, w_xk, w_xv, w_xo, ln2_g, ln2_b, w_up, w_down, ln3_g, ln3_b):
    batch, seq, _ = x_prompt.shape
    n_samples, dec_seq, _ = x_sample.shape
    assert dec_seq == 1 and seq % MOBA_BLOCK == 0
    assert N_KV_HEADS & (N_KV_HEADS - 1) == 0 and X_HEADS & (X_HEADS - 1) == 0 and X_HEADS <= SUBLANES
    n_pool = cache_k.shape[1]
    n_pages = page_table.shape[1]
    n_blocks = seq // MOBA_BLOCK
    assert n_pages % PAGES_PER_BLOCK == 0
    mp, ms = batch * seq, n_samples

    pt_flat = page_table.reshape(-1).astype(jnp.int32)
    cache_k_rows = cache_k.reshape(DEPTH * n_pool, PAGE_ROWS, HEAD_DIM)
    cache_mk_rows = cache_mem_k.reshape(DEPTH * n_samples, N_MEM * X_HEADS, X_HEAD_DIM)
    cache_mv_rows = cache_mem_v.reshape(DEPTH * n_samples, N_MEM * X_HEADS, X_HEAD_DIM)
    mem_b = mem_prompt.reshape(batch * N_MEM, D_MODEL).astype(BF16)

    xpf = x_prompt.reshape(mp, D_MODEL)
    xsf = x_sample.reshape(ms, D_MODEL)
    xpb, xsb = xpf.astype(BF16), xsf.astype(BF16)

    w = dict(
        b_gate=b_gate, w_pa=_to_bf16(w_pa), w_pb=_to_bf16(w_pb), w_o=_to_bf16(w_o), w_xo=_to_bf16(w_xo),
        ln1_g=ln1_g[:, None], ln1_b=ln1_b[:, None], ln2_g=ln2_g[:, None], ln2_b=ln2_b[:, None],
        ln3_g=ln3_g[:, None], ln3_b=ln3_b[:, None])
    w_xq_b, w_xk_b, w_xv_b = _to_bf16(w_xq), _to_bf16(w_xk), _to_bf16(w_xv)

    sgu_g3, sgu_b3 = sgu_ln_g[:, None], sgu_ln_b[:, None]
    b_gate_row = b_gate.reshape(DEPTH, 1, 2 * D_MODEL)
    b_s_t = jnp.swapaxes(b_s, 1, 2)

    kp_l, vp_l, mkp_l, mvp_l, ks_l, vs_l, vns_l = [], [], [], [], [], [], []
    for l in range(DEPTH):
        h_s, w_in_b = _matmul_casting(xsb, w_in, l, 1024)
        qkv_s, rest_s = h_s[:, :QKV_W], h_s[:, QKV_W:]
        qkv_s3 = qkv_s.reshape(ms, 1, QKV_W)
        a_s = _cached_moba(qkv_s3, cache_k_rows, cache_v, pt_flat, l, n_pool, n_pages)
        a_s = a_s.reshape(ms, Q_W).astype(BF16)
        w0 = jnp.repeat(w_s[l][:, 0, 0], SGU_GROUP_DIM)[None]
        b0 = jnp.repeat(b_s[l][:, 0], SGU_GROUP_DIM)[None]
        s_s, vn_s = _sgu_first_rows(rest_s, sgu_ln_g[l][None], sgu_ln_b[l][None], w0, b0)
        mix_s = _merge(a_s, s_s, rest_s, w["b_gate"], w["w_pa"], w["w_pb"], l, ms, 1024)
        x1f, x1b = _proj_ln(mix_s, w["w_o"], xsf, w["ln1_g"], w["ln1_b"], l, ms)
        (qx,) = _matmul(x1b, w_xq_b, l, 0, X_W, (F32,), ms, X_W)
        o_s = _xattn_single(qx.reshape(ms, 1, X_W), cache_mk_rows, cache_mv_rows, l * n_samples)
        x2f, x2b = _proj_ln(o_s.reshape(ms, X_W).astype(BF16), w["w_xo"], x1f, w["ln2_g"], w["ln2_b"], l, ms)
        xsf, xsb, w_up_b, w_down_b = _mlp_casting(x2b, x2f, w_up, w_down, w["ln3_g"], w["ln3_b"], l, _MLP_CAST_TF)
        ks_l.append(qkv_s[:, Q_W:Q_W + KV_W].reshape(ms, 1, N_KV_HEADS, HEAD_DIM))
        vs_l.append(qkv_s[:, Q_W + KV_W:].reshape(ms, 1, N_KV_HEADS, HEAD_DIM))
        vns_l.append(vn_s.reshape(ms, 1, SGU_WIDTH))

        mk_f, mk_b = _matmul(mem_b, w_xk_b, l, 0, X_W, (F32, BF16), 512, X_W)
        mv_f, mv_b = _matmul(mem_b, w_xv_b, l, 0, X_W, (F32, BF16), 512, X_W)
        (q_b,) = _matmul(xpb, w_in_b, 0, 0, Q_W, (BF16,), _PROJ_TM, Q_W)
        k_rows, k_b, kmean = _kv_proj(xpb, w_in_b, 0, Q_W, _PROJ_TM)
        v_rows, vt = _v_proj(xpb, w_in_b, 0, Q_W + KV_W, seq, _PROJ_TM)
        gu = _proj_act(xpb, w_in_b, 0, QKV_W, SGU_WIDTH, "gelu", (), l, _PROJ_TM, SGU_WIDTH)
        vn = _proj_act(xpb, w_in_b, 0, QKV_W + SGU_WIDTH, SGU_WIDTH, "gelu_ln", (sgu_g3, sgu_b3), l,
                       _PROJ_TM, SGU_WIDTH)
        gates = _proj_act(xpb, w_in_b, 0, QKV_W + 2 * SGU_WIDTH, 2 * D_MODEL, "gate", (b_gate_row,), l,
                          _PROJ_TM, 1024)
        a = _moba_prompt(q_b, k_b, vt, kmean.reshape(batch, n_blocks, KV_W), batch, seq)
        x2f, x2b = _prompt_mid(a, gu, vn, gates, xpf, mk_b.reshape(batch, N_MEM, X_W),
                               mv_b.reshape(batch, N_MEM, X_W), w_s, b_s_t, w, w_xq_b, l, seq, _MID_TM)
        xpf, xpb = _mlp(x2b, x2f, w_up_b, w_down_b, 0, w["ln3_g"], w["ln3_b"], l, _MLP_TM, _MLP_TF)
        kp_l.append(k_rows.reshape(batch, seq // PAGE_SIZE, PAGE_SIZE, N_KV_HEADS, HEAD_DIM))
        vp_l.append(v_rows.reshape(batch, seq // PAGE_SIZE, PAGE_SIZE, N_KV_HEADS, HEAD_DIM))
        mkp_l.append(mk_f.reshape(batch, N_MEM, X_HEADS, X_HEAD_DIM))
        mvp_l.append(mv_f.reshape(batch, N_MEM, X_HEADS, X_HEAD_DIM))

    return (xpf.reshape(batch, seq, D_MODEL), xsf.reshape(ms, 1, D_MODEL),
            jnp.stack(kp_l), jnp.stack(vp_l), jnp.stack(mkp_l), jnp.stack(mvp_l),
            jnp.stack(ks_l), jnp.stack(vs_l), jnp.stack(vns_l))
```

```python
import functools

import jax
import jax.numpy as jnp
import numpy as np
from jax import lax
from jax.experimental import pallas as pl
from jax.experimental.pallas import tpu as pltpu

D_MODEL = 2048
DEPTH = 2
PAGE_SIZE = 128
N_HEADS = 8
N_KV_HEADS = 4
HEAD_DIM = 128
MOBA_BLOCK = 256
MOBA_TOPK = 3
SGU_WIDTH = 1024
SGU_GROUPS = 8
SGU_GROUP_DIM = SGU_WIDTH // SGU_GROUPS
SGU_CHUNK = 128
N_MEM = 256
X_HEADS = 4
X_HEAD_DIM = 128
D_FF = 4 * D_MODEL
DN_ALPHA = (2 * DEPTH) ** 0.25
LN_EPS = 1e-5
Q_W = N_HEADS * HEAD_DIM
KV_W = N_KV_HEADS * HEAD_DIM
X_W = X_HEADS * X_HEAD_DIM
QKV_W = Q_W + 2 * KV_W
PAGES_PER_BLOCK = MOBA_BLOCK // PAGE_SIZE
PAGE_ROWS = PAGE_SIZE * N_KV_HEADS
MASKED = -1e30

LANES = 128
SUBLANES = 8
VMEM_LIMIT = 56 * 1024 * 1024

BF16 = jnp.bfloat16
F32 = jnp.float32
_NT = (((1,), (1,)), ((), ()))


def _params(*sem):
    return pltpu.CompilerParams(dimension_semantics=sem, vmem_limit_bytes=VMEM_LIMIT)


def _dot(a, b):
    return jnp.dot(a, b, preferred_element_type=F32)


def _dot_nt(a, b):
    return lax.dot_general(a, b, _NT, preferred_element_type=F32)


def _gelu(x):
    c = np.float32(np.sqrt(2 / np.pi))
    return x * (0.5 * (1.0 + jnp.tanh(c * (x + 0.044715 * (x * x * x)))))


def _layer_norm(z, g, b):
    mu = jnp.mean(z, axis=-1, keepdims=True)
    d = z - mu
    var = jnp.mean(d * d, axis=-1, keepdims=True)
    return d * lax.rsqrt(var + LN_EPS) * g + b


def _top_blocks(gate, valid_f, idx_f, axis):
    sel = jnp.zeros(gate.shape, F32)
    for _ in range(MOBA_TOPK):
        m = jnp.max(gate, axis=axis, keepdims=True)
        first = jnp.min(jnp.where(gate == m, idx_f, float(gate.shape[axis])), axis=axis, keepdims=True)
        pick = idx_f == first
        sel = jnp.where(pick, valid_f, sel)
        gate = jnp.where(pick, -jnp.inf, gate)
    return sel


def _rows_to_sublanes(ref, n_rows, width, lead=0):
    sub = lax.broadcasted_iota(jnp.int32, (SUBLANES, width), 0)
    out = jnp.zeros((SUBLANES, width), F32)
    for r in range(n_rows):
        out = jnp.where(sub == r, ref[lead, :, r * width:(r + 1) * width].astype(F32), out)
    return out


def _mm_kernel(x_ref, w_ref, *out_refs):
    acc = _dot(x_ref[...], w_ref[...])
    for o in out_refs:
        o[...] = acc.astype(o.dtype)


def _cast_kernel(x_ref, o_ref):
    o_ref[...] = x_ref[...].astype(o_ref.dtype)


_CAST_BLOCK_BYTES = 4 * 1024 * 1024


def _to_bf16(w):
    d, k, n = w.shape
    rows = d * k
    tr = min(rows, max(SUBLANES, _CAST_BLOCK_BYTES // (4 * n)))
    assert rows % tr == 0
    out = pl.pallas_call(
        _cast_kernel,
        out_shape=jax.ShapeDtypeStruct((rows, n), BF16),
        grid=(rows // tr,),
        in_specs=[pl.BlockSpec((tr, n), lambda i: (i, 0))],
        out_specs=pl.BlockSpec((tr, n), lambda i: (i, 0)),
        compiler_params=_params("parallel"),
        name="to_bf16",
    )(w.reshape(rows, n))
    return out.reshape(d, k, n)


def _layer_spec(layer, block, index):
    return pl.BlockSpec((None,) + block, lambda *g: (layer,) + index(*g))


def _matmul(x, w, layer, col_off, ncols, out_dtypes, tm, tn):
    m, k = x.shape
    tm = min(tm, m)
    tn = min(tn, ncols)
    assert m % tm == 0 and ncols % tn == 0 and col_off % tn == 0
    joff = col_off // tn
    outs = pl.pallas_call(
        _mm_kernel,
        out_shape=[jax.ShapeDtypeStruct((m, ncols), dt) for dt in out_dtypes],
        grid=(ncols // tn, m // tm),
        in_specs=[pl.BlockSpec((tm, k), lambda j, i: (i, 0)),
                  _layer_spec(layer, (k, tn), lambda j, i: (0, j + joff))],
        out_specs=[pl.BlockSpec((tm, tn), lambda j, i: (i, j)) for _ in out_dtypes],
        compiler_params=_params("parallel", "parallel"),
        name="matmul",
    )(x, w)
    return outs


def _mm_cast_kernel(x_ref, w_ref, o_ref, wb_ref):
    wb = w_ref[...].astype(BF16)
    wb_ref[...] = wb
    o_ref[...] = _dot(x_ref[...], wb)


def _matmul_casting(x, w_f32, layer, tn):
    m, k = x.shape
    n = w_f32.shape[2]
    assert n % tn == 0
    return pl.pallas_call(
        _mm_cast_kernel,
        out_shape=[jax.ShapeDtypeStruct((m, n), F32), jax.ShapeDtypeStruct((1, k, n), BF16)],
        grid=(n // tn,),
        in_specs=[pl.BlockSpec((m, k), lambda j: (0, 0)),
                  _layer_spec(layer, (k, tn), lambda j: (0, j))],
        out_specs=[pl.BlockSpec((m, tn), lambda j: (0, j)),
                   pl.BlockSpec((None, k, tn), lambda j: (0, 0, j))],
        compiler_params=_params("parallel"),
        name="matmul_casting",
    )(x, w_f32)


def _kv_proj_kernel(x_ref, w_ref, rows_ref, ob_ref, km_ref):
    acc = _dot(x_ref[...], w_ref[...])
    tm = acc.shape[0]
    for h in range(N_KV_HEADS):
        rows_ref[pl.ds(h, tm, stride=N_KV_HEADS), :] = acc[:, h * HEAD_DIM:(h + 1) * HEAD_DIM]
    ob_ref[...] = acc.astype(ob_ref.dtype)
    for c in range(tm // MOBA_BLOCK):
        blk = acc[c * MOBA_BLOCK:(c + 1) * MOBA_BLOCK]
        km_ref[c] = jnp.sum(blk, axis=0, keepdims=True) * (1.0 / MOBA_BLOCK)


def _kv_proj(x, w, layer, col_off, tm):
    m, k = x.shape
    assert m % tm == 0 and tm % MOBA_BLOCK == 0 and col_off % KV_W == 0
    return pl.pallas_call(
        _kv_proj_kernel,
        out_shape=[jax.ShapeDtypeStruct((m * N_KV_HEADS, HEAD_DIM), F32),
                   jax.ShapeDtypeStruct((m, KV_W), BF16),
                   jax.ShapeDtypeStruct((m // MOBA_BLOCK, 1, KV_W), F32)],
        grid=(m // tm,),
        in_specs=[pl.BlockSpec((tm, k), lambda i: (i, 0)),
                  _layer_spec(layer, (k, KV_W), lambda i: (0, col_off // KV_W))],
        out_specs=[pl.BlockSpec((tm * N_KV_HEADS, HEAD_DIM), lambda i: (i, 0)),
                   pl.BlockSpec((tm, KV_W), lambda i: (i, 0)),
                   pl.BlockSpec((tm // MOBA_BLOCK, 1, KV_W), lambda i: (i, 0, 0))],
        compiler_params=_params("parallel"),
        name="kv_proj",
    )(x, w)


def _v_proj_kernel(x_ref, w_ref, rows_ref, vt_ref):
    acc = _dot(x_ref[...], w_ref[...])
    tm = acc.shape[0]
    for h in range(N_KV_HEADS):
        vh = acc[:, h * HEAD_DIM:(h + 1) * HEAD_DIM]
        rows_ref[pl.ds(h, tm, stride=N_KV_HEADS), :] = vh
        for c in range(tm // MOBA_BLOCK):
            vt_ref[h, c] = vh[c * MOBA_BLOCK:(c + 1) * MOBA_BLOCK].T.astype(vt_ref.dtype)


def _v_proj(x, w, layer, col_off, seq, tm):
    m, k = x.shape
    assert m % tm == 0 and seq % tm == 0 and tm % MOBA_BLOCK == 0 and col_off % KV_W == 0
    tiles_per_batch = seq // tm
    return pl.pallas_call(
        _v_proj_kernel,
        out_shape=[jax.ShapeDtypeStruct((m * N_KV_HEADS, HEAD_DIM), F32),
                   jax.ShapeDtypeStruct((m // seq, N_KV_HEADS, seq // MOBA_BLOCK, HEAD_DIM, MOBA_BLOCK), BF16)],
        grid=(m // tm,),
        in_specs=[pl.BlockSpec((tm, k), lambda i: (i, 0)),
                  _layer_spec(layer, (k, KV_W), lambda i: (0, col_off // KV_W))],
        out_specs=[pl.BlockSpec((tm * N_KV_HEADS, HEAD_DIM), lambda i: (i, 0)),
                   pl.BlockSpec((None, N_KV_HEADS, tm // MOBA_BLOCK, HEAD_DIM, MOBA_BLOCK),
                                lambda i: (i // tiles_per_batch, 0, i % tiles_per_batch, 0, 0))],
        compiler_params=_params("parallel"),
        name="v_proj",
    )(x, w)


_MOBA_KV_PER_STEP = 4


def _moba_prompt_kernel(q_ref, k_ref, vt_ref, km_ref, o_ref, sel_ref, m_ref, l_ref, acc_ref):
    i = pl.program_id(2)
    rep = N_HEADS // N_KV_HEADS
    heads = _MOBA_KV_PER_STEP * rep
    blk = MOBA_BLOCK
    scale_log2e = np.float32(HEAD_DIM ** -0.5 * np.log2(np.e))

    def cols(c):
        return slice(c * HEAD_DIM, (c + 1) * HEAD_DIM)

    qs = [q_ref[:, cols(c)] for c in range(heads)]
    blk_id = lax.broadcasted_iota(jnp.int32, (km_ref.shape[0], blk), 0)
    valid = blk_id < i
    for c in range(heads):
        km = km_ref[:, cols(c // rep)].astype(BF16)
        gate = jnp.where(valid, _dot_nt(km, qs[c]), MASKED)
        sel_ref[c] = _top_blocks(gate, valid.astype(F32), blk_id.astype(F32), 0)

    def attend(j, masks, first):
        kjs = [k_ref[pl.ds(pl.multiple_of(j * blk, blk), blk), cols(g)] for g in range(_MOBA_KV_PER_STEP)]
        scores = [_dot_nt(kjs[c // rep], qs[c]) for c in range(heads)]
        ps, m_news, l_blks = [], [], []
        for c in range(heads):
            s = scores[c] * scale_log2e
            if first:
                s = jnp.where(masks[c], s, MASKED)
                m_new = jnp.max(s, axis=0, keepdims=True)
                p = jnp.exp2(s - m_new)
            else:
                m_blk = jnp.where(masks[c], jnp.max(s, axis=0, keepdims=True), MASKED)
                m_new = jnp.maximum(m_ref[c], m_blk)
                p = jnp.exp2(s - jnp.where(masks[c], m_new, -MASKED))
            l_blks.append(jnp.sum(p, axis=0, keepdims=True))
            ps.append(p.astype(BF16))
            m_news.append(m_new)
        pvs = [_dot(vt_ref[c // rep, j], ps[c]) for c in range(heads)]
        for c in range(heads):
            if first:
                l_ref[c] = l_blks[c]
                acc_ref[c] = pvs[c]
            else:
                a = jnp.exp2(m_ref[c] - m_news[c])
                l_ref[c] = a * l_ref[c] + l_blks[c]
                acc_ref[c] = a * acc_ref[c] + pvs[c]
            m_ref[c] = m_news[c]

    key = lax.broadcasted_iota(jnp.int32, (blk, blk), 0)
    qry = lax.broadcasted_iota(jnp.int32, (blk, blk), 1)
    attend(i, [key <= qry] * heads, True)

    def selected(j):
        return [sel_ref[c, pl.ds(j, 1), :] > 0.5 for c in range(heads)]

    def attend_pair(j0):
        js = (j0, j0 + 1)
        masks = [selected(j) for j in js]
        kjs = [[k_ref[pl.ds(pl.multiple_of(j * blk, blk), blk), cols(g)] for g in range(_MOBA_KV_PER_STEP)]
               for j in js]
        scores = [[_dot_nt(kjs[t][c // rep], qs[c]) * scale_log2e for c in range(heads)] for t in range(2)]
        ps, m_news, l_blks = [], [], []
        for c in range(heads):
            m_new = m_ref[c]
            for t in range(2):
                m_new = jnp.maximum(m_new, jnp.where(masks[t][c], jnp.max(scores[t][c], axis=0, keepdims=True), MASKED))
            p2 = [jnp.exp2(scores[t][c] - jnp.where(masks[t][c], m_new, -MASKED)) for t in range(2)]
            l_blks.append(jnp.sum(p2[0], axis=0, keepdims=True) + jnp.sum(p2[1], axis=0, keepdims=True))
            ps.append([p.astype(BF16) for p in p2])
            m_news.append(m_new)
        pvs = [_dot(vt_ref[c // rep, js[0]], ps[c][0]) + _dot(vt_ref[c // rep, js[1]], ps[c][1]) for c in range(heads)]
        for c in range(heads):
            a = jnp.exp2(m_ref[c] - m_news[c])
            l_ref[c] = a * l_ref[c] + l_blks[c]
            acc_ref[c] = a * acc_ref[c] + pvs[c]
            m_ref[c] = m_news[c]

    def body(t, carry):
        attend_pair(2 * t)
        return carry

    lax.fori_loop(0, i // 2, body, 0)

    @pl.when(i % 2 == 1)
    def _():
        attend(i - 1, selected(i - 1), False)
    for c in range(heads):
        o_ref[:, cols(c)] = (acc_ref[c] / l_ref[c]).T.astype(o_ref.dtype)


def _moba_prompt(q_b, k_b, vt, kmean, batch, seq):
    nq = seq // MOBA_BLOCK
    g = _MOBA_KV_PER_STEP
    heads = g * (N_HEADS // N_KV_HEADS)
    assert N_KV_HEADS % g == 0
    return pl.pallas_call(
        _moba_prompt_kernel,
        out_shape=jax.ShapeDtypeStruct((batch * seq, Q_W), BF16),
        grid=(batch, N_KV_HEADS // g, nq),
        in_specs=[
            pl.BlockSpec((MOBA_BLOCK, heads * HEAD_DIM), lambda b, gg, i: (b * nq + i, gg)),
            pl.BlockSpec((seq, g * HEAD_DIM), lambda b, gg, i: (b, gg)),
            pl.BlockSpec((None, g, nq, HEAD_DIM, MOBA_BLOCK), lambda b, gg, i: (b, gg, 0, 0, 0)),
            pl.BlockSpec((None, nq, g * HEAD_DIM), lambda b, gg, i: (b, 0, gg)),
        ],
        out_specs=pl.BlockSpec((MOBA_BLOCK, heads * HEAD_DIM), lambda b, gg, i: (b * nq + i, gg)),
        scratch_shapes=[pltpu.VMEM((heads, nq, MOBA_BLOCK), F32),
                        pltpu.VMEM((heads, 1, MOBA_BLOCK), F32),
                        pltpu.VMEM((heads, 1, MOBA_BLOCK), F32),
                        pltpu.VMEM((heads, HEAD_DIM, MOBA_BLOCK), F32)],
        compiler_params=_params("parallel", "parallel", "arbitrary"),
        name="moba_prompt",
    )(q_b, k_b, vt, kmean)


def _cached_moba_kernel(pt_ref, qkv_ref, ck_hbm, cv_hbm, o_ref, kbuf, vbuf, km_ref, ksem, vsem,
                        *, layer, n_pool, n_pages):
    b = pl.program_id(0)
    n = pl.num_programs(0)
    slot = b % 2
    rep = N_HEADS // N_KV_HEADS
    n_blocks = n_pages // PAGES_PER_BLOCK
    scale = HEAD_DIM ** -0.5
    base = layer * n_pool

    def k_copy(sample, page_slot, sl):
        page = base + pt_ref[sample * n_pages + page_slot]
        return pltpu.make_async_copy(ck_hbm.at[page], kbuf.at[sl, page_slot], ksem.at[sl])

    def start_keys(sample, sl):
        for p in range(n_pages):
            k_copy(sample, p, sl).start(priority=1)

    @pl.when(b == 0)
    def _():
        km_ref[...] = jnp.zeros_like(km_ref)
        start_keys(0, 0)

    @pl.when(b + 1 < n)
    def _():
        start_keys(b + 1, 1 - slot)

    for p in range(n_pages):
        k_copy(b, p, slot).wait()

    for blk in range(n_blocks):
        tot = jnp.zeros((SUBLANES, HEAD_DIM), F32)
        for r in range(PAGES_PER_BLOCK):
            page = kbuf[slot, PAGES_PER_BLOCK * blk + r]
            tot = tot + jnp.sum(page.reshape(PAGE_ROWS // SUBLANES, SUBLANES, HEAD_DIM), axis=0)
        km_ref[blk * SUBLANES:(blk + 1) * SUBLANES, :] = tot + pltpu.roll(tot, N_KV_HEADS, axis=0)

    q8 = _rows_to_sublanes(qkv_ref, N_HEADS, HEAD_DIM)
    q8b = q8.astype(BF16)
    hrow = lax.broadcasted_iota(jnp.int32, (N_HEADS, LANES), 0)
    lane = lax.broadcasted_iota(jnp.int32, (N_HEADS, LANES), 1)
    gate = jnp.zeros((N_HEADS, LANES), F32)
    for kvh in range(N_KV_HEADS):
        km = km_ref[pl.ds(kvh, LANES, stride=SUBLANES), :] * (1.0 / MOBA_BLOCK)
        gate = jnp.where(hrow >= kvh * rep, _dot_nt(q8b, km.astype(BF16)), gate)
    gate = jnp.where(lane < n_blocks, gate, MASKED)
    lane_f = lane.astype(F32)
    picks = []
    for _ in range(MOBA_TOPK):
        m = jnp.max(gate, axis=-1, keepdims=True)
        first = jnp.min(jnp.where(gate == m, lane_f, float(LANES)), axis=-1, keepdims=True)
        picks.append(first.astype(jnp.int32))
        gate = jnp.where(lane_f == first, -jnp.inf, gate)
    blocks = [[picks[t][h, 0] for t in range(MOBA_TOPK)] for h in range(N_HEADS)]

    def v_copy(h, t, r):
        page = pt_ref[b * n_pages + blocks[h][t] * PAGES_PER_BLOCK + r]
        return pltpu.make_async_copy(cv_hbm.at[layer, page, :, h // rep, :],
                                     vbuf.at[(h * MOBA_TOPK + t) * PAGES_PER_BLOCK + r], vsem)

    sel = [(h, t, r) for h in range(N_HEADS) for t in range(MOBA_TOPK) for r in range(PAGES_PER_BLOCK)]
    for h, t, r in sel:
        v_copy(h, t, r).start()

    k_new = qkv_ref[0, :, Q_W:Q_W + KV_W]
    v_new = qkv_ref[0, :, Q_W + KV_W:]
    scores = {}
    for h, t, r in sel:
        kvh = h // rep
        kp = kbuf[slot, blocks[h][t] * PAGES_PER_BLOCK + r, pl.ds(kvh, PAGE_SIZE, stride=N_KV_HEADS), :]
        qh = jnp.broadcast_to(q8b[h:h + 1], (SUBLANES, HEAD_DIM))
        scores[h, t, r] = _dot_nt(qh, kp.astype(BF16))[0:1] * scale
    probs, p_new = {}, []
    for h in range(N_HEADS):
        kvh = h // rep
        kn = k_new[:, kvh * HEAD_DIM:(kvh + 1) * HEAD_DIM].astype(BF16).astype(F32)
        s_new = jnp.sum(q8b[h:h + 1].astype(F32) * kn, axis=-1, keepdims=True) * scale
        mine = [scores[h, t, r] for t in range(MOBA_TOPK) for r in range(PAGES_PER_BLOCK)]
        m = s_new
        for s in mine:
            m = jnp.maximum(m, jnp.max(s, axis=-1, keepdims=True))
        es = [jnp.exp(s - m) for s in mine]
        e_new = jnp.exp(s_new - m)
        denom = e_new
        for e in es:
            denom = denom + jnp.sum(e, axis=-1, keepdims=True)
        p_new.append((e_new / denom).astype(BF16).astype(F32))
        for idx, (t, r) in enumerate((t, r) for t in range(MOBA_TOPK) for r in range(PAGES_PER_BLOCK)):
            probs[h, t, r] = (es[idx] / denom).astype(BF16)

    for h, t, r in sel:
        v_copy(h, t, r).wait()

    for h in range(N_HEADS):
        kvh = h // rep
        out = p_new[h] * v_new[:, kvh * HEAD_DIM:(kvh + 1) * HEAD_DIM].astype(BF16).astype(F32)
        for t in range(MOBA_TOPK):
            for r in range(PAGES_PER_BLOCK):
                p8 = jnp.broadcast_to(probs[h, t, r], (SUBLANES, PAGE_SIZE))
                vp = vbuf[(h * MOBA_TOPK + t) * PAGES_PER_BLOCK + r]
                out = out + _dot(p8, vp.astype(BF16))[0:1]
        o_ref[0, :, h * HEAD_DIM:(h + 1) * HEAD_DIM] = out


def _cached_moba(qkv_f3, cache_k_rows, cache_v, pt_flat, layer, n_pool, n_pages):
    n = qkv_f3.shape[0]
    assert SUBLANES == 2 * N_KV_HEADS and n_pages // PAGES_PER_BLOCK <= LANES
    return pl.pallas_call(
        functools.partial(_cached_moba_kernel, layer=layer, n_pool=n_pool, n_pages=n_pages),
        out_shape=jax.ShapeDtypeStruct((n, 1, Q_W), F32),
        grid_spec=pltpu.PrefetchScalarGridSpec(
            num_scalar_prefetch=1,
            grid=(n,),
            in_specs=[pl.BlockSpec((1, 1, QKV_W), lambda b, pt: (b, 0, 0)),
                      pl.BlockSpec(memory_space=pl.ANY),
                      pl.BlockSpec(memory_space=pl.ANY)],
            out_specs=pl.BlockSpec((1, 1, Q_W), lambda b, pt: (b, 0, 0)),
            scratch_shapes=[pltpu.VMEM((2, n_pages, PAGE_ROWS, HEAD_DIM), F32),
                            pltpu.VMEM((N_HEADS * MOBA_TOPK * PAGES_PER_BLOCK, PAGE_SIZE, HEAD_DIM), F32),
                            pltpu.VMEM((LANES * SUBLANES, HEAD_DIM), F32),
                            pltpu.SemaphoreType.DMA((2,)),
                            pltpu.SemaphoreType.DMA(())],
        ),
        compiler_params=_params("arbitrary"),
        name="cached_moba",
    )(pt_flat, qkv_f3, cache_k_rows, cache_v)


def _proj_act_kernel(x_ref, w_ref, *refs, act):
    o_ref = refs[-1]
    acc = _dot(x_ref[...], w_ref[...])
    if act == "gelu":
        o_ref[...] = _gelu(acc)
    elif act == "gelu_ln":
        o_ref[...] = _layer_norm(_gelu(acc), refs[0][...], refs[1][...])
    else:
        o_ref[...] = jax.nn.sigmoid(acc + refs[0][...])


def _proj_act(x, w, w_layer, col_off, ncols, act, vecs, layer, tm, tn):
    m, k = x.shape
    assert m % tm == 0 and ncols % tn == 0 and col_off % tn == 0
    assert act != "gelu_ln" or tn == ncols
    joff = col_off // tn
    return pl.pallas_call(
        functools.partial(_proj_act_kernel, act=act),
        out_shape=jax.ShapeDtypeStruct((m, ncols), F32),
        grid=(ncols // tn, m // tm),
        in_specs=[pl.BlockSpec((tm, k), lambda j, i: (i, 0)),
                  _layer_spec(w_layer, (k, tn), lambda j, i: (0, j + joff))]
        + [_layer_spec(layer, (1, tn), lambda j, i: (0, j)) for _ in vecs],
        out_specs=pl.BlockSpec((tm, tn), lambda j, i: (i, j)),
        compiler_params=_params("parallel", "parallel"),
        name="proj_" + act,
    )(x, w, *vecs)


def _sgu_mix_kernel(gu_ref, vn_ref, ws_ref, bs_ref, s_ref):
    t = SGU_CHUNK
    row = lax.broadcasted_iota(jnp.int32, (t, t), 0)
    col = lax.broadcasted_iota(jnp.int32, (t, t), 1)
    for g in range(SGU_GROUPS):
        cs = slice(g * SGU_GROUP_DIM, (g + 1) * SGU_GROUP_DIM)
        ws = jnp.where(col <= row, ws_ref[g], 0.0).astype(BF16)
        bias = bs_ref[:, g:g + 1]
        for c in range(gu_ref.shape[0] // t):
            rs = slice(c * t, (c + 1) * t)
            mixed = _dot(ws, vn_ref[rs, cs].astype(BF16)) + bias
            s_ref[rs, cs] = (gu_ref[rs, cs] * mixed).astype(s_ref.dtype)


def _prompt_mid_kernel(a_ref, gu_ref, vn_ref, ga_ref, gb_ref, x_ref, mk_ref, mv_ref,
                       ws_ref, bs_ref, wpa_ref, wpb_ref, wo_ref, g1_ref, b1_ref,
                       wxq_ref, wxo_ref, g2_ref, b2_ref, of_ref, ob_ref, s_scr, o_scr):
    _sgu_mix_kernel(gu_ref, vn_ref, ws_ref, bs_ref, s_scr)
    mix = ga_ref[...] * _dot(a_ref[...], wpa_ref[...]) + gb_ref[...] * _dot(s_scr[...], wpb_ref[...])
    x1 = _layer_norm(DN_ALPHA * x_ref[...] + _dot(mix.astype(BF16), wo_ref[...]), g1_ref[...], b1_ref[...])
    qx = _dot(x1.astype(BF16), wxq_ref[...]).astype(BF16)
    scale = X_HEAD_DIM ** -0.5
    for h in range(X_HEADS):
        cs = slice(h * X_HEAD_DIM, (h + 1) * X_HEAD_DIM)
        s = _dot_nt(qx[:, cs], mk_ref[:, cs]) * scale
        e = jnp.exp(s - jnp.max(s, axis=-1, keepdims=True))
        p = (e / jnp.sum(e, axis=-1, keepdims=True)).astype(BF16)
        o_scr[:, cs] = _dot(p, mv_ref[:, cs]).astype(o_scr.dtype)
    y = _layer_norm(DN_ALPHA * x1 + _dot(o_scr[...], wxo_ref[...]), g2_ref[...], b2_ref[...])
    of_ref[...] = y
    ob_ref[...] = y.astype(ob_ref.dtype)


def _prompt_mid(a, gu, vn, gates, x, mk, mv, w_s, b_s_t, w, w_xq, layer, seq, tm):
    m = a.shape[0]
    assert m % tm == 0 and seq % tm == 0 and tm % SGU_CHUNK == 0
    tiles_per_batch = seq // tm

    def const(block):
        return pl.BlockSpec((None,) + block, lambda i: (layer,) + (0,) * len(block), pipeline_mode=pl.Buffered(1))

    mem = pl.BlockSpec((None, N_MEM, X_W), lambda i: (i // tiles_per_batch, 0, 0))
    vec_d = const((1, D_MODEL))
    return pl.pallas_call(
        _prompt_mid_kernel,
        out_shape=[jax.ShapeDtypeStruct((m, D_MODEL), F32), jax.ShapeDtypeStruct((m, D_MODEL), BF16)],
        grid=(m // tm,),
        in_specs=[pl.BlockSpec((tm, Q_W), lambda i: (i, 0)),
                  pl.BlockSpec((tm, SGU_WIDTH), lambda i: (i, 0)),
                  pl.BlockSpec((tm, SGU_WIDTH), lambda i: (i, 0)),
                  pl.BlockSpec((tm, D_MODEL), lambda i: (i, 0)),
                  pl.BlockSpec((tm, D_MODEL), lambda i: (i, 1)),
                  pl.BlockSpec((tm, D_MODEL), lambda i: (i, 0)),
                  mem, mem,
                  const((SGU_GROUPS, SGU_CHUNK, SGU_CHUNK)), const((SGU_CHUNK, SGU_GROUPS)),
                  const((Q_W, D_MODEL)), const((SGU_WIDTH, D_MODEL)),
                  const((D_MODEL, D_MODEL)), vec_d, vec_d,
                  const((D_MODEL, X_W)), const((X_W, D_MODEL)), vec_d, vec_d],
        out_specs=[pl.BlockSpec((tm, D_MODEL), lambda i: (i, 0)), pl.BlockSpec((tm, D_MODEL), lambda i: (i, 0))],
        scratch_shapes=[pltpu.VMEM((tm, SGU_WIDTH), BF16), pltpu.VMEM((tm, X_W), BF16)],
        compiler_params=_params("parallel"),
        name="prompt_mid",
    )(a, gu, vn, gates, gates, x, mk, mv, w_s, b_s_t,
      w["w_pa"], w["w_pb"], w["w_o"], w["ln1_g"], w["ln1_b"],
      w_xq, w["w_xo"], w["ln2_g"], w["ln2_b"])


def _sgu_first_row_kernel(zu_ref, zv_ref, g_ref, b_ref, w0_ref, b0_ref, s_ref, vn_ref):
    vn = _layer_norm(_gelu(zv_ref[...]), g_ref[...], b_ref[...])
    vn_ref[...] = vn
    s_ref[...] = (_gelu(zu_ref[...]) * (vn * w0_ref[...] + b0_ref[...])).astype(s_ref.dtype)


def _sgu_first_rows(rest, ln_g, ln_b, w0, b0):
    m = rest.shape[0]
    vec = pl.BlockSpec((1, SGU_WIDTH), lambda i: (0, 0))
    return pl.pallas_call(
        _sgu_first_row_kernel,
        out_shape=[jax.ShapeDtypeStruct((m, SGU_WIDTH), BF16), jax.ShapeDtypeStruct((m, SGU_WIDTH), F32)],
        grid=(1,),
        in_specs=[pl.BlockSpec((m, SGU_WIDTH), lambda i: (0, 0)),
                  pl.BlockSpec((m, SGU_WIDTH), lambda i: (0, 1)), vec, vec, vec, vec],
        out_specs=[pl.BlockSpec((m, SGU_WIDTH), lambda i: (0, 0)), pl.BlockSpec((m, SGU_WIDTH), lambda i: (0, 0))],
        compiler_params=_params("arbitrary"),
        name="sgu_first_rows",
    )(rest, rest, ln_g, ln_b, w0, b0)


def _merge_kernel(a_ref, s_ref, ga_ref, gb_ref, bg_ref, wa_ref, wb_ref, o_ref):
    g_a = jax.nn.sigmoid(ga_ref[...] + bg_ref[0:1, :])
    g_b = jax.nn.sigmoid(gb_ref[...] + bg_ref[1:2, :])
    mix = g_a * _dot(a_ref[...], wa_ref[...]) + g_b * _dot(s_ref[...], wb_ref[...])
    o_ref[...] = mix.astype(o_ref.dtype)


def _merge(a, s, rest, b_gate, w_pa, w_pb, layer, tm, tn):
    m = a.shape[0]
    tm = min(tm, m)
    ga0 = 2 * SGU_WIDTH // tn
    gb0 = (2 * SGU_WIDTH + D_MODEL) // tn
    return pl.pallas_call(
        _merge_kernel,
        out_shape=jax.ShapeDtypeStruct((m, D_MODEL), BF16),
        grid=(D_MODEL // tn, m // tm),
        in_specs=[pl.BlockSpec((tm, Q_W), lambda j, i: (i, 0)),
                  pl.BlockSpec((tm, SGU_WIDTH), lambda j, i: (i, 0)),
                  pl.BlockSpec((tm, tn), lambda j, i: (i, ga0 + j)),
                  pl.BlockSpec((tm, tn), lambda j, i: (i, gb0 + j)),
                  _layer_spec(layer, (2, tn), lambda j, i: (0, j)),
                  _layer_spec(layer, (Q_W, tn), lambda j, i: (0, j)),
                  _layer_spec(layer, (SGU_WIDTH, tn), lambda j, i: (0, j))],
        out_specs=pl.BlockSpec((tm, tn), lambda j, i: (i, j)),
        compiler_params=_params("parallel", "parallel"),
        name="merge",
    )(a, s, rest, rest, b_gate, w_pa, w_pb)


def _proj_ln_kernel(a_ref, w_ref, x_ref, g_ref, b_ref, of_ref, ob_ref):
    z = DN_ALPHA * x_ref[...] + _dot(a_ref[...], w_ref[...])
    y = _layer_norm(z, g_ref[...], b_ref[...])
    of_ref[...] = y
    ob_ref[...] = y.astype(ob_ref.dtype)


def _proj_ln(a, w, x, g, b, layer, tm):
    m, k = a.shape
    tm = min(tm, m)
    vec = _layer_spec(layer, (1, D_MODEL), lambda i: (0, 0))
    return pl.pallas_call(
        _proj_ln_kernel,
        out_shape=[jax.ShapeDtypeStruct((m, D_MODEL), F32), jax.ShapeDtypeStruct((m, D_MODEL), BF16)],
        grid=(m // tm,),
        in_specs=[pl.BlockSpec((tm, k), lambda i: (i, 0)),
                  _layer_spec(layer, (k, D_MODEL), lambda i: (0, 0)),
                  pl.BlockSpec((tm, D_MODEL), lambda i: (i, 0)), vec, vec],
        out_specs=[pl.BlockSpec((tm, D_MODEL), lambda i: (i, 0)), pl.BlockSpec((tm, D_MODEL), lambda i: (i, 0))],
        compiler_params=_params("parallel"),
        name="proj_ln",
    )(a, w, x, g, b)


_XATTN_SAMPLES_PER_STEP = 4


def _xattn_single_kernel(q_ref, mk_ref, mv_ref, o_ref):
    scale = X_HEAD_DIM ** -0.5
    rows = N_MEM * X_HEADS
    head = lax.broadcasted_iota(jnp.int32, (SUBLANES, rows), 0)
    row_head = jnp.bitwise_and(lax.broadcasted_iota(jnp.int32, (SUBLANES, rows), 1), X_HEADS - 1)
    for b in range(q_ref.shape[0]):
        q8 = _rows_to_sublanes(q_ref, X_HEADS, X_HEAD_DIM, b).astype(BF16)
        s = jnp.where(row_head == head, _dot_nt(q8, mk_ref[b].astype(BF16)) * scale, MASKED)
        e = jnp.exp(s - jnp.max(s, axis=-1, keepdims=True))
        p = (e / jnp.sum(e, axis=-1, keepdims=True)).astype(BF16)
        o = _dot(p, mv_ref[b].astype(BF16))
        for h in range(X_HEADS):
            o_ref[b, :, h * X_HEAD_DIM:(h + 1) * X_HEAD_DIM] = o[h:h + 1].astype(o_ref.dtype)


def _xattn_single(q3, mk_rows, mv_rows, mem_off):
    n = q3.shape[0]
    rows = N_MEM * X_HEADS
    per = _XATTN_SAMPLES_PER_STEP
    assert n % per == 0 and mem_off % per == 0
    return pl.pallas_call(
        _xattn_single_kernel,
        out_shape=jax.ShapeDtypeStruct((n, 1, X_W), F32),
        grid=(n // per,),
        in_specs=[pl.BlockSpec((per, 1, X_W), lambda b: (b, 0, 0)),
                  pl.BlockSpec((per, rows, X_HEAD_DIM), lambda b: (mem_off // per + b, 0, 0)),
                  pl.BlockSpec((per, rows, X_HEAD_DIM), lambda b: (mem_off // per + b, 0, 0))],
        out_specs=pl.BlockSpec((per, 1, X_W), lambda b: (b, 0, 0)),
        compiler_params=_params("parallel"),
        name="xattn_single",
    )(q3, mk_rows, mv_rows)


def _mlp_kernel(xb_ref, xf_ref, wu_ref, wd_ref, g_ref, b_ref, of_ref, ob_ref):
    f = pl.program_id(1)

    @pl.when(f == 0)
    def _():
        of_ref[...] = jnp.zeros_like(of_ref)

    h = jnp.maximum(_dot(xb_ref[...], wu_ref[...]), 0.0)
    of_ref[...] += _dot((h * h).astype(BF16), wd_ref[...])

    @pl.when(f == pl.num_programs(1) - 1)
    def _():
        y = _layer_norm(DN_ALPHA * xf_ref[...] + of_ref[...], g_ref[...], b_ref[...])
        of_ref[...] = y
        ob_ref[...] = y.astype(ob_ref.dtype)


def _mlp(xb, xf, w_up, w_down, w_layer, g, b, layer, tm, tf):
    m = xb.shape[0]
    tm = min(tm, m)
    vec = _layer_spec(layer, (1, D_MODEL), lambda i, f: (0, 0))
    return pl.pallas_call(
        _mlp_kernel,
        out_shape=[jax.ShapeDtypeStruct((m, D_MODEL), F32), jax.ShapeDtypeStruct((m, D_MODEL), BF16)],
        grid=(m // tm, D_FF // tf),
        in_specs=[pl.BlockSpec((tm, D_MODEL), lambda i, f: (i, 0)),
                  pl.BlockSpec((tm, D_MODEL), lambda i, f: (i, 0)),
                  _layer_spec(w_layer, (D_MODEL, tf), lambda i, f: (0, f)),
                  _layer_spec(w_layer, (tf, D_MODEL), lambda i, f: (f, 0)), vec, vec],
        out_specs=[pl.BlockSpec((tm, D_MODEL), lambda i, f: (i, 0)),
                   pl.BlockSpec((tm, D_MODEL), lambda i, f: (i, 0))],
        compiler_params=_params("parallel", "arbitrary"),
        name="mlp",
    )(xb, xf, w_up, w_down, g, b)


def _mlp_cast_kernel(xb_ref, xf_ref, wu_ref, wd_ref, g_ref, b_ref, of_ref, ob_ref, wub_ref, wdb_ref):
    wub_ref[...] = wu_ref[...].astype(BF16)
    wdb_ref[...] = wd_ref[...].astype(BF16)
    _mlp_kernel(xb_ref, xf_ref, wub_ref, wdb_ref, g_ref, b_ref, of_ref, ob_ref)


def _mlp_casting(xb, xf, w_up_f32, w_down_f32, g, b, layer, tf):
    m = xb.shape[0]
    vec = _layer_spec(layer, (1, D_MODEL), lambda i, f: (0, 0))
    rows = pl.BlockSpec((m, D_MODEL), lambda i, f: (0, 0))
    return pl.pallas_call(
        _mlp_cast_kernel,
        out_shape=[jax.ShapeDtypeStruct((m, D_MODEL), F32), jax.ShapeDtypeStruct((m, D_MODEL), BF16),
                   jax.ShapeDtypeStruct((1, D_MODEL, D_FF), BF16), jax.ShapeDtypeStruct((1, D_FF, D_MODEL), BF16)],
        grid=(1, D_FF // tf),
        in_specs=[rows, rows,
                  _layer_spec(layer, (D_MODEL, tf), lambda i, f: (0, f)),
                  _layer_spec(layer, (tf, D_MODEL), lambda i, f: (f, 0)), vec, vec],
        out_specs=[rows, rows,
                   pl.BlockSpec((None, D_MODEL, tf), lambda i, f: (0, 0, f)),
                   pl.BlockSpec((None, tf, D_MODEL), lambda i, f: (0, f, 0))],
        compiler_params=_params("parallel", "arbitrary"),
        name="mlp_casting",
    )(xb, xf, w_up_f32, w_down_f32, g, b)


_MLP_TM = 512
_MLP_TF = 1024
_MLP_CAST_TF = 512
_MID_TM = 256
_PROJ_TM = 1024


def kernel(x_prompt, x_sample, mem_prompt, cache_k, cache_v, cache_mem_k, cache_mem_v, page_table,
           w_in, b_gate, sgu_ln_g, sgu_ln_b, w_s, b_s, w_pa, w_pb, w_o, ln1_g, ln1_b,
           w_xq, w_xk, w_xv, w_xo, ln2_g, ln2_b, w_up, w_down, ln3_g, ln3_b):
    batch, seq, _ = x_prompt.shape
    n_samples, dec_seq, _ = x_sample.shape
    assert dec_seq == 1 and seq % MOBA_BLOCK == 0
    assert N_KV_HEADS & (N_KV_HEADS - 1) == 0 and X_HEADS & (X_HEADS - 1) == 0 and X_HEADS <= SUBLANES
    n_pool = cache_k.shape[1]
    n_pages = page_table.shape[1]
    n_blocks = seq // MOBA_BLOCK
    assert n_pages % PAGES_PER_BLOCK == 0
    mp, ms = batch * seq, n_samples

    pt_flat = page_table.reshape(-1).astype(jnp.int32)
    cache_k_rows = cache_k.reshape(DEPTH * n_pool, PAGE_ROWS, HEAD_DIM)
    cache_mk_rows = cache_mem_k.reshape(DEPTH * n_samples, N_MEM * X_HEADS, X_HEAD_DIM)
    cache_mv_rows = cache_mem_v.reshape(DEPTH * n_samples, N_MEM * X_HEADS, X_HEAD_DIM)
    mem_b = mem_prompt.reshape(batch * N_MEM, D_MODEL).astype(BF16)

    xpf = x_prompt.reshape(mp, D_MODEL)
    xsf = x_sample.reshape(ms, D_MODEL)
    xpb, xsb = xpf.astype(BF16), xsf.astype(BF16)

    w = dict(
        b_gate=b_gate, w_pa=_to_bf16(w_pa), w_pb=_to_bf16(w_pb), w_o=_to_bf16(w_o), w_xo=_to_bf16(w_xo),
        ln1_g=ln1_g[:, None], ln1_b=ln1_b[:, None], ln2_g=ln2_g[:, None], ln2_b=ln2_b[:, None],
        ln3_g=ln3_g[:, None], ln3_b=ln3_b[:, None])
    w_xq_b, w_xk_b, w_xv_b = _to_bf16(w_xq), _to_bf16(w_xk), _to_bf16(w_xv)

    sgu_g3, sgu_b3 = sgu_ln_g[:, None], sgu_ln_b[:, None]
    b_gate_row = b_gate.reshape(DEPTH, 1, 2 * D_MODEL)
    b_s_t = jnp.swapaxes(b_s, 1, 2)

    kp_l, vp_l, mkp_l, mvp_l, ks_l, vs_l, vns_l = [], [], [], [], [], [], []
    for l in range(DEPTH):
        h_s, w_in_b = _matmul_casting(xsb, w_in, l, 1024)
        qkv_s, rest_s = h_s[:, :QKV_W], h_s[:, QKV_W:]
        qkv_s3 = qkv_s.reshape(ms, 1, QKV_W)
        a_s = _cached_moba(qkv_s3, cache_k_rows, cache_v, pt_flat, l, n_pool, n_pages)
        a_s = a_s.reshape(ms, Q_W).astype(BF16)
        w0 = jnp.repeat(w_s[l][:, 0, 0], SGU_GROUP_DIM)[None]
        b0 = jnp.repeat(b_s[l][:, 0], SGU_GROUP_DIM)[None]
        s_s, vn_s = _sgu_first_rows(rest_s, sgu_ln_g[l][None], sgu_ln_b[l][None], w0, b0)
        mix_s = _merge(a_s, s_s, rest_s, w["b_gate"], w["w_pa"], w["w_pb"], l, ms, 1024)
        x1f, x1b = _proj_ln(mix_s, w["w_o"], xsf, w["ln1_g"], w["ln1_b"], l, ms)
        (qx,) = _matmul(x1b, w_xq_b, l, 0, X_W, (F32,), ms, X_W)
        o_s = _xattn_single(qx.reshape(ms, 1, X_W), cache_mk_rows, cache_mv_rows, l * n_samples)
        x2f, x2b = _proj_ln(o_s.reshape(ms, X_W).astype(BF16), w["w_xo"], x1f, w["ln2_g"], w["ln2_b"], l, ms)
        xsf, xsb, w_up_b, w_down_b = _mlp_casting(x2b, x2f, w_up, w_down, w["ln3_g"], w["ln3_b"], l, _MLP_CAST_TF)
        ks_l.append(qkv_s[:, Q_W:Q_W + KV_W].reshape(ms, 1, N_KV_HEADS, HEAD_DIM))
        vs_l.append(qkv_s[:, Q_W + KV_W:].reshape(ms, 1, N_KV_HEADS, HEAD_DIM))
        vns_l.append(vn_s.reshape(ms, 1, SGU_WIDTH))

        mk_f, mk_b = _matmul(mem_b, w_xk_b, l, 0, X_W, (F32, BF16), 512, X_W)
        mv_f, mv_b = _matmul(mem_b, w_xv_b, l, 0, X_W, (F32, BF16), 512, X_W)
        (q_b,) = _matmul(xpb, w_in_b, 0, 0, Q_W, (BF16,), _PROJ_TM, Q_W)
        k_rows, k_b, kmean = _kv_proj(xpb, w_in_b, 0, Q_W, _PROJ_TM)
        v_rows, vt = _v_proj(xpb, w_in_b, 0, Q_W + KV_W, seq, _PROJ_TM)
        gu = _proj_act(xpb, w_in_b, 0, QKV_W, SGU_WIDTH, "gelu", (), l, _PROJ_TM, SGU_WIDTH)
        vn = _proj_act(xpb, w_in_b, 0, QKV_W + SGU_WIDTH, SGU_WIDTH, "gelu_ln", (sgu_g3, sgu_b3), l,
                       _PROJ_TM, SGU_WIDTH)
        gates = _proj_act(xpb, w_in_b, 0, QKV_W + 2 * SGU_WIDTH, 2 * D_MODEL, "gate", (b_gate_row,), l,
                          _PROJ_TM, 1024)
        a = _moba_prompt(q_b, k_b, vt, kmean.reshape(batch, n_blocks, KV_W), batch, seq)
        x2f, x2b = _prompt_mid(a, gu, vn, gates, xpf, mk_b.reshape(batch, N_MEM, X_W),
                               mv_b.reshape(batch, N_MEM, X_W), w_s, b_s_t, w, w_xq_b, l, seq, _MID_TM)
        xpf, xpb = _mlp(x2b, x2f, w_up_b, w_down_b, 0, w["ln3_g"], w["ln3_b"], l, _MLP_TM, _MLP_TF)
        kp_l.append(k_rows.reshape(batch, seq // PAGE_SIZE, PAGE_SIZE, N_KV_HEADS, HEAD_DIM))
        vp_l.append(v_rows.reshape(batch, seq // PAGE_SIZE, PAGE_SIZE, N_KV_HEADS, HEAD_DIM))
        mkp_l.append(mk_f.reshape(batch, N_MEM, X_HEADS, X_HEAD_DIM))
        mvp_l.append(mv_f.reshape(batch, N_MEM, X_HEADS, X_HEAD_DIM))

    return (xpf.reshape(batch, seq, D_MODEL), xsf.reshape(ms, 1, D_MODEL),
            jnp.stack(kp_l), jnp.stack(vp_l), jnp.stack(mkp_l), jnp.stack(mvp_l),
            jnp.stack(ks_l), jnp.stack(vs_l), jnp.stack(vns_l))
```

```python
import functools

import jax
import jax.numpy as jnp
import numpy as np
from jax import lax
from jax.experimental import pallas as pl
from jax.experimental.pallas import tpu as pltpu

D_MODEL = 2048
DEPTH = 2
PAGE_SIZE = 128
N_HEADS = 8
N_KV_HEADS = 4
HEAD_DIM = 128
MOBA_BLOCK = 256
MOBA_TOPK = 3
SGU_WIDTH = 1024
SGU_GROUPS = 8
SGU_GROUP_DIM = SGU_WIDTH // SGU_GROUPS
SGU_CHUNK = 128
N_MEM = 256
X_HEADS = 4
X_HEAD_DIM = 128
D_FF = 4 * D_MODEL
DN_ALPHA = (2 * DEPTH) ** 0.25
LN_EPS = 1e-5
Q_W = N_HEADS * HEAD_DIM
KV_W = N_KV_HEADS * HEAD_DIM
X_W = X_HEADS * X_HEAD_DIM
QKV_W = Q_W + 2 * KV_W
PAGES_PER_BLOCK = MOBA_BLOCK // PAGE_SIZE
PAGE_ROWS = PAGE_SIZE * N_KV_HEADS
MASKED = -1e30

LANES = 128
SUBLANES = 8
VMEM_LIMIT = 56 * 1024 * 1024

BF16 = jnp.bfloat16
F32 = jnp.float32
_NT = (((1,), (1,)), ((), ()))


def _params(*sem):
    return pltpu.CompilerParams(dimension_semantics=sem, vmem_limit_bytes=VMEM_LIMIT)


def _dot(a, b):
    return jnp.dot(a, b, preferred_element_type=F32)


def _dot_nt(a, b):
    return lax.dot_general(a, b, _NT, preferred_element_type=F32)


def _gelu(x):
    c = np.float32(np.sqrt(2 / np.pi))
    return x * (0.5 * (1.0 + jnp.tanh(c * (x + 0.044715 * (x * x * x)))))


def _layer_norm(z, g, b):
    mu = jnp.mean(z, axis=-1, keepdims=True)
    d = z - mu
    var = jnp.mean(d * d, axis=-1, keepdims=True)
    return d * lax.rsqrt(var + LN_EPS) * g + b


def _top_blocks(gate, valid_f, idx_f, axis):
    sel = jnp.zeros(gate.shape, F32)
    for _ in range(MOBA_TOPK):
        m = jnp.max(gate, axis=axis, keepdims=True)
        first = jnp.min(jnp.where(gate == m, idx_f, float(gate.shape[axis])), axis=axis, keepdims=True)
        pick = idx_f == first
        sel = jnp.where(pick, valid_f, sel)
        gate = jnp.where(pick, -jnp.inf, gate)
    return sel


def _rows_to_sublanes(ref, n_rows, width, lead=0):
    sub = lax.broadcasted_iota(jnp.int32, (SUBLANES, width), 0)
    out = jnp.zeros((SUBLANES, width), F32)
    for r in range(n_rows):
        out = jnp.where(sub == r, ref[lead, :, r * width:(r + 1) * width].astype(F32), out)
    return out


def _mm_kernel(x_ref, w_ref, *out_refs):
    acc = _dot(x_ref[...], w_ref[...])
    for o in out_refs:
        o[...] = acc.astype(o.dtype)


def _cast_kernel(x_ref, o_ref):
    o_ref[...] = x_ref[...].astype(o_ref.dtype)


_CAST_BLOCK_BYTES = 4 * 1024 * 1024


def _to_bf16(w):
    d, k, n = w.shape
    rows = d * k
    tr = min(rows, max(SUBLANES, _CAST_BLOCK_BYTES // (4 * n)))
    assert rows % tr == 0
    out = pl.pallas_call(
        _cast_kernel,
        out_shape=jax.ShapeDtypeStruct((rows, n), BF16),
        grid=(rows // tr,),
        in_specs=[pl.BlockSpec((tr, n), lambda i: (i, 0))],
        out_specs=pl.BlockSpec((tr, n), lambda i: (i, 0)),
        compiler_params=_params("parallel"),
        name="to_bf16",
    )(w.reshape(rows, n))
    return out.reshape(d, k, n)


def _layer_spec(layer, block, index):
    return pl.BlockSpec((None,) + block, lambda *g: (layer,) + index(*g))


def _matmul(x, w, layer, col_off, ncols, out_dtypes, tm, tn):
    m, k = x.shape
    tm = min(tm, m)
    tn = min(tn, ncols)
    assert m % tm == 0 and ncols % tn == 0 and col_off % tn == 0
    joff = col_off // tn
    outs = pl.pallas_call(
        _mm_kernel,
        out_shape=[jax.ShapeDtypeStruct((m, ncols), dt) for dt in out_dtypes],
        grid=(ncols // tn, m // tm),
        in_specs=[pl.BlockSpec((tm, k), lambda j, i: (i, 0)),
                  _layer_spec(layer, (k, tn), lambda j, i: (0, j + joff))],
        out_specs=[pl.BlockSpec((tm, tn), lambda j, i: (i, j)) for _ in out_dtypes],
        compiler_params=_params("parallel", "parallel"),
        name="matmul",
    )(x, w)
    return outs


def _mm_cast_kernel(x_ref, w_ref, o_ref, wb_ref):
    wb = w_ref[...].astype(BF16)
    wb_ref[...] = wb
    o_ref[...] = _dot(x_ref[...], wb)


def _matmul_casting(x, w_f32, layer, tn):
    m, k = x.shape
    n = w_f32.shape[2]
    assert n % tn == 0
    return pl.pallas_call(
        _mm_cast_kernel,
        out_shape=[jax.ShapeDtypeStruct((m, n), F32), jax.ShapeDtypeStruct((1, k, n), BF16)],
        grid=(n // tn,),
        in_specs=[pl.BlockSpec((m, k), lambda j: (0, 0)),
                  _layer_spec(layer, (k, tn), lambda j: (0, j))],
        out_specs=[pl.BlockSpec((m, tn), lambda j: (0, j)),
                   pl.BlockSpec((None, k, tn), lambda j: (0, 0, j))],
        compiler_params=_params("parallel"),
        name="matmul_casting",
    )(x, w_f32)


def _store_page_rows(acc, prev_ref, rows_ref):
    n_prev = rows_ref.shape[0] - 1
    if n_prev:
        rows_ref[0:n_prev] = prev_ref[...]
    tm = acc.shape[0]
    for h in range(N_KV_HEADS):
        rows_ref[n_prev, pl.ds(h, tm, stride=N_KV_HEADS), :] = acc[:, h * HEAD_DIM:(h + 1) * HEAD_DIM]


def _kv_proj_kernel(x_ref, w_ref, *refs):
    prev_ref = refs[0] if len(refs) == 4 else None
    rows_ref, ob_ref, km_ref = refs[-3:]
    acc = _dot(x_ref[...], w_ref[...])
    _store_page_rows(acc, prev_ref, rows_ref)
    ob_ref[...] = acc.astype(ob_ref.dtype)
    for c in range(acc.shape[0] // MOBA_BLOCK):
        blk = acc[c * MOBA_BLOCK:(c + 1) * MOBA_BLOCK]
        km_ref[c] = jnp.sum(blk, axis=0, keepdims=True) * (1.0 / MOBA_BLOCK)


def _page_row_specs(m, tm, prev_rows):
    n_prev = 0 if prev_rows is None else prev_rows.shape[0]
    rows = tm * N_KV_HEADS
    in_specs = [] if prev_rows is None else [pl.BlockSpec((n_prev, rows, HEAD_DIM), lambda i: (0, i, 0))]
    args = [] if prev_rows is None else [prev_rows]
    out_shape = jax.ShapeDtypeStruct((n_prev + 1, m * N_KV_HEADS, HEAD_DIM), F32)
    return in_specs, args, out_shape, pl.BlockSpec((n_prev + 1, rows, HEAD_DIM), lambda i: (0, i, 0))


def _kv_proj(x, w, layer, col_off, tm, prev_rows):
    m, k = x.shape
    assert m % tm == 0 and tm % MOBA_BLOCK == 0 and col_off % KV_W == 0
    prev_specs, prev_args, rows_shape, rows_spec = _page_row_specs(m, tm, prev_rows)
    return pl.pallas_call(
        _kv_proj_kernel,
        out_shape=[rows_shape,
                   jax.ShapeDtypeStruct((m, KV_W), BF16),
                   jax.ShapeDtypeStruct((m // MOBA_BLOCK, 1, KV_W), F32)],
        grid=(m // tm,),
        in_specs=[pl.BlockSpec((tm, k), lambda i: (i, 0)),
                  _layer_spec(layer, (k, KV_W), lambda i: (0, col_off // KV_W))] + prev_specs,
        out_specs=[rows_spec,
                   pl.BlockSpec((tm, KV_W), lambda i: (i, 0)),
                   pl.BlockSpec((tm // MOBA_BLOCK, 1, KV_W), lambda i: (i, 0, 0))],
        compiler_params=_params("parallel"),
        name="kv_proj",
    )(x, w, *prev_args)


def _v_proj_kernel(x_ref, w_ref, *refs):
    prev_ref = refs[0] if len(refs) == 3 else None
    rows_ref, vt_ref = refs[-2:]
    acc = _dot(x_ref[...], w_ref[...])
    _store_page_rows(acc, prev_ref, rows_ref)
    for h in range(N_KV_HEADS):
        vh = acc[:, h * HEAD_DIM:(h + 1) * HEAD_DIM]
        for c in range(acc.shape[0] // MOBA_BLOCK):
            vt_ref[h, c] = vh[c * MOBA_BLOCK:(c + 1) * MOBA_BLOCK].T.astype(vt_ref.dtype)


def _v_proj(x, w, layer, col_off, seq, tm, prev_rows):
    m, k = x.shape
    assert m % tm == 0 and seq % tm == 0 and tm % MOBA_BLOCK == 0 and col_off % KV_W == 0
    tiles_per_batch = seq // tm
    prev_specs, prev_args, rows_shape, rows_spec = _page_row_specs(m, tm, prev_rows)
    return pl.pallas_call(
        _v_proj_kernel,
        out_shape=[rows_shape,
                   jax.ShapeDtypeStruct((m // seq, N_KV_HEADS, seq // MOBA_BLOCK, HEAD_DIM, MOBA_BLOCK), BF16)],
        grid=(m // tm,),
        in_specs=[pl.BlockSpec((tm, k), lambda i: (i, 0)),
                  _layer_spec(layer, (k, KV_W), lambda i: (0, col_off // KV_W))] + prev_specs,
        out_specs=[rows_spec,
                   pl.BlockSpec((None, N_KV_HEADS, tm // MOBA_BLOCK, HEAD_DIM, MOBA_BLOCK),
                                lambda i: (i // tiles_per_batch, 0, i % tiles_per_batch, 0, 0))],
        compiler_params=_params("parallel"),
        name="v_proj",
    )(x, w, *prev_args)


_MOBA_KV_PER_STEP = 4


def _moba_prompt_kernel(q_ref, k_ref, vt_ref, km_ref, o_ref, sel_ref, m_ref, l_ref, acc_ref):
    i = pl.program_id(2)
    rep = N_HEADS // N_KV_HEADS
    heads = _MOBA_KV_PER_STEP * rep
    blk = MOBA_BLOCK
    scale_log2e = np.float32(HEAD_DIM ** -0.5 * np.log2(np.e))

    def cols(c):
        return slice(c * HEAD_DIM, (c + 1) * HEAD_DIM)

    qs = [q_ref[:, cols(c)] for c in range(heads)]
    blk_id = lax.broadcasted_iota(jnp.int32, (km_ref.shape[0], blk), 0)
    valid = blk_id < i
    for c in range(heads):
        km = km_ref[:, cols(c // rep)].astype(BF16)
        gate = jnp.where(valid, _dot_nt(km, qs[c]), MASKED)
        sel_ref[c] = _top_blocks(gate, valid.astype(F32), blk_id.astype(F32), 0)

    def attend(j, masks, first):
        kjs = [k_ref[pl.ds(pl.multiple_of(j * blk, blk), blk), cols(g)] for g in range(_MOBA_KV_PER_STEP)]
        scores = [_dot_nt(kjs[c // rep], qs[c]) for c in range(heads)]
        ps, m_news, l_blks = [], [], []
        for c in range(heads):
            s = scores[c] * scale_log2e
            if first:
                s = jnp.where(masks[c], s, MASKED)
                m_new = jnp.max(s, axis=0, keepdims=True)
                p = jnp.exp2(s - m_new)
            else:
                m_blk = jnp.where(masks[c], jnp.max(s, axis=0, keepdims=True), MASKED)
                m_new = jnp.maximum(m_ref[c], m_blk)
                p = jnp.exp2(s - jnp.where(masks[c], m_new, -MASKED))
            l_blks.append(jnp.sum(p, axis=0, keepdims=True))
            ps.append(p.astype(BF16))
            m_news.append(m_new)
        pvs = [_dot(vt_ref[c // rep, j], ps[c]) for c in range(heads)]
        for c in range(heads):
            if first:
                l_ref[c] = l_blks[c]
                acc_ref[c] = pvs[c]
            else:
                a = jnp.exp2(m_ref[c] - m_news[c])
                l_ref[c] = a * l_ref[c] + l_blks[c]
                acc_ref[c] = a * acc_ref[c] + pvs[c]
            m_ref[c] = m_news[c]

    key = lax.broadcasted_iota(jnp.int32, (blk, blk), 0)
    qry = lax.broadcasted_iota(jnp.int32, (blk, blk), 1)
    attend(i, [key <= qry] * heads, True)

    def selected(j):
        return [sel_ref[c, pl.ds(j, 1), :] > 0.5 for c in range(heads)]

    def attend_pair(j0):
        js = (j0, j0 + 1)
        masks = [selected(j) for j in js]
        kjs = [[k_ref[pl.ds(pl.multiple_of(j * blk, blk), blk), cols(g)] for g in range(_MOBA_KV_PER_STEP)]
               for j in js]
        scores = [[_dot_nt(kjs[t][c // rep], qs[c]) * scale_log2e for c in range(heads)] for t in range(2)]
        ps, m_news, l_blks = [], [], []
        for c in range(heads):
            m_new = m_ref[c]
            for t in range(2):
                m_new = jnp.maximum(m_new, jnp.where(masks[t][c], jnp.max(scores[t][c], axis=0, keepdims=True), MASKED))
            p2 = [jnp.exp2(scores[t][c] - jnp.where(masks[t][c], m_new, -MASKED)) for t in range(2)]
            l_blks.append(jnp.sum(p2[0], axis=0, keepdims=True) + jnp.sum(p2[1], axis=0, keepdims=True))
            ps.append([p.astype(BF16) for p in p2])
            m_news.append(m_new)
        pvs = [_dot(vt_ref[c // rep, js[0]], ps[c][0]) + _dot(vt_ref[c // rep, js[1]], ps[c][1]) for c in range(heads)]
        for c in range(heads):
            a = jnp.exp2(m_ref[c] - m_news[c])
            l_ref[c] = a * l_ref[c] + l_blks[c]
            acc_ref[c] = a * acc_ref[c] + pvs[c]
            m_ref[c] = m_news[c]

    def body(t, carry):
        attend_pair(2 * t)
        return carry

    lax.fori_loop(0, i // 2, body, 0)

    @pl.when(i % 2 == 1)
    def _():
        attend(i - 1, selected(i - 1), False)
    for c in range(heads):
        o_ref[:, cols(c)] = (acc_ref[c] / l_ref[c]).T.astype(o_ref.dtype)


def _moba_prompt(q_b, k_b, vt, kmean, batch, seq):
    nq = seq // MOBA_BLOCK
    g = _MOBA_KV_PER_STEP
    heads = g * (N_HEADS // N_KV_HEADS)
    assert N_KV_HEADS % g == 0
    return pl.pallas_call(
        _moba_prompt_kernel,
        out_shape=jax.ShapeDtypeStruct((batch * seq, Q_W), BF16),
        grid=(batch, N_KV_HEADS // g, nq),
        in_specs=[
            pl.BlockSpec((MOBA_BLOCK, heads * HEAD_DIM), lambda b, gg, i: (b * nq + i, gg)),
            pl.BlockSpec((seq, g * HEAD_DIM), lambda b, gg, i: (b, gg)),
            pl.BlockSpec((None, g, nq, HEAD_DIM, MOBA_BLOCK), lambda b, gg, i: (b, gg, 0, 0, 0)),
            pl.BlockSpec((None, nq, g * HEAD_DIM), lambda b, gg, i: (b, 0, gg)),
        ],
        out_specs=pl.BlockSpec((MOBA_BLOCK, heads * HEAD_DIM), lambda b, gg, i: (b * nq + i, gg)),
        scratch_shapes=[pltpu.VMEM((heads, nq, MOBA_BLOCK), F32),
                        pltpu.VMEM((heads, 1, MOBA_BLOCK), F32),
                        pltpu.VMEM((heads, 1, MOBA_BLOCK), F32),
                        pltpu.VMEM((heads, HEAD_DIM, MOBA_BLOCK), F32)],
        compiler_params=_params("parallel", "parallel", "arbitrary"),
        name="moba_prompt",
    )(q_b, k_b, vt, kmean)


def _cached_moba_kernel(pt_ref, qkv_ref, ck_hbm, cv_hbm, o_ref, kbuf, vbuf, km_ref, ksem, vsem,
                        *, layer, n_pool, n_pages):
    b = pl.program_id(0)
    n = pl.num_programs(0)
    slot = b % 2
    rep = N_HEADS // N_KV_HEADS
    n_blocks = n_pages // PAGES_PER_BLOCK
    scale = HEAD_DIM ** -0.5
    base = layer * n_pool

    def k_copy(sample, page_slot, sl):
        page = base + pt_ref[sample * n_pages + page_slot]
        return pltpu.make_async_copy(ck_hbm.at[page], kbuf.at[sl, page_slot], ksem.at[sl])

    def start_keys(sample, sl):
        for p in range(n_pages):
            k_copy(sample, p, sl).start(priority=1)

    @pl.when(b == 0)
    def _():
        km_ref[...] = jnp.zeros_like(km_ref)
        start_keys(0, 0)

    @pl.when(b + 1 < n)
    def _():
        start_keys(b + 1, 1 - slot)

    for p in range(n_pages):
        k_copy(b, p, slot).wait()

    for blk in range(n_blocks):
        tot = jnp.zeros((SUBLANES, HEAD_DIM), F32)
        for r in range(PAGES_PER_BLOCK):
            page = kbuf[slot, PAGES_PER_BLOCK * blk + r]
            tot = tot + jnp.sum(page.reshape(PAGE_ROWS // SUBLANES, SUBLANES, HEAD_DIM), axis=0)
        km_ref[blk * SUBLANES:(blk + 1) * SUBLANES, :] = tot + pltpu.roll(tot, N_KV_HEADS, axis=0)

    q8 = _rows_to_sublanes(qkv_ref, N_HEADS, HEAD_DIM)
    q8b = q8.astype(BF16)
    hrow = lax.broadcasted_iota(jnp.int32, (N_HEADS, LANES), 0)
    lane = lax.broadcasted_iota(jnp.int32, (N_HEADS, LANES), 1)
    gate = jnp.zeros((N_HEADS, LANES), F32)
    for kvh in range(N_KV_HEADS):
        km = km_ref[pl.ds(kvh, LANES, stride=SUBLANES), :] * (1.0 / MOBA_BLOCK)
        gate = jnp.where(hrow >= kvh * rep, _dot_nt(q8b, km.astype(BF16)), gate)
    gate = jnp.where(lane < n_blocks, gate, MASKED)
    lane_f = lane.astype(F32)
    picks = []
    for _ in range(MOBA_TOPK):
        m = jnp.max(gate, axis=-1, keepdims=True)
        first = jnp.min(jnp.where(gate == m, lane_f, float(LANES)), axis=-1, keepdims=True)
        picks.append(first.astype(jnp.int32))
        gate = jnp.where(lane_f == first, -jnp.inf, gate)
    blocks = [[picks[t][h, 0] for t in range(MOBA_TOPK)] for h in range(N_HEADS)]

    def v_copy(h, t, r):
        page = pt_ref[b * n_pages + blocks[h][t] * PAGES_PER_BLOCK + r]
        return pltpu.make_async_copy(cv_hbm.at[layer, page, :, h // rep, :],
                                     vbuf.at[(h * MOBA_TOPK + t) * PAGES_PER_BLOCK + r], vsem)

    sel = [(h, t, r) for h in range(N_HEADS) for t in range(MOBA_TOPK) for r in range(PAGES_PER_BLOCK)]
    for h, t, r in sel:
        v_copy(h, t, r).start()

    k_new = qkv_ref[0, :, Q_W:Q_W + KV_W]
    v_new = qkv_ref[0, :, Q_W + KV_W:]
    scores = {}
    for h, t, r in sel:
        kvh = h // rep
        kp = kbuf[slot, blocks[h][t] * PAGES_PER_BLOCK + r, pl.ds(kvh, PAGE_SIZE, stride=N_KV_HEADS), :]
        qh = jnp.broadcast_to(q8b[h:h + 1], (SUBLANES, HEAD_DIM))
        scores[h, t, r] = _dot_nt(qh, kp.astype(BF16))[0:1] * scale
    probs, p_new = {}, []
    for h in range(N_HEADS):
        kvh = h // rep
        kn = k_new[:, kvh * HEAD_DIM:(kvh + 1) * HEAD_DIM].astype(BF16).astype(F32)
        s_new = jnp.sum(q8b[h:h + 1].astype(F32) * kn, axis=-1, keepdims=True) * scale
        mine = [scores[h, t, r] for t in range(MOBA_TOPK) for r in range(PAGES_PER_BLOCK)]
        m = s_new
        for s in mine:
            m = jnp.maximum(m, jnp.max(s, axis=-1, keepdims=True))
        es = [jnp.exp(s - m) for s in mine]
        e_new = jnp.exp(s_new - m)
        denom = e_new
        for e in es:
            denom = denom + jnp.sum(e, axis=-1, keepdims=True)
        p_new.append((e_new / denom).astype(BF16).astype(F32))
        for idx, (t, r) in enumerate((t, r) for t in range(MOBA_TOPK) for r in range(PAGES_PER_BLOCK)):
            probs[h, t, r] = (es[idx] / denom).astype(BF16)

    for h, t, r in sel:
        v_copy(h, t, r).wait()

    for h in range(N_HEADS):
        kvh = h // rep
        out = p_new[h] * v_new[:, kvh * HEAD_DIM:(kvh + 1) * HEAD_DIM].astype(BF16).astype(F32)
        for t in range(MOBA_TOPK):
            for r in range(PAGES_PER_BLOCK):
                p8 = jnp.broadcast_to(probs[h, t, r], (SUBLANES, PAGE_SIZE))
                vp = vbuf[(h * MOBA_TOPK + t) * PAGES_PER_BLOCK + r]
                out = out + _dot(p8, vp.astype(BF16))[0:1]
        o_ref[0, :, h * HEAD_DIM:(h + 1) * HEAD_DIM] = out


def _cached_moba(qkv_f3, cache_k_rows, cache_v, pt_flat, layer, n_pool, n_pages):
    n = qkv_f3.shape[0]
    assert SUBLANES == 2 * N_KV_HEADS and n_pages // PAGES_PER_BLOCK <= LANES
    return pl.pallas_call(
        functools.partial(_cached_moba_kernel, layer=layer, n_pool=n_pool, n_pages=n_pages),
        out_shape=jax.ShapeDtypeStruct((n, 1, Q_W), F32),
        grid_spec=pltpu.PrefetchScalarGridSpec(
            num_scalar_prefetch=1,
            grid=(n,),
            in_specs=[pl.BlockSpec((1, 1, QKV_W), lambda b, pt: (b, 0, 0)),
                      pl.BlockSpec(memory_space=pl.ANY),
                      pl.BlockSpec(memory_space=pl.ANY)],
            out_specs=pl.BlockSpec((1, 1, Q_W), lambda b, pt: (b, 0, 0)),
            scratch_shapes=[pltpu.VMEM((2, n_pages, PAGE_ROWS, HEAD_DIM), F32),
                            pltpu.VMEM((N_HEADS * MOBA_TOPK * PAGES_PER_BLOCK, PAGE_SIZE, HEAD_DIM), F32),
                            pltpu.VMEM((LANES * SUBLANES, HEAD_DIM), F32),
                            pltpu.SemaphoreType.DMA((2,)),
                            pltpu.SemaphoreType.DMA(())],
        ),
        compiler_params=_params("arbitrary"),
        name="cached_moba",
    )(pt_flat, qkv_f3, cache_k_rows, cache_v)


def _proj_act_kernel(x_ref, w_ref, *refs, act):
    o_ref = refs[-1]
    acc = _dot(x_ref[...], w_ref[...])
    if act == "gelu":
        o_ref[...] = _gelu(acc)
    elif act == "gelu_ln":
        o_ref[...] = _layer_norm(_gelu(acc), refs[0][...], refs[1][...])
    else:
        o_ref[...] = jax.nn.sigmoid(acc + refs[0][...])


def _proj_act(x, w, w_layer, col_off, ncols, act, vecs, layer, tm, tn):
    m, k = x.shape
    assert m % tm == 0 and ncols % tn == 0 and col_off % tn == 0
    assert act != "gelu_ln" or tn == ncols
    joff = col_off // tn
    return pl.pallas_call(
        functools.partial(_proj_act_kernel, act=act),
        out_shape=jax.ShapeDtypeStruct((m, ncols), F32),
        grid=(ncols // tn, m // tm),
        in_specs=[pl.BlockSpec((tm, k), lambda j, i: (i, 0)),
                  _layer_spec(w_layer, (k, tn), lambda j, i: (0, j + joff))]
        + [_layer_spec(layer, (1, tn), lambda j, i: (0, j)) for _ in vecs],
        out_specs=pl.BlockSpec((tm, tn), lambda j, i: (i, j)),
        compiler_params=_params("parallel", "parallel"),
        name="proj_" + act,
    )(x, w, *vecs)


def _sgu_mix_kernel(gu_ref, vn_ref, ws_ref, bs_ref, s_ref):
    t = SGU_CHUNK
    row = lax.broadcasted_iota(jnp.int32, (t, t), 0)
    col = lax.broadcasted_iota(jnp.int32, (t, t), 1)
    for g in range(SGU_GROUPS):
        cs = slice(g * SGU_GROUP_DIM, (g + 1) * SGU_GROUP_DIM)
        ws = jnp.where(col <= row, ws_ref[g], 0.0).astype(BF16)
        bias = bs_ref[:, g:g + 1]
        for c in range(gu_ref.shape[0] // t):
            rs = slice(c * t, (c + 1) * t)
            mixed = _dot(ws, vn_ref[rs, cs].astype(BF16)) + bias
            s_ref[rs, cs] = (gu_ref[rs, cs] * mixed).astype(s_ref.dtype)


def _prompt_mid_kernel(a_ref, gu_ref, vn_ref, ga_ref, gb_ref, x_ref, mk_ref, mv_ref,
                       ws_ref, bs_ref, wpa_ref, wpb_ref, wo_ref, g1_ref, b1_ref,
                       wxq_ref, wxo_ref, g2_ref, b2_ref, of_ref, ob_ref, s_scr, o_scr):
    _sgu_mix_kernel(gu_ref, vn_ref, ws_ref, bs_ref, s_scr)
    mix = ga_ref[...] * _dot(a_ref[...], wpa_ref[...]) + gb_ref[...] * _dot(s_scr[...], wpb_ref[...])
    x1 = _layer_norm(DN_ALPHA * x_ref[...] + _dot(mix.astype(BF16), wo_ref[...]), g1_ref[...], b1_ref[...])
    qx = _dot(x1.astype(BF16), wxq_ref[...]).astype(BF16)
    scale = X_HEAD_DIM ** -0.5
    for h in range(X_HEADS):
        cs = slice(h * X_HEAD_DIM, (h + 1) * X_HEAD_DIM)
        s = _dot_nt(qx[:, cs], mk_ref[:, cs]) * scale
        e = jnp.exp(s - jnp.max(s, axis=-1, keepdims=True))
        p = (e / jnp.sum(e, axis=-1, keepdims=True)).astype(BF16)
        o_scr[:, cs] = _dot(p, mv_ref[:, cs]).astype(o_scr.dtype)
    y = _layer_norm(DN_ALPHA * x1 + _dot(o_scr[...], wxo_ref[...]), g2_ref[...], b2_ref[...])
    of_ref[...] = y
    ob_ref[...] = y.astype(ob_ref.dtype)


def _prompt_mid(a, gu, vn, gates, x, mk, mv, w_s, b_s_t, w, w_xq, layer, seq, tm):
    m = a.shape[0]
    assert m % tm == 0 and seq % tm == 0 and tm % SGU_CHUNK == 0
    tiles_per_batch = seq // tm

    def const(block):
        return pl.BlockSpec((None,) + block, lambda i: (layer,) + (0,) * len(block), pipeline_mode=pl.Buffered(1))

    mem = pl.BlockSpec((None, N_MEM, X_W), lambda i: (i // tiles_per_batch, 0, 0))
    vec_d = const((1, D_MODEL))
    return pl.pallas_call(
        _prompt_mid_kernel,
        out_shape=[jax.ShapeDtypeStruct((m, D_MODEL), F32), jax.ShapeDtypeStruct((m, D_MODEL), BF16)],
        grid=(m // tm,),
        in_specs=[pl.BlockSpec((tm, Q_W), lambda i: (i, 0)),
                  pl.BlockSpec((tm, SGU_WIDTH), lambda i: (i, 0)),
                  pl.BlockSpec((tm, SGU_WIDTH), lambda i: (i, 0)),
                  pl.BlockSpec((tm, D_MODEL), lambda i: (i, 0)),
                  pl.BlockSpec((tm, D_MODEL), lambda i: (i, 1)),
                  pl.BlockSpec((tm, D_MODEL), lambda i: (i, 0)),
                  mem, mem,
                  const((SGU_GROUPS, SGU_CHUNK, SGU_CHUNK)), const((SGU_CHUNK, SGU_GROUPS)),
                  const((Q_W, D_MODEL)), const((SGU_WIDTH, D_MODEL)),
                  const((D_MODEL, D_MODEL)), vec_d, vec_d,
                  const((D_MODEL, X_W)), const((X_W, D_MODEL)), vec_d, vec_d],
        out_specs=[pl.BlockSpec((tm, D_MODEL), lambda i: (i, 0)), pl.BlockSpec((tm, D_MODEL), lambda i: (i, 0))],
        scratch_shapes=[pltpu.VMEM((tm, SGU_WIDTH), BF16), pltpu.VMEM((tm, X_W), BF16)],
        compiler_params=_params("parallel"),
        name="prompt_mid",
    )(a, gu, vn, gates, gates, x, mk, mv, w_s, b_s_t,
      w["w_pa"], w["w_pb"], w["w_o"], w["ln1_g"], w["ln1_b"],
      w_xq, w["w_xo"], w["ln2_g"], w["ln2_b"])


def _sgu_first_row_kernel(zu_ref, zv_ref, g_ref, b_ref, w0_ref, b0_ref, s_ref, vn_ref):
    vn = _layer_norm(_gelu(zv_ref[...]), g_ref[...], b_ref[...])
    vn_ref[...] = vn
    s_ref[...] = (_gelu(zu_ref[...]) * (vn * w0_ref[...] + b0_ref[...])).astype(s_ref.dtype)


def _sgu_first_rows(rest, ln_g, ln_b, w0, b0):
    m = rest.shape[0]
    vec = pl.BlockSpec((1, SGU_WIDTH), lambda i: (0, 0))
    return pl.pallas_call(
        _sgu_first_row_kernel,
        out_shape=[jax.ShapeDtypeStruct((m, SGU_WIDTH), BF16), jax.ShapeDtypeStruct((m, SGU_WIDTH), F32)],
        grid=(1,),
        in_specs=[pl.BlockSpec((m, SGU_WIDTH), lambda i: (0, 0)),
                  pl.BlockSpec((m, SGU_WIDTH), lambda i: (0, 1)), vec, vec, vec, vec],
        out_specs=[pl.BlockSpec((m, SGU_WIDTH), lambda i: (0, 0)), pl.BlockSpec((m, SGU_WIDTH), lambda i: (0, 0))],
        compiler_params=_params("arbitrary"),
        name="sgu_first_rows",
    )(rest, rest, ln_g, ln_b, w0, b0)


def _merge_kernel(a_ref, s_ref, ga_ref, gb_ref, bg_ref, wa_ref, wb_ref, o_ref):
    g_a = jax.nn.sigmoid(ga_ref[...] + bg_ref[0:1, :])
    g_b = jax.nn.sigmoid(gb_ref[...] + bg_ref[1:2, :])
    mix = g_a * _dot(a_ref[...], wa_ref[...]) + g_b * _dot(s_ref[...], wb_ref[...])
    o_ref[...] = mix.astype(o_ref.dtype)


def _merge(a, s, rest, b_gate, w_pa, w_pb, layer, tm, tn):
    m = a.shape[0]
    tm = min(tm, m)
    ga0 = 2 * SGU_WIDTH // tn
    gb0 = (2 * SGU_WIDTH + D_MODEL) // tn
    return pl.pallas_call(
        _merge_kernel,
        out_shape=jax.ShapeDtypeStruct((m, D_MODEL), BF16),
        grid=(D_MODEL // tn, m // tm),
        in_specs=[pl.BlockSpec((tm, Q_W), lambda j, i: (i, 0)),
                  pl.BlockSpec((tm, SGU_WIDTH), lambda j, i: (i, 0)),
                  pl.BlockSpec((tm, tn), lambda j, i: (i, ga0 + j)),
                  pl.BlockSpec((tm, tn), lambda j, i: (i, gb0 + j)),
                  _layer_spec(layer, (2, tn), lambda j, i: (0, j)),
                  _layer_spec(layer, (Q_W, tn), lambda j, i: (0, j)),
                  _layer_spec(layer, (SGU_WIDTH, tn), lambda j, i: (0, j))],
        out_specs=pl.BlockSpec((tm, tn), lambda j, i: (i, j)),
        compiler_params=_params("parallel", "parallel"),
        name="merge",
    )(a, s, rest, rest, b_gate, w_pa, w_pb)


def _proj_ln_kernel(a_ref, w_ref, x_ref, g_ref, b_ref, of_ref, ob_ref):
    z = DN_ALPHA * x_ref[...] + _dot(a_ref[...], w_ref[...])
    y = _layer_norm(z, g_ref[...], b_ref[...])
    of_ref[...] = y
    ob_ref[...] = y.astype(ob_ref.dtype)


def _proj_ln(a, w, x, g, b, layer, tm):
    m, k = a.shape
    tm = min(tm, m)
    vec = _layer_spec(layer, (1, D_MODEL), lambda i: (0, 0))
    return pl.pallas_call(
        _proj_ln_kernel,
        out_shape=[jax.ShapeDtypeStruct((m, D_MODEL), F32), jax.ShapeDtypeStruct((m, D_MODEL), BF16)],
        grid=(m // tm,),
        in_specs=[pl.BlockSpec((tm, k), lambda i: (i, 0)),
                  _layer_spec(layer, (k, D_MODEL), lambda i: (0, 0)),
                  pl.BlockSpec((tm, D_MODEL), lambda i: (i, 0)), vec, vec],
        out_specs=[pl.BlockSpec((tm, D_MODEL), lambda i: (i, 0)), pl.BlockSpec((tm, D_MODEL), lambda i: (i, 0))],
        compiler_params=_params("parallel"),
        name="proj_ln",
    )(a, w, x, g, b)


_XATTN_SAMPLES_PER_STEP = 4


def _xattn_single_kernel(q_ref, mk_ref, mv_ref, o_ref):
    scale = X_HEAD_DIM ** -0.5
    rows = N_MEM * X_HEADS
    head = lax.broadcasted_iota(jnp.int32, (SUBLANES, rows), 0)
    row_head = jnp.bitwise_and(lax.broadcasted_iota(jnp.int32, (SUBLANES, rows), 1), X_HEADS - 1)
    for b in range(q_ref.shape[0]):
        q8 = _rows_to_sublanes(q_ref, X_HEADS, X_HEAD_DIM, b).astype(BF16)
        s = jnp.where(row_head == head, _dot_nt(q8, mk_ref[b].astype(BF16)) * scale, MASKED)
        e = jnp.exp(s - jnp.max(s, axis=-1, keepdims=True))
        p = (e / jnp.sum(e, axis=-1, keepdims=True)).astype(BF16)
        o = _dot(p, mv_ref[b].astype(BF16))
        for h in range(X_HEADS):
            o_ref[b, :, h * X_HEAD_DIM:(h + 1) * X_HEAD_DIM] = o[h:h + 1].astype(o_ref.dtype)


def _xattn_single(q3, mk_rows, mv_rows, mem_off):
    n = q3.shape[0]
    rows = N_MEM * X_HEADS
    per = _XATTN_SAMPLES_PER_STEP
    assert n % per == 0 and mem_off % per == 0
    return pl.pallas_call(
        _xattn_single_kernel,
        out_shape=jax.ShapeDtypeStruct((n, 1, X_W), F32),
        grid=(n // per,),
        in_specs=[pl.BlockSpec((per, 1, X_W), lambda b: (b, 0, 0)),
                  pl.BlockSpec((per, rows, X_HEAD_DIM), lambda b: (mem_off // per + b, 0, 0)),
                  pl.BlockSpec((per, rows, X_HEAD_DIM), lambda b: (mem_off // per + b, 0, 0))],
        out_specs=pl.BlockSpec((per, 1, X_W), lambda b: (b, 0, 0)),
        compiler_params=_params("parallel"),
        name="xattn_single",
    )(q3, mk_rows, mv_rows)


def _mlp_kernel(xb_ref, xf_ref, wu_ref, wd_ref, g_ref, b_ref, of_ref, ob_ref):
    f = pl.program_id(1)

    @pl.when(f == 0)
    def _():
        of_ref[...] = jnp.zeros_like(of_ref)

    h = jnp.maximum(_dot(xb_ref[...], wu_ref[...]), 0.0)
    of_ref[...] += _dot((h * h).astype(BF16), wd_ref[...])

    @pl.when(f == pl.num_programs(1) - 1)
    def _():
        y = _layer_norm(DN_ALPHA * xf_ref[...] + of_ref[...], g_ref[...], b_ref[...])
        of_ref[...] = y
        ob_ref[...] = y.astype(ob_ref.dtype)


def _mlp(xb, xf, w_up, w_down, w_layer, g, b, layer, tm, tf):
    m = xb.shape[0]
    tm = min(tm, m)
    vec = _layer_spec(layer, (1, D_MODEL), lambda i, f: (0, 0))
    return pl.pallas_call(
        _mlp_kernel,
        out_shape=[jax.ShapeDtypeStruct((m, D_MODEL), F32), jax.ShapeDtypeStruct((m, D_MODEL), BF16)],
        grid=(m // tm, D_FF // tf),
        in_specs=[pl.BlockSpec((tm, D_MODEL), lambda i, f: (i, 0)),
                  pl.BlockSpec((tm, D_MODEL), lambda i, f: (i, 0)),
                  _layer_spec(w_layer, (D_MODEL, tf), lambda i, f: (0, f)),
                  _layer_spec(w_layer, (tf, D_MODEL), lambda i, f: (f, 0)), vec, vec],
        out_specs=[pl.BlockSpec((tm, D_MODEL), lambda i, f: (i, 0)),
                   pl.BlockSpec((tm, D_MODEL), lambda i, f: (i, 0))],
        compiler_params=_params("parallel", "arbitrary"),
        name="mlp",
    )(xb, xf, w_up, w_down, g, b)


def _mlp_cast_kernel(xb_ref, xf_ref, wu_ref, wd_ref, g_ref, b_ref, of_ref, ob_ref, wub_ref, wdb_ref):
    wub_ref[...] = wu_ref[...].astype(BF16)
    wdb_ref[...] = wd_ref[...].astype(BF16)
    _mlp_kernel(xb_ref, xf_ref, wub_ref, wdb_ref, g_ref, b_ref, of_ref, ob_ref)


def _mlp_casting(xb, xf, w_up_f32, w_down_f32, g, b, layer, tf):
    m = xb.shape[0]
    vec = _layer_spec(layer, (1, D_MODEL), lambda i, f: (0, 0))
    rows = pl.BlockSpec((m, D_MODEL), lambda i, f: (0, 0))
    return pl.pallas_call(
        _mlp_cast_kernel,
        out_shape=[jax.ShapeDtypeStruct((m, D_MODEL), F32), jax.ShapeDtypeStruct((m, D_MODEL), BF16),
                   jax.ShapeDtypeStruct((1, D_MODEL, D_FF), BF16), jax.ShapeDtypeStruct((1, D_FF, D_MODEL), BF16)],
        grid=(1, D_FF // tf),
        in_specs=[rows, rows,
                  _layer_spec(layer, (D_MODEL, tf), lambda i, f: (0, f)),
                  _layer_spec(layer, (tf, D_MODEL), lambda i, f: (f, 0)), vec, vec],
        out_specs=[rows, rows,
                   pl.BlockSpec((None, D_MODEL, tf), lambda i, f: (0, 0, f)),
                   pl.BlockSpec((None, tf, D_MODEL), lambda i, f: (0, f, 0))],
        compiler_params=_params("parallel", "arbitrary"),
        name="mlp_casting",
    )(xb, xf, w_up_f32, w_down_f32, g, b)


_MLP_TM = 512
_MLP_TF = 1024
_MLP_CAST_TF = 512
_MID_TM = 256
_PROJ_TM = 1024


def kernel(x_prompt, x_sample, mem_prompt, cache_k, cache_v, cache_mem_k, cache_mem_v, page_table,
           w_in, b_gate, sgu_ln_g, sgu_ln_b, w_s, b_s, w_pa, w_pb, w_o, ln1_g, ln1_b,
           w_xq, w_xk, w_xv, w_xo, ln2_g, ln2_b, w_up, w_down, ln3_g, ln3_b):
    batch, seq, _ = x_prompt.shape
    n_samples, dec_seq, _ = x_sample.shape
    assert dec_seq == 1 and seq % MOBA_BLOCK == 0
    assert N_KV_HEADS & (N_KV_HEADS - 1) == 0 and X_HEADS & (X_HEADS - 1) == 0 and X_HEADS <= SUBLANES
    n_pool = cache_k.shape[1]
    n_pages = page_table.shape[1]
    n_blocks = seq // MOBA_BLOCK
    assert n_pages % PAGES_PER_BLOCK == 0
    mp, ms = batch * seq, n_samples

    pt_flat = page_table.reshape(-1).astype(jnp.int32)
    cache_k_rows = cache_k.reshape(DEPTH * n_pool, PAGE_ROWS, HEAD_DIM)
    cache_mk_rows = cache_mem_k.reshape(DEPTH * n_samples, N_MEM * X_HEADS, X_HEAD_DIM)
    cache_mv_rows = cache_mem_v.reshape(DEPTH * n_samples, N_MEM * X_HEADS, X_HEAD_DIM)
    mem_b = mem_prompt.reshape(batch * N_MEM, D_MODEL).astype(BF16)

    xpf = x_prompt.reshape(mp, D_MODEL)
    xsf = x_sample.reshape(ms, D_MODEL)
    xpb, xsb = xpf.astype(BF16), xsf.astype(BF16)

    w = dict(
        b_gate=b_gate, w_pa=_to_bf16(w_pa), w_pb=_to_bf16(w_pb), w_o=_to_bf16(w_o), w_xo=_to_bf16(w_xo),
        ln1_g=ln1_g[:, None], ln1_b=ln1_b[:, None], ln2_g=ln2_g[:, None], ln2_b=ln2_b[:, None],
        ln3_g=ln3_g[:, None], ln3_b=ln3_b[:, None])
    w_xq_b, w_xk_b, w_xv_b = _to_bf16(w_xq), _to_bf16(w_xk), _to_bf16(w_xv)

    sgu_g3, sgu_b3 = sgu_ln_g[:, None], sgu_ln_b[:, None]
    b_gate_row = b_gate.reshape(DEPTH, 1, 2 * D_MODEL)
    b_s_t = jnp.swapaxes(b_s, 1, 2)

    k_rows = v_rows = None
    mkp_l, mvp_l, ks_l, vs_l, vns_l = [], [], [], [], []
    for l in range(DEPTH):
        h_s, w_in_b = _matmul_casting(xsb, w_in, l, 1024)
        qkv_s, rest_s = h_s[:, :QKV_W], h_s[:, QKV_W:]
        qkv_s3 = qkv_s.reshape(ms, 1, QKV_W)
        a_s = _cached_moba(qkv_s3, cache_k_rows, cache_v, pt_flat, l, n_pool, n_pages)
        a_s = a_s.reshape(ms, Q_W).astype(BF16)
        w0 = jnp.repeat(w_s[l][:, 0, 0], SGU_GROUP_DIM)[None]
        b0 = jnp.repeat(b_s[l][:, 0], SGU_GROUP_DIM)[None]
        s_s, vn_s = _sgu_first_rows(rest_s, sgu_ln_g[l][None], sgu_ln_b[l][None], w0, b0)
        mix_s = _merge(a_s, s_s, rest_s, w["b_gate"], w["w_pa"], w["w_pb"], l, ms, 1024)
        x1f, x1b = _proj_ln(mix_s, w["w_o"], xsf, w["ln1_g"], w["ln1_b"], l, ms)
        (qx,) = _matmul(x1b, w_xq_b, l, 0, X_W, (F32,), ms, X_W)
        o_s = _xattn_single(qx.reshape(ms, 1, X_W), cache_mk_rows, cache_mv_rows, l * n_samples)
        x2f, x2b = _proj_ln(o_s.reshape(ms, X_W).astype(BF16), w["w_xo"], x1f, w["ln2_g"], w["ln2_b"], l, ms)
        xsf, xsb, w_up_b, w_down_b = _mlp_casting(x2b, x2f, w_up, w_down, w["ln3_g"], w["ln3_b"], l, _MLP_CAST_TF)
        ks_l.append(qkv_s[:, Q_W:Q_W + KV_W].reshape(ms, 1, N_KV_HEADS, HEAD_DIM))
        vs_l.append(qkv_s[:, Q_W + KV_W:].reshape(ms, 1, N_KV_HEADS, HEAD_DIM))
        vns_l.append(vn_s.reshape(ms, 1, SGU_WIDTH))

        mk_f, mk_b = _matmul(mem_b, w_xk_b, l, 0, X_W, (F32, BF16), 512, X_W)
        mv_f, mv_b = _matmul(mem_b, w_xv_b, l, 0, X_W, (F32, BF16), 512, X_W)
        (q_b,) = _matmul(xpb, w_in_b, 0, 0, Q_W, (BF16,), _PROJ_TM, Q_W)
        k_rows, k_b, kmean = _kv_proj(xpb, w_in_b, 0, Q_W, _PROJ_TM, k_rows)
        v_rows, vt = _v_proj(xpb, w_in_b, 0, Q_W + KV_W, seq, _PROJ_TM, v_rows)
        gu = _proj_act(xpb, w_in_b, 0, QKV_W, SGU_WIDTH, "gelu", (), l, _PROJ_TM, SGU_WIDTH)
        vn = _proj_act(xpb, w_in_b, 0, QKV_W + SGU_WIDTH, SGU_WIDTH, "gelu_ln", (sgu_g3, sgu_b3), l,
                       _PROJ_TM, SGU_WIDTH)
        gates = _proj_act(xpb, w_in_b, 0, QKV_W + 2 * SGU_WIDTH, 2 * D_MODEL, "gate", (b_gate_row,), l,
                          _PROJ_TM, 1024)
        a = _moba_prompt(q_b, k_b, vt, kmean.reshape(batch, n_blocks, KV_W), batch, seq)
        x2f, x2b = _prompt_mid(a, gu, vn, gates, xpf, mk_b.reshape(batch, N_MEM, X_W),
                               mv_b.reshape(batch, N_MEM, X_W), w_s, b_s_t, w, w_xq_b, l, seq, _MID_TM)
        xpf, xpb = _mlp(x2b, x2f, w_up_b, w_down_b, 0, w["ln3_g"], w["ln3_b"], l, _MLP_TM, _MLP_TF)
        mkp_l.append(mk_f.reshape(batch, N_MEM, X_HEADS, X_HEAD_DIM))
        mvp_l.append(mv_f.reshape(batch, N_MEM, X_HEADS, X_HEAD_DIM))

    pages = (DEPTH, batch, seq // PAGE_SIZE, PAGE_SIZE, N_KV_HEADS, HEAD_DIM)
    return (xpf.reshape(batch, seq, D_MODEL), xsf.reshape(ms, 1, D_MODEL),
            k_rows.reshape(pages), v_rows.reshape(pages), jnp.stack(mkp_l), jnp.stack(mvp_l),
            jnp.stack(ks_l), jnp.stack(vs_l), jnp.stack(vns_l))
```
